```python
import math
import jax, jax.numpy as jnp
from jax import lax
import numpy as np

D_MODEL = 2048
BATCH = 4
SEQ = 4096
DEPTH = 1

HEAD_DIM = 128
A_Q_HEADS = 8
A_KV_HEADS = 2
A_GROUP = A_Q_HEADS // A_KV_HEADS
ROPE_THETA = 10000.0
GRID_W = 64
B_HEADS = 8
B_QK_DIM = 64
B_V_DIM = 2 * B_QK_DIM
REL_BUCKETS = 32
REL_MAX_DIST = 128
N_EXPERTS = 32
TOP_K = 4
D_FF = D_MODEL
SWIGLU_LIMIT = 7.0
SWIGLU_ALPHA = 1.702
MOE_BLOCK = 128
Q_BLOCK = 128
EPS = 1e-6

A_Q_W = A_Q_HEADS * HEAD_DIM
A_KV_W = A_KV_HEADS * HEAD_DIM
B_QK_W = B_HEADS * 2 * B_QK_DIM
B_V_W = B_HEADS * B_V_DIM
IN_W = A_Q_W + 2 * A_KV_W + 2 * B_QK_W + B_V_W
MIX_W = A_Q_W + B_V_W

kernel_name = "hymba_style_gqa_axialrope_diffattn_moe_encoder"


def lambda_init_for(layer_idx):
    return 0.8 - 0.6 * math.exp(-0.3 * layer_idx)


def rms_norm(x, g):
    xf = x.astype(jnp.float32)
    y = xf * lax.rsqrt(jnp.mean(xf * xf, axis=-1, keepdims=True) + EPS)
    return (y * g.astype(jnp.float32)).astype(x.dtype)


def axial_rope_tables(rows):
    row = jnp.repeat(jnp.arange(rows, dtype=jnp.float32), GRID_W)
    col = jnp.tile(jnp.arange(GRID_W, dtype=jnp.float32), rows)
    axis_dim = HEAD_DIM // 2
    inv = ROPE_THETA ** (-jnp.arange(0, axis_dim, 2, dtype=jnp.float32) / axis_dim)
    ang = jnp.concatenate([row[:, None] * inv, col[:, None] * inv], axis=-1)
    return jnp.cos(ang), jnp.sin(ang)


def apply_rope(x, cos, sin):
    xr = x.reshape(x.shape[:-1] + (HEAD_DIM // 2, 2)).astype(jnp.float32)
    c = cos[:, None, :]
    s = sin[:, None, :]
    x0, x1 = xr[..., 0], xr[..., 1]
    out = jnp.stack([x0 * c - x1 * s, x0 * s + x1 * c], axis=-1)
    return out.reshape(x.shape).astype(x.dtype)


def rel_bucket(rp):
    nb = REL_BUCKETS // 2
    max_exact = nb // 2
    ret = jnp.where(rp > 0, nb, 0)
    n = jnp.abs(rp)
    nf = jnp.maximum(n, 1).astype(jnp.float32)
    large = max_exact + (jnp.log(nf / max_exact) / math.log(REL_MAX_DIST / max_exact)
                         * (nb - max_exact)).astype(jnp.int32)
    large = jnp.minimum(large, nb - 1)
    return ret + jnp.where(n < max_exact, n, large)


def gqa_axial_attention(q, k, v):
    B, S = q.shape[:2]
    nb = S // Q_BLOCK
    qg = q.reshape(B, nb, Q_BLOCK, A_KV_HEADS, A_GROUP, HEAD_DIM).transpose(1, 0, 2, 3, 4, 5)
    scale = HEAD_DIM ** -0.5

    def block(qb):
        s = jnp.einsum('bqkgd,bskd->bkgqs', qb, k).astype(jnp.float32) * scale
        p = jax.nn.softmax(s, axis=-1).astype(v.dtype)
        return jnp.einsum('bkgqs,bskd->bqkgd', p, v)

    o = lax.map(block, qg)
    return o.transpose(1, 0, 2, 3, 4, 5).reshape(B, S, A_Q_W)


def diff_attention(q1, q2, k1, k2, v, rel_table, lam, subln_g, lambda_init):
    B, S = q1.shape[:2]
    nb = S // Q_BLOCK
    q1b = q1.reshape(B, nb, Q_BLOCK, B_HEADS, B_QK_DIM).transpose(1, 0, 2, 3, 4)
    q2b = q2.reshape(B, nb, Q_BLOCK, B_HEADS, B_QK_DIM).transpose(1, 0, 2, 3, 4)
    starts = jnp.arange(nb, dtype=jnp.int32) * Q_BLOCK
    k_pos = jnp.arange(S, dtype=jnp.int32)
    scale = B_QK_DIM ** -0.5

    def block(args):
        qa, qb, start = args
        q_pos = start + jnp.arange(Q_BLOCK, dtype=jnp.int32)
        bucket = rel_bucket(k_pos[None, :] - q_pos[:, None])
        bias = rel_table.astype(jnp.float32)[bucket].transpose(2, 0, 1)
        s1 = jnp.einsum('bqhd,bshd->bhqs', qa, k1).astype(jnp.float32) * scale + bias
        s2 = jnp.einsum('bqhd,bshd->bhqs', qb, k2).astype(jnp.float32) * scale + bias
        attn = jax.nn.softmax(s1, axis=-1) - lam * jax.nn.softmax(s2, axis=-1)
        o = jnp.einsum('bhqs,bshd->bqhd', attn.astype(v.dtype), v)
        return rms_norm(o, subln_g) * (1.0 - lambda_init)

    o = lax.map(block, (q1b, q2b, starts))
    return o.transpose(1, 0, 2, 3, 4).reshape(B, S, B_V_W)


def routed_experts(h, w_router, b_router, w_gu, b_gu, w_down, b_down):
    T, D = h.shape
    logits = (h @ w_router).astype(jnp.float32) + b_router.astype(jnp.float32)
    top_val, top_idx = lax.top_k(logits, TOP_K)
    gates = jax.nn.softmax(top_val, axis=-1)
    n_assign = T * TOP_K
    e_flat = top_idx.reshape(-1)
    tok_flat = jnp.arange(n_assign, dtype=jnp.int32) // TOP_K
    g_flat = gates.reshape(-1)
    order = jnp.argsort(e_flat)
    e_sorted = e_flat[order]
    tok_sorted = tok_flat[order]
    g_sorted = g_flat[order]
    counts = jnp.bincount(e_flat, length=N_EXPERTS)
    padded = (counts + MOE_BLOCK - 1) // MOE_BLOCK * MOE_BLOCK
    start = jnp.cumsum(counts) - counts
    pad_end = jnp.cumsum(padded)
    pad_start = pad_end - padded
    dest = pad_start[e_sorted] + (jnp.arange(n_assign, dtype=jnp.int32) - start[e_sorted])
    n_rows = n_assign + N_EXPERTS * MOE_BLOCK
    n_blocks = n_rows // MOE_BLOCK
    row_tok = jnp.full((n_rows,), T, jnp.int32).at[dest].set(tok_sorted)
    row_gate = jnp.zeros((n_rows,), jnp.float32).at[dest].set(g_sorted)
    block_expert = jnp.minimum(
        jnp.searchsorted(pad_end, jnp.arange(n_blocks, dtype=jnp.int32) * MOE_BLOCK, side='right'),
        N_EXPERTS - 1)
    h_pad = jnp.concatenate([h, jnp.zeros((1, D), h.dtype)], axis=0)

    def expert_block(args):
        toks, e = args
        xb = h_pad[toks]
        gu = xb @ w_gu[e] + b_gu[e]
        g, u = jnp.split(gu, 2, axis=-1)
        g = jnp.minimum(g, SWIGLU_LIMIT)
        u = jnp.clip(u, -SWIGLU_LIMIT, SWIGLU_LIMIT)
        glu = g * jax.nn.sigmoid(SWIGLU_ALPHA * g)
        return ((u + 1.0) * glu) @ w_down[e] + b_down[e]

    out = lax.map(expert_block, (row_tok.reshape(n_blocks, MOE_BLOCK), block_expert))
    out = out.reshape(n_rows, D) * row_gate[:, None].astype(out.dtype)
    y = jnp.zeros((T + 1, D), out.dtype).at[row_tok].add(out)
    return y[:T]


def setup_inputs(seed: int = 0) -> dict:
    key = jax.random.key(seed)
    ks = jax.random.split(key, 24)
    f32 = jnp.float32
    nrm = lambda k, shape, s: jax.random.normal(k, shape, f32) * s
    gain = lambda k, shape: 1.0 + 0.01 * jax.random.normal(k, shape, f32)
    return {
        "x": nrm(ks[0], (BATCH, SEQ, D_MODEL), 1.0),
        "c": nrm(ks[1], (BATCH, D_MODEL), 1.0),
        "rel_bias_table": nrm(ks[2], (REL_BUCKETS, B_HEADS), 0.5),
        "w_ada": nrm(ks[3], (DEPTH, D_MODEL, 6 * D_MODEL), 0.01),
        "b_ada": nrm(ks[4], (DEPTH, 6 * D_MODEL), 0.01),
        "norm_attn": gain(ks[5], (DEPTH, D_MODEL)),
        "norm_ffn": gain(ks[6], (DEPTH, D_MODEL)),
        "w_in": nrm(ks[7], (DEPTH, D_MODEL, IN_W), D_MODEL ** -0.5),
        "w_out": nrm(ks[8], (DEPTH, MIX_W, D_MODEL), MIX_W ** -0.5),
        "a_q_norm": gain(ks[9], (DEPTH, HEAD_DIM)),
        "a_k_norm": gain(ks[10], (DEPTH, HEAD_DIM)),
        "b_q_norm": gain(ks[11], (DEPTH, B_QK_DIM)),
        "b_k_norm": gain(ks[12], (DEPTH, B_QK_DIM)),
        "lambda_q1": nrm(ks[13], (DEPTH, B_QK_DIM), 0.1),
        "lambda_k1": nrm(ks[14], (DEPTH, B_QK_DIM), 0.1),
        "lambda_q2": nrm(ks[15], (DEPTH, B_QK_DIM), 0.1),
        "lambda_k2": nrm(ks[16], (DEPTH, B_QK_DIM), 0.1),
        "b_subln": gain(ks[17], (DEPTH, B_V_DIM)),
        "w_router": nrm(ks[18], (DEPTH, D_MODEL, N_EXPERTS), D_MODEL ** -0.5),
        "b_router": nrm(ks[19], (DEPTH, N_EXPERTS), 0.01),
        "w_gu": nrm(ks[20], (DEPTH, N_EXPERTS, D_MODEL, 2 * D_FF), D_MODEL ** -0.5),
        "b_gu": nrm(ks[21], (DEPTH, N_EXPERTS, 2 * D_FF), 0.01),
        "w_down": nrm(ks[22], (DEPTH, N_EXPERTS, D_FF, D_MODEL), D_FF ** -0.5),
        "b_down": nrm(ks[23], (DEPTH, N_EXPERTS, D_MODEL), 0.01),
    }


def reference(x, c, rel_bias_table, w_ada, b_ada, norm_attn, norm_ffn, w_in, w_out,
              a_q_norm, a_k_norm, b_q_norm, b_k_norm, lambda_q1, lambda_k1, lambda_q2,
              lambda_k2, b_subln, w_router, b_router, w_gu, b_gu, w_down, b_down):
    B, S, D = x.shape
    rows = S // GRID_W
    cos, sin = axial_rope_tables(rows)
    c_act = jax.nn.silu(c)
    for l in range(DEPTH):
        lam_init = lambda_init_for(l)
        mod = (c_act @ w_ada[l] + b_ada[l])[:, None, :]
        sh_a, sc_a, g_a, sh_f, sc_f, g_f = jnp.split(mod, 6, axis=-1)

        h = rms_norm(x, norm_attn[l]) * (1.0 + sc_a) + sh_a
        proj = h @ w_in[l]
        o0 = A_Q_W
        o1 = o0 + A_KV_W
        o2 = o1 + A_KV_W
        o3 = o2 + B_QK_W
        o4 = o3 + B_QK_W
        qa = proj[..., :o0].reshape(B, S, A_Q_HEADS, HEAD_DIM)
        ka = proj[..., o0:o1].reshape(B, S, A_KV_HEADS, HEAD_DIM)
        va = proj[..., o1:o2].reshape(B, S, A_KV_HEADS, HEAD_DIM)
        qa = apply_rope(rms_norm(qa, a_q_norm[l]), cos, sin)
        ka = apply_rope(rms_norm(ka, a_k_norm[l]), cos, sin)
        out_a = gqa_axial_attention(qa, ka, va)
        qb = rms_norm(proj[..., o2:o3].reshape(B, S, B_HEADS, 2, B_QK_DIM), b_q_norm[l])
        kb = rms_norm(proj[..., o3:o4].reshape(B, S, B_HEADS, 2, B_QK_DIM), b_k_norm[l])
        vb = proj[..., o4:].reshape(B, S, B_HEADS, B_V_DIM)
        lam = (jnp.exp(jnp.sum(lambda_q1[l].astype(jnp.float32) * lambda_k1[l].astype(jnp.float32)))
               - jnp.exp(jnp.sum(lambda_q2[l].astype(jnp.float32) * lambda_k2[l].astype(jnp.float32)))
               + lam_init)
        out_b = diff_attention(qb[..., 0, :], qb[..., 1, :], kb[..., 0, :], kb[..., 1, :], vb,
                               rel_bias_table, lam, b_subln[l], lam_init)
        mix = jnp.concatenate([out_a, out_b], axis=-1) @ w_out[l]
        x = x + g_a * mix

        h = rms_norm(x, norm_ffn[l]) * (1.0 + sc_f) + sh_f
        y = routed_experts(h.reshape(B * S, D), w_router[l], b_router[l], w_gu[l], b_gu[l],
                           w_down[l], b_down[l]).reshape(B, S, D)
        x = x + g_f * y
    return x
```

```python
import functools
import math

import jax
import jax.numpy as jnp
from jax import lax
from jax.experimental import pallas as pl
from jax.experimental.pallas import tpu as pltpu

F32 = jnp.float32
BF16 = jnp.bfloat16
HIGHEST = lax.Precision.HIGHEST

D_MODEL = 2048
HEAD_DIM = 128
A_Q_HEADS = 8
A_KV_HEADS = 2
A_GROUP = A_Q_HEADS // A_KV_HEADS
ROPE_THETA = 10000.0
GRID_W = 64
B_HEADS = 8
B_QK_DIM = 64
REL_BUCKETS = 32
N_EXPERTS = 32
TOP_K = 4
D_FF = D_MODEL
SWIGLU_LIMIT = 7.0
SWIGLU_ALPHA = 1.702
EPS = 1e-6
LAMBDA_INIT = 0.8 - 0.6 * math.exp(-0.3 * 0)

A_Q_W = A_Q_HEADS * HEAD_DIM
A_KV_W = A_KV_HEADS * HEAD_DIM
B_QK_W = B_HEADS * 2 * B_QK_DIM
B_V_W = B_HEADS * HEAD_DIM
OFF_KA = A_Q_W
OFF_VA = OFF_KA + A_KV_W
OFF_QB = OFF_VA + A_KV_W
OFF_KB = OFF_QB + B_QK_W
OFF_VB = OFF_KB + B_QK_W
IN_W = OFF_VB + B_V_W

LOG2E = 1.4426950408889634
LANES = 128
EXPERT_LANES = LANES
NEG_BIG = -1e30

VMEM_LIMIT = 56 * 1024 * 1024

ADA_TN = 1536
PROJ_TM = 512
PROJ_SEG = 512
ATT_TQ = 256
ATT_A_TK = 512
POST_TM = 256
ROUTE_TB = 512
ROW_TT = 256
EXP_TM = 1024
EXP_SUB = 256
EXP_TF = 256


def _cparams(sem, vmem=VMEM_LIMIT):
    return pltpu.CompilerParams(dimension_semantics=sem, vmem_limit_bytes=vmem)


def _ada_kernel(c_ref, w_ref, b_ref, o_ref):
    c = c_ref[...]
    ca = c * jax.nn.sigmoid(c)
    o_ref[...] = jnp.dot(ca, w_ref[...], preferred_element_type=F32,
                         precision=HIGHEST) + b_ref[...]


def _ada_mod(c_pad, w_ada, b_ada):
    rows, d = c_pad.shape
    n = w_ada.shape[1]
    return pl.pallas_call(
        _ada_kernel,
        grid=(n // ADA_TN,),
        in_specs=[pl.BlockSpec((rows, d), lambda j: (0, 0)),
                  pl.BlockSpec((d, ADA_TN), lambda j: (0, j)),
                  pl.BlockSpec((1, ADA_TN), lambda j: (0, j))],
        out_specs=pl.BlockSpec((rows, ADA_TN), lambda j: (0, j)),
        out_shape=jax.ShapeDtypeStruct((rows, n), F32),
        compiler_params=_cparams(("arbitrary",)),
        name="ada_mod",
    )(c_pad, w_ada, b_ada)


def _inproj_kernel(x_ref, mod_ref, ng_ref, w_ref, cos_ref, se_ref, so_ref,
                   gaq_ref, gak_ref, gbq_ref, gbk_ref, o_ref):
    x = x_ref[...]
    ms = jnp.mean(x * x, axis=-1, keepdims=True)
    y = x * lax.rsqrt(ms + EPS) * ng_ref[...]
    h = y * (1.0 + mod_ref[0, 1:2, :]) + mod_ref[0, 0:1, :]
    hb = h.astype(BF16)

    cos = cos_ref[...]
    sin_even = se_ref[...]
    sin_odd = so_ref[...]
    lane = lax.broadcasted_iota(jnp.int32, (1, LANES), 1)
    low_half = lane < B_QK_DIM

    def rope(t):
        return (t * cos + pltpu.roll(t, LANES - 1, 1) * sin_even
                + pltpu.roll(t, 1, 1) * sin_odd)

    def norm_head(t, g):
        m = jnp.mean(t * t, axis=-1, keepdims=True)
        return t * lax.rsqrt(m + EPS) * g

    def norm_halves(t, g):
        sq = t * t
        s_lo = jnp.sum(jnp.where(low_half, sq, 0.0), axis=-1, keepdims=True)
        s_hi = jnp.sum(jnp.where(low_half, 0.0, sq), axis=-1, keepdims=True)
        m = jnp.where(low_half, s_lo, s_hi) * (1.0 / B_QK_DIM)
        return t * lax.rsqrt(m + EPS) * g

    a_scale = (HEAD_DIM ** -0.5) * LOG2E
    b_scale = (B_QK_DIM ** -0.5) * LOG2E
    for seg in range(IN_W // PROJ_SEG):
        acc = jnp.dot(hb, w_ref[:, seg * PROJ_SEG:(seg + 1) * PROJ_SEG],
                      preferred_element_type=F32)
        for j in range(PROJ_SEG // LANES):
            col = seg * PROJ_SEG + j * LANES
            t = acc[:, j * LANES:(j + 1) * LANES]
            if col < OFF_KA:
                t = rope(norm_head(t, gaq_ref[...])) * a_scale
            elif col < OFF_VA:
                t = rope(norm_head(t, gak_ref[...]))
            elif col < OFF_QB:
                pass
            elif col < OFF_KB:
                t = norm_halves(t, gbq_ref[...]) * b_scale
            elif col < OFF_VB:
                t = norm_halves(t, gbk_ref[...])
            o_ref[:, col:col + LANES] = t.astype(BF16)


def _in_proj(x2, mod3, norm_g, w_in_bf, cos_rep, sin_even, sin_odd, gaq, gak, gbq, gbk, seq):
    t, d = x2.shape
    tm = min(PROJ_TM, seq)
    per_b = seq // tm
    vec = lambda: pl.BlockSpec((1, LANES), lambda i: (0, 0))
    tab = lambda: pl.BlockSpec((tm, LANES), lambda i: (i % per_b, 0))
    return pl.pallas_call(
        _inproj_kernel,
        grid=(t // tm,),
        in_specs=[pl.BlockSpec((tm, d), lambda i: (i, 0)),
                  pl.BlockSpec((1, 6, d), lambda i: (i // per_b, 0, 0)),
                  pl.BlockSpec((1, d), lambda i: (0, 0)),
                  pl.BlockSpec((d, IN_W), lambda i: (0, 0), pipeline_mode=pl.Buffered(1)),
                  tab(), tab(), tab(), vec(), vec(), vec(), vec()],
        out_specs=pl.BlockSpec((tm, IN_W), lambda i: (i, 0)),
        out_shape=jax.ShapeDtypeStruct((t, IN_W), BF16),
        compiler_params=_cparams(("arbitrary",)),
        name="in_proj",
    )(x2, mod3, norm_g, w_in_bf, cos_rep, sin_even, sin_odd, gaq, gak, gbq, gbk)


def _bias_band_kernel(tab_ref, o_ref, *, tq):
    h = pl.program_id(0)
    qq = lax.broadcasted_iota(jnp.int32, (tq, 3 * tq), 0)
    kk = lax.broadcasted_iota(jnp.int32, (tq, 3 * tq), 1)
    d = kk - tq - qq
    n = jnp.abs(d)
    n2 = n * n
    large = jnp.full_like(n, 8)
    for j in range(1, 8):
        large = large + (n2 >= 64 * 2 ** j).astype(jnp.int32)
    bucket = jnp.where(n < 8, n, large) + jnp.where(d > 0, 16, 0)
    acc = jnp.zeros((tq, 3 * tq), F32)
    for b in range(REL_BUCKETS):
        acc = jnp.where(bucket == b, tab_ref[b, h], acc)
    o_ref[0] = acc * LOG2E


def _bias_band(rel_table, tq):
    return pl.pallas_call(
        functools.partial(_bias_band_kernel, tq=tq),
        grid=(B_HEADS,),
        in_specs=[pl.BlockSpec(memory_space=pltpu.SMEM)],
        out_specs=pl.BlockSpec((1, tq, 3 * tq), lambda h: (h, 0, 0)),
        out_shape=jax.ShapeDtypeStruct((B_HEADS, tq, 3 * tq), F32),
        compiler_params=_cparams(("arbitrary",)),
        name="bias_band",
    )(rel_table)


def _flash_step(s, shift, v_c, m_ref, l_ref, acc_ref, idx):
    tk = s.shape[1]
    m_prev = m_ref[idx]
    m_cur = jnp.max(s, axis=1, keepdims=True) + shift
    m_new = jnp.maximum(m_prev, m_cur)
    alpha = jnp.exp2(m_prev - m_new)
    off = m_new - shift
    p = jnp.exp2(s - jnp.tile(off, (1, tk // LANES)))
    l_ref[idx] = alpha * l_ref[idx] + jnp.sum(p, axis=1, keepdims=True)
    acc_ref[idx] = alpha * acc_ref[idx] + jnp.dot(
        p.astype(BF16), v_c, preferred_element_type=F32)
    m_ref[idx] = m_new


_NT = (((1,), (1,)), ((), ()))


def _attn_a_kernel(q_ref, k_ref, v_ref, o_ref, m_ref, l_ref, acc_ref, *, tk):
    seq = k_ref.shape[0]
    m_ref[...] = jnp.full(m_ref.shape, -jnp.inf, F32)
    l_ref[...] = jnp.zeros(l_ref.shape, F32)
    acc_ref[...] = jnp.zeros(acc_ref.shape, F32)

    def chunk(c, carry):
        r0 = pl.multiple_of(c * tk, tk)
        k_c = k_ref[pl.ds(r0, tk), :]
        v_c = v_ref[pl.ds(r0, tk), :]
        for g in range(A_GROUP):
            q = q_ref[:, g * HEAD_DIM:(g + 1) * HEAD_DIM]
            s = lax.dot_general(q, k_c, _NT, preferred_element_type=F32)
            _flash_step(s, 0.0, v_c, m_ref, l_ref, acc_ref, g)
        return carry

    lax.fori_loop(0, seq // tk, chunk, 0)
    for g in range(A_GROUP):
        o_ref[:, g * HEAD_DIM:(g + 1) * HEAD_DIM] = (acc_ref[g] / l_ref[g]).astype(BF16)


def _attn_a(proj, batch, seq):
    t = proj.shape[0]
    tq = min(ATT_TQ, seq)
    tk = min(ATT_A_TK, seq)
    nq = seq // tq
    gw = A_GROUP * HEAD_DIM
    return pl.pallas_call(
        functools.partial(_attn_a_kernel, tk=tk),
        grid=(batch, A_KV_HEADS, nq),
        in_specs=[pl.BlockSpec((tq, gw), lambda b, g, i: (b * nq + i, g)),
                  pl.BlockSpec((seq, HEAD_DIM), lambda b, g, i: (b, OFF_KA // HEAD_DIM + g)),
                  pl.BlockSpec((seq, HEAD_DIM), lambda b, g, i: (b, OFF_VA // HEAD_DIM + g))],
        out_specs=pl.BlockSpec((tq, gw), lambda b, g, i: (b * nq + i, g)),
        out_shape=jax.ShapeDtypeStruct((t, A_Q_W), BF16),
        scratch_shapes=[pltpu.VMEM((A_GROUP, tq, LANES), F32),
                        pltpu.VMEM((A_GROUP, tq, LANES), F32),
                        pltpu.VMEM((A_GROUP, tq, HEAD_DIM), F32)],
        compiler_params=_cparams(("arbitrary", "arbitrary", "arbitrary")),
        name="attn_a",
    )(proj, proj, proj)


def _attn_b_kernel(tab_ref, q_ref, k_ref, v_ref, band_ref, lq1_ref, lk1_ref, lq2_ref,
                   lk2_ref, sg_ref, o_ref, m_ref, l_ref, acc_ref):
    tq = q_ref.shape[0]
    seq = k_ref.shape[0]
    n_chunks = seq // tq
    h = pl.program_id(1)
    i = pl.program_id(2)
    m_ref[...] = jnp.full(m_ref.shape, -jnp.inf, F32)
    l_ref[...] = jnp.zeros(l_ref.shape, F32)
    acc_ref[...] = jnp.zeros(acc_ref.shape, F32)

    q = q_ref[...]
    lane = lax.broadcasted_iota(jnp.int32, (1, LANES), 1)
    low_half = lane < B_QK_DIM
    zero = jnp.zeros_like(q)
    q1 = jnp.where(low_half, q, zero)
    q2 = jnp.where(low_half, zero, q)
    far_left = tab_ref[REL_BUCKETS // 2 - 1, h] * LOG2E
    far_right = tab_ref[REL_BUCKETS - 1, h] * LOG2E

    def step(c, shift, bias):
        r0 = pl.multiple_of(c * tq, tq)
        k_c = k_ref[pl.ds(r0, tq), :]
        v_c = v_ref[pl.ds(r0, tq), :]
        s1 = lax.dot_general(q1, k_c, _NT, preferred_element_type=F32)
        s2 = lax.dot_general(q2, k_c, _NT, preferred_element_type=F32)
        if bias is not None:
            s1 = s1 + bias
            s2 = s2 + bias
        _flash_step(s1, shift, v_c, m_ref, l_ref, acc_ref, 0)
        _flash_step(s2, shift, v_c, m_ref, l_ref, acc_ref, 1)

    def left(c, carry):
        step(c, far_left, None)
        return carry

    def right(c, carry):
        step(c, far_right, None)
        return carry

    lax.fori_loop(0, jnp.maximum(i - 1, 0), left, 0)
    for jj in range(3):
        c = i - 1 + jj

        @pl.when((c >= 0) & (c < n_chunks))
        def _():
            step(c, 0.0, band_ref[0, :, jj * tq:(jj + 1) * tq])

    lax.fori_loop(jnp.minimum(i + 2, n_chunks), n_chunks, right, 0)

    lam1 = jnp.exp(jnp.sum(lq1_ref[...] * lk1_ref[...], axis=-1, keepdims=True))
    lam2 = jnp.exp(jnp.sum(lq2_ref[...] * lk2_ref[...], axis=-1, keepdims=True))
    lam = lam1 - lam2 + LAMBDA_INIT
    o = acc_ref[0] / l_ref[0] - lam * (acc_ref[1] / l_ref[1])
    ms = jnp.mean(o * o, axis=-1, keepdims=True)
    o = o * lax.rsqrt(ms + EPS) * sg_ref[...] * (1.0 - LAMBDA_INIT)
    o_ref[...] = o.astype(BF16)


def _attn_b(rel_table, proj, band, lq1, lk1, lq2, lk2, subln, batch, seq):
    t = proj.shape[0]
    tq = band.shape[1]
    nq = seq // tq
    small = lambda w: pl.BlockSpec((1, w), lambda b, h, i, tab: (0, 0))
    grid_spec = pltpu.PrefetchScalarGridSpec(
        num_scalar_prefetch=1,
        grid=(batch, B_HEADS, nq),
        in_specs=[pl.BlockSpec((tq, HEAD_DIM), lambda b, h, i, tab: (b * nq + i, OFF_QB // HEAD_DIM + h)),
                  pl.BlockSpec((seq, HEAD_DIM), lambda b, h, i, tab: (b, OFF_KB // HEAD_DIM + h)),
                  pl.BlockSpec((seq, HEAD_DIM), lambda b, h, i, tab: (b, OFF_VB // HEAD_DIM + h)),
                  pl.BlockSpec((1, tq, 3 * tq), lambda b, h, i, tab: (h, 0, 0)),
                  small(B_QK_DIM), small(B_QK_DIM), small(B_QK_DIM), small(B_QK_DIM),
                  small(HEAD_DIM)],
        out_specs=pl.BlockSpec((tq, HEAD_DIM), lambda b, h, i, tab: (b * nq + i, h)),
        scratch_shapes=[pltpu.VMEM((2, tq, LANES), F32),
                        pltpu.VMEM((2, tq, LANES), F32),
                        pltpu.VMEM((2, tq, HEAD_DIM), F32)],
    )
    return pl.pallas_call(
        _attn_b_kernel,
        grid_spec=grid_spec,
        out_shape=jax.ShapeDtypeStruct((t, B_V_W), BF16),
        compiler_params=_cparams(("arbitrary", "arbitrary", "arbitrary")),
        name="attn_b",
    )(rel_table, proj, proj, proj, band, lq1, lk1, lq2, lk2, subln)


def _post_attn_kernel(oa_ref, ob_ref, x_ref, mod_ref, woa_ref, wob_ref, ng_ref, wr_ref,
                      br_ref, x1_ref, h2_ref, lg_ref):
    mix = jnp.dot(oa_ref[...], woa_ref[...], preferred_element_type=F32)
    mix = mix + jnp.dot(ob_ref[...], wob_ref[...], preferred_element_type=F32)
    x1 = x_ref[...] + mod_ref[0, 2:3, :] * mix
    x1_ref[...] = x1
    ms = jnp.mean(x1 * x1, axis=-1, keepdims=True)
    y = x1 * lax.rsqrt(ms + EPS) * ng_ref[...]
    h2 = y * (1.0 + mod_ref[0, 4:5, :]) + mod_ref[0, 3:4, :]
    h2_ref[...] = h2
    lg_ref[...] = jnp.dot(h2, wr_ref[...], preferred_element_type=F32,
                          precision=HIGHEST) + br_ref[...]


def _post_attn(out_a, out_b, x2, mod3, w_out_bf, norm_g, w_router_pad, b_router_pad, seq):
    t, d = x2.shape
    tm = min(POST_TM, seq)
    per_b = seq // tm
    half = w_out_bf.shape[0] // 2
    return pl.pallas_call(
        _post_attn_kernel,
        grid=(t // tm,),
        in_specs=[pl.BlockSpec((tm, half), lambda i: (i, 0)),
                  pl.BlockSpec((tm, half), lambda i: (i, 0)),
                  pl.BlockSpec((tm, d), lambda i: (i, 0)),
                  pl.BlockSpec((1, 6, d), lambda i: (i // per_b, 0, 0)),
                  pl.BlockSpec((half, d), lambda i: (0, 0), pipeline_mode=pl.Buffered(1)),
                  pl.BlockSpec((half, d), lambda i: (1, 0), pipeline_mode=pl.Buffered(1)),
                  pl.BlockSpec((1, d), lambda i: (0, 0)),
                  pl.BlockSpec((d, EXPERT_LANES), lambda i: (0, 0)),
                  pl.BlockSpec((1, EXPERT_LANES), lambda i: (0, 0))],
        out_specs=[pl.BlockSpec((tm, d), lambda i: (i, 0)),
                   pl.BlockSpec((tm, d), lambda i: (i, 0)),
                   pl.BlockSpec((tm, EXPERT_LANES), lambda i: (i, 0))],
        out_shape=[jax.ShapeDtypeStruct((t, d), F32),
                   jax.ShapeDtypeStruct((t, d), F32),
                   jax.ShapeDtypeStruct((t, EXPERT_LANES), F32)],
        compiler_params=_cparams(("arbitrary",)),
        name="post_attn",
    )(out_a, out_b, x2, mod3, w_out_bf, w_out_bf, norm_g, w_router_pad, b_router_pad)


def _route_kernel(lg_ref, pos_ref, gate_ref, cnt_ref, counts, start, carry):
    phase = pl.program_id(0)
    j = pl.program_id(1)
    tb = lg_ref.shape[0]
    lane_i = lax.broadcasted_iota(jnp.int32, (tb, EXPERT_LANES), 1)
    lane_f = lane_i.astype(F32)

    logit = lg_ref[...]
    vals, hots = [], []
    for _ in range(TOP_K):
        mk = jnp.max(logit, axis=1, keepdims=True)
        idx = jnp.min(jnp.where(logit == mk, lane_f, float(EXPERT_LANES)), axis=1, keepdims=True)
        hot = lane_f == idx
        logit = jnp.where(hot, -jnp.inf, logit)
        vals.append(mk)
        hots.append(hot)
    sel = jnp.zeros((tb, EXPERT_LANES), F32)
    for hot in hots:
        sel = sel + hot.astype(F32)
    col_sum = jnp.sum(sel, axis=0, keepdims=True)

    @pl.when((phase == 0) & (j == 0))
    def _():
        counts[...] = jnp.zeros_like(counts)

    @pl.when(phase == 0)
    def _():
        counts[...] += col_sum

    @pl.when((phase == 1) & (j == 0))
    def _():
        r = lax.broadcasted_iota(jnp.int32, (EXPERT_LANES, EXPERT_LANES), 0)
        c = lax.broadcasted_iota(jnp.int32, (EXPERT_LANES, EXPERT_LANES), 1)
        before = (r < c).astype(F32)
        start[...] = jnp.dot(counts[...], before, preferred_element_type=F32,
                             precision=HIGHEST)
        carry[...] = jnp.zeros_like(carry)
        cnt_ref[...] = counts[...].astype(jnp.int32)

    @pl.when(phase == 1)
    def _():
        r = lax.broadcasted_iota(jnp.int32, (tb, tb), 0)
        c = lax.broadcasted_iota(jnp.int32, (tb, tb), 1)
        earlier = (c < r).astype(BF16)
        prefix = jnp.dot(earlier, sel.astype(BF16), preferred_element_type=F32)
        base = prefix + carry[...] + start[...]
        exps = [jnp.exp(v - vals[0]) for v in vals]
        denom = exps[0] + exps[1] + exps[2] + exps[3]
        pos_out = jnp.zeros((tb, EXPERT_LANES), F32)
        gate_out = jnp.zeros((tb, EXPERT_LANES), F32)
        for k in range(TOP_K):
            pos_k = jnp.sum(jnp.where(hots[k], base, 0.0), axis=1, keepdims=True)
            pos_out = jnp.where(lane_i == k, pos_k, pos_out)
            gate_out = jnp.where(lane_i == k, exps[k] / denom, gate_out)
        pos_ref[...] = pos_out.astype(jnp.int32)
        gate_ref[...] = gate_out
        carry[...] += col_sum


def _route(logits):
    t = logits.shape[0]
    tb = min(ROUTE_TB, t)
    return pl.pallas_call(
        _route_kernel,
        grid=(2, t // tb),
        in_specs=[pl.BlockSpec((tb, EXPERT_LANES), lambda p, j: (j, 0))],
        out_specs=[pl.BlockSpec((tb, EXPERT_LANES), lambda p, j: (j * p, 0)),
                   pl.BlockSpec((tb, EXPERT_LANES), lambda p, j: (j * p, 0)),
                   pl.BlockSpec((1, EXPERT_LANES), lambda p, j: (0, 0))],
        out_shape=[jax.ShapeDtypeStruct((t, EXPERT_LANES), jnp.int32),
                   jax.ShapeDtypeStruct((t, EXPERT_LANES), F32),
                   jax.ShapeDtypeStruct((1, EXPERT_LANES), jnp.int32)],
        scratch_shapes=[pltpu.VMEM((1, EXPERT_LANES), F32),
                        pltpu.VMEM((1, EXPERT_LANES), F32),
                        pltpu.VMEM((1, EXPERT_LANES), F32)],
        compiler_params=_cparams(("arbitrary", "arbitrary")),
        name="route",
    )(logits)


def _row_copy(src_ref, src_row, dst_ref, dst_row, sem):
    return pltpu.make_async_copy(src_ref.at[pl.ds(src_row, 1), :],
                                 dst_ref.at[pl.ds(dst_row, 1), :], sem)


def _dispatch_kernel(pos_ref, h_ref, xs_ref, sem):
    tt = h_ref.shape[0]

    def issue(r, carry):
        for k in range(TOP_K):
            _row_copy(h_ref, r, xs_ref, pos_ref[TOP_K * r + k], sem).start()
        return carry

    def drain(r, carry):
        for k in range(TOP_K):
            _row_copy(h_ref, 0, xs_ref, 0, sem).wait()
        return carry

    lax.fori_loop(0, tt, issue, 0)
    lax.fori_loop(0, tt, drain, 0)


def _dispatch(pos_flat, h2):
    t, d = h2.shape
    tt = min(ROW_TT, t)
    return pl.pallas_call(
        _dispatch_kernel,
        grid=(t // tt,),
        in_specs=[pl.BlockSpec((tt * TOP_K,), lambda i: (i,), memory_space=pltpu.SMEM),
                  pl.BlockSpec((tt, d), lambda i: (i, 0))],
        out_specs=pl.BlockSpec(memory_space=pl.ANY),
        out_shape=jax.ShapeDtypeStruct((t * TOP_K, d), F32),
        scratch_shapes=[pltpu.SemaphoreType.DMA(())],
        compiler_params=_cparams(("arbitrary",)),
        name="dispatch",
    )(pos_flat, h2)


def _experts_kernel(blk_ref, exp_ref, lo_ref, hi_ref, first_ref, feff_ref,
                    x_ref, wg_ref, wu_ref, wd_ref, bg_ref, bu_ref, bd_ref, o_ref,
                    wg_s, wu_s, wd_s, *, sub):
    v = pl.program_id(0)
    f = pl.program_id(1)
    lo = lo_ref[v]
    hi = hi_ref[v]

    @pl.when((first_ref[v] == 1) & (f == 0))
    def _():
        o_ref[...] = jnp.zeros_like(o_ref)

    @pl.when(hi > lo)
    def _():
        wg_s[...] = wg_ref[0].astype(BF16)
        wu_s[...] = wu_ref[0].astype(BF16)
        wd_s[...] = wd_ref[0].astype(BF16)
        bias_down = jnp.where(f == 0, bd_ref[0], 0.0)

        def sub_block(j, carry):
            r0 = pl.multiple_of(j * sub, sub)
            xb = x_ref[pl.ds(r0, sub), :].astype(BF16)
            g = jnp.dot(xb, wg_s[...], preferred_element_type=F32) + bg_ref[0]
            u = jnp.dot(xb, wu_s[...], preferred_element_type=F32) + bu_ref[0]
            g = jnp.minimum(g, SWIGLU_LIMIT)
            u = jnp.clip(u, -SWIGLU_LIMIT, SWIGLU_LIMIT)
            glu = g * jax.nn.sigmoid(SWIGLU_ALPHA * g)
            a = ((u + 1.0) * glu).astype(BF16)
            y = jnp.dot(a, wd_s[...], preferred_element_type=F32) + bias_down
            rows = r0 + lax.broadcasted_iota(jnp.int32, (sub, 1), 0)
            mine = (rows >= lo) & (rows < hi)
            o_ref[pl.ds(r0, sub), :] += jnp.where(mine, y, 0.0)
            return carry

        lax.fori_loop(lo // sub, (hi + sub - 1) // sub, sub_block, 0)


def _experts(meta, xs, w_gu, b_gu3, w_down, b_down3):
    na, d = xs.shape
    n_e, _, two_f = w_gu.shape
    ff = two_f // 2
    tm = min(EXP_TM, na)
    sub = min(EXP_SUB, tm)
    tf = EXP_TF
    nf = ff // tf
    n_visits = meta[0].shape[0]
    grid_spec = pltpu.PrefetchScalarGridSpec(
        num_scalar_prefetch=6,
        grid=(n_visits, nf),
        in_specs=[
            pl.BlockSpec((tm, d), lambda v, f, blk, ex, lo, hi, fi, fe: (blk[v], 0)),
            pl.BlockSpec((1, d, tf), lambda v, f, blk, ex, lo, hi, fi, fe: (ex[v], 0, fe[v * nf + f])),
            pl.BlockSpec((1, d, tf), lambda v, f, blk, ex, lo, hi, fi, fe: (ex[v], 0, nf + fe[v * nf + f])),
            pl.BlockSpec((1, tf, d), lambda v, f, blk, ex, lo, hi, fi, fe: (ex[v], fe[v * nf + f], 0)),
            pl.BlockSpec((1, 1, tf), lambda v, f, blk, ex, lo, hi, fi, fe: (ex[v], 0, fe[v * nf + f])),
            pl.BlockSpec((1, 1, tf), lambda v, f, blk, ex, lo, hi, fi, fe: (ex[v], 0, nf + fe[v * nf + f])),
            pl.BlockSpec((1, 1, d), lambda v, f, blk, ex, lo, hi, fi, fe: (ex[v], 0, 0)),
        ],
        out_specs=pl.BlockSpec((tm, d), lambda v, f, blk, ex, lo, hi, fi, fe: (blk[v], 0)),
        scratch_shapes=[pltpu.VMEM((d, tf), BF16), pltpu.VMEM((d, tf), BF16),
                        pltpu.VMEM((tf, d), BF16)],
    )
    return pl.pallas_call(
        functools.partial(_experts_kernel, sub=sub),
        grid_spec=grid_spec,
        out_shape=jax.ShapeDtypeStruct((na, d), F32),
        compiler_params=_cparams(("arbitrary", "arbitrary"), vmem=60 * 1024 * 1024),
        name="experts",
    )(*meta, xs, w_gu, w_gu, w_down, b_gu3, b_gu3, b_down3)


def _visit_plan(counts, na, tm, nf):
    n_e = counts.shape[0]
    nb = na // tm
    n_visits = nb + n_e - 1
    ends = jnp.cumsum(counts)
    starts = ends - counts
    fb = starts // tm
    lb = jnp.maximum(ends - 1, 0) // tm
    nvis = jnp.where(counts > 0, lb - fb + 1, 0)
    vend = jnp.cumsum(nvis)
    vstart = vend - nvis
    total = vend[-1]
    v = jnp.arange(n_visits, dtype=jnp.int32)
    valid = v < total
    e_v = jnp.minimum(jnp.searchsorted(vend, jnp.minimum(v, total - 1), side="right"),
                      n_e - 1).astype(jnp.int32)
    r_v = jnp.where(valid, fb[e_v] + v - vstart[e_v], nb - 1).astype(jnp.int32)
    lo = jnp.where(valid, jnp.maximum(starts[e_v], r_v * tm) - r_v * tm, 0).astype(jnp.int32)
    hi = jnp.where(valid, jnp.minimum(ends[e_v], (r_v + 1) * tm) - r_v * tm, 0).astype(jnp.int32)
    prev_r = jnp.concatenate([jnp.full((1,), -1, jnp.int32), r_v[:-1]])
    first = (r_v != prev_r).astype(jnp.int32)
    f = jnp.arange(nf, dtype=jnp.int32)
    feff = jnp.where(valid[:, None], f[None, :], nf - 1).reshape(-1).astype(jnp.int32)
    return r_v, e_v, lo, hi, first, feff


def _combine_kernel(pos_ref, ys_ref, gate_ref, x1_ref, mod_ref, o_ref, rows, sem):
    tt = x1_ref.shape[0]

    def issue(r, carry):
        for k in range(TOP_K):
            pltpu.make_async_copy(ys_ref.at[pl.ds(pos_ref[TOP_K * r + k], 1), :],
                                  rows.at[k, pl.ds(r, 1), :], sem).start()
        return carry

    def drain(r, carry):
        for k in range(TOP_K):
            pltpu.make_async_copy(ys_ref.at[pl.ds(0, 1), :],
                                  rows.at[0, pl.ds(0, 1), :], sem).wait()
        return carry

    lax.fori_loop(0, tt, issue, 0)
    lax.fori_loop(0, tt, drain, 0)
    gate = gate_ref[...]
    y = gate[:, 0:1] * rows[0]
    for k in range(1, TOP_K):
        y = y + gate[:, k:k + 1] * rows[k]
    o_ref[...] = x1_ref[...] + mod_ref[0, 5:6, :] * y


def _combine(pos_flat, ys, gates, x1, mod3, seq):
    t, d = x1.shape
    tt = min(ROW_TT, seq)
    per_b = seq // tt
    return pl.pallas_call(
        _combine_kernel,
        grid=(t // tt,),
        in_specs=[pl.BlockSpec((tt * TOP_K,), lambda i: (i,), memory_space=pltpu.SMEM),
                  pl.BlockSpec(memory_space=pl.ANY),
                  pl.BlockSpec((tt, EXPERT_LANES), lambda i: (i, 0)),
                  pl.BlockSpec((tt, d), lambda i: (i, 0)),
                  pl.BlockSpec((1, 6, d), lambda i: (i // per_b, 0, 0))],
        out_specs=pl.BlockSpec((tt, d), lambda i: (i, 0)),
        out_shape=jax.ShapeDtypeStruct((t, d), F32),
        scratch_shapes=[pltpu.VMEM((TOP_K, tt, d), F32), pltpu.SemaphoreType.DMA(())],
        compiler_params=_cparams(("arbitrary",)),
        name="combine",
    )(pos_flat, ys, gates, x1, mod3)


def _rope_tables(seq):
    rows = seq // GRID_W
    row = jnp.repeat(jnp.arange(rows, dtype=F32), GRID_W)
    col = jnp.tile(jnp.arange(GRID_W, dtype=F32), rows)
    axis_dim = HEAD_DIM // 2
    inv = ROPE_THETA ** (-jnp.arange(0, axis_dim, 2, dtype=F32) / axis_dim)
    ang = jnp.concatenate([row[:, None] * inv, col[:, None] * inv], axis=-1)
    cos_rep = jnp.repeat(jnp.cos(ang), 2, axis=-1)
    sin_rep = jnp.repeat(jnp.sin(ang), 2, axis=-1)
    even = (jnp.arange(HEAD_DIM) % 2 == 0)[None, :]
    return cos_rep, jnp.where(even, -sin_rep, 0.0), jnp.where(even, 0.0, sin_rep)


def _layer(x, c, rel_table, w_ada, b_ada, norm_attn, norm_ffn, w_in, w_out, a_q_norm,
           a_k_norm, b_q_norm, b_k_norm, lq1, lk1, lq2, lk2, b_subln, w_router, b_router,
           w_gu, b_gu, w_down, b_down):
    batch, seq, d = x.shape
    t = batch * seq
    x2 = x.reshape(t, d)

    c_pad = jnp.pad(c, ((0, 8 - batch % 8 if batch % 8 else 0), (0, 0)))
    mod = _ada_mod(c_pad, w_ada, b_ada.reshape(1, -1))
    mod3 = mod[:batch].reshape(batch, 6, d)

    cos_rep, sin_even, sin_odd = _rope_tables(seq)
    row = lambda p: p.reshape(1, -1)
    proj = _in_proj(x2, mod3, row(norm_attn), w_in.astype(BF16), cos_rep, sin_even, sin_odd,
                    row(a_q_norm), row(a_k_norm), row(jnp.tile(b_q_norm, 2)),
                    row(jnp.tile(b_k_norm, 2)), seq)

    tq = min(ATT_TQ, seq)
    band = _bias_band(rel_table, tq)
    out_a = _attn_a(proj, batch, seq)
    out_b = _attn_b(rel_table, proj, band, row(lq1), row(lk1), row(lq2), row(lk2),
                    row(b_subln), batch, seq)

    n_e = w_router.shape[1]
    w_router_pad = jnp.pad(w_router, ((0, 0), (0, EXPERT_LANES - n_e)))
    b_router_pad = jnp.pad(b_router, (0, EXPERT_LANES - n_e), constant_values=NEG_BIG)
    x1, h2, logits = _post_attn(out_a, out_b, x2, mod3, w_out.astype(BF16), row(norm_ffn),
                                w_router_pad, row(b_router_pad), seq)

    pos, gates, counts = _route(logits)
    pos_flat = pos[:, :TOP_K].reshape(-1)
    xs = _dispatch(pos_flat, h2)

    na = t * TOP_K
    tm = min(EXP_TM, na)
    meta = _visit_plan(counts[0, :n_e], na, tm, (w_gu.shape[2] // 2) // EXP_TF)
    ys = _experts(meta, xs, w_gu, b_gu.reshape(n_e, 1, -1), w_down, b_down.reshape(n_e, 1, -1))

    out = _combine(pos_flat, ys, gates, x1, mod3, seq)
    return out.reshape(batch, seq, d)


def kernel(x, c, rel_bias_table, w_ada, b_ada, norm_attn, norm_ffn, w_in, w_out, a_q_norm,
           a_k_norm, b_q_norm, b_k_norm, lambda_q1, lambda_k1, lambda_q2, lambda_k2, b_subln,
           w_router, b_router, w_gu, b_gu, w_down, b_down):
    return _layer(x, c, rel_bias_table, w_ada[0], b_ada[0], norm_attn[0], norm_ffn[0],
                  w_in[0], w_out[0], a_q_norm[0], a_k_norm[0], b_q_norm[0], b_k_norm[0],
                  lambda_q1[0], lambda_k1[0], lambda_q2[0], lambda_k2[0], b_subln[0],
                  w_router[0], b_router[0], w_gu[0], b_gu[0], w_down[0], b_down[0])
```

```python
import functools
import math

import jax
import jax.numpy as jnp
from jax import lax
from jax.experimental import pallas as pl
from jax.experimental.pallas import tpu as pltpu

F32 = jnp.float32
BF16 = jnp.bfloat16
HIGHEST = lax.Precision.HIGHEST

D_MODEL = 2048
HEAD_DIM = 128
A_Q_HEADS = 8
A_KV_HEADS = 2
A_GROUP = A_Q_HEADS // A_KV_HEADS
ROPE_THETA = 10000.0
GRID_W = 64
B_HEADS = 8
B_QK_DIM = 64
REL_BUCKETS = 32
N_EXPERTS = 32
TOP_K = 4
D_FF = D_MODEL
SWIGLU_LIMIT = 7.0
SWIGLU_ALPHA = 1.702
EPS = 1e-6
LAMBDA_INIT = 0.8 - 0.6 * math.exp(-0.3 * 0)

A_Q_W = A_Q_HEADS * HEAD_DIM
A_KV_W = A_KV_HEADS * HEAD_DIM
B_QK_W = B_HEADS * 2 * B_QK_DIM
B_V_W = B_HEADS * HEAD_DIM
OFF_KA = A_Q_W
OFF_VA = OFF_KA + A_KV_W
OFF_QB = OFF_VA + A_KV_W
OFF_KB = OFF_QB + B_QK_W
OFF_VB = OFF_KB + B_QK_W
IN_W = OFF_VB + B_V_W

LOG2E = 1.4426950408889634
LANES = 128
EXPERT_LANES = LANES
NEG_BIG = -1e30

VMEM_LIMIT = 56 * 1024 * 1024

ADA_TN = 1536
PROJ_TM = 512
PROJ_SEG = 512
ATT_TQ = 256
ATT_A_TK = 512
POST_TM = 256
ROUTE_TB = 512
ROW_TT = 256
EXP_TM = 1024
EXP_SUB = 256
EXP_TF = 512


def _cparams(sem, vmem=VMEM_LIMIT):
    return pltpu.CompilerParams(dimension_semantics=sem, vmem_limit_bytes=vmem)


def _ada_kernel(c_ref, w_ref, b_ref, o_ref):
    c = c_ref[...]
    ca = c * jax.nn.sigmoid(c)
    o_ref[...] = jnp.dot(ca, w_ref[...], preferred_element_type=F32,
                         precision=HIGHEST) + b_ref[...]


def _ada_mod(c_pad, w_ada, b_ada):
    rows, d = c_pad.shape
    n = w_ada.shape[1]
    return pl.pallas_call(
        _ada_kernel,
        grid=(n // ADA_TN,),
        in_specs=[pl.BlockSpec((rows, d), lambda j: (0, 0)),
                  pl.BlockSpec((d, ADA_TN), lambda j: (0, j)),
                  pl.BlockSpec((1, ADA_TN), lambda j: (0, j))],
        out_specs=pl.BlockSpec((rows, ADA_TN), lambda j: (0, j)),
        out_shape=jax.ShapeDtypeStruct((rows, n), F32),
        compiler_params=_cparams(("arbitrary",)),
        name="ada_mod",
    )(c_pad, w_ada, b_ada)


def _inproj_kernel(x_ref, mod_ref, ng_ref, w_ref, cos_ref, se_ref, so_ref,
                   gaq_ref, gak_ref, gbq_ref, gbk_ref, o_ref):
    x = x_ref[...]
    ms = jnp.mean(x * x, axis=-1, keepdims=True)
    y = x * lax.rsqrt(ms + EPS) * ng_ref[...]
    h = y * (1.0 + mod_ref[0, 1:2, :]) + mod_ref[0, 0:1, :]
    hb = h.astype(BF16)

    cos = cos_ref[...]
    sin_even = se_ref[...]
    sin_odd = so_ref[...]
    lane = lax.broadcasted_iota(jnp.int32, (1, LANES), 1)
    low_half = lane < B_QK_DIM

    def rope(t):
        return (t * cos + pltpu.roll(t, LANES - 1, 1) * sin_even
                + pltpu.roll(t, 1, 1) * sin_odd)

    def norm_head(t, g):
        m = jnp.mean(t * t, axis=-1, keepdims=True)
        return t * lax.rsqrt(m + EPS) * g

    def norm_halves(t, g):
        sq = t * t
        s_lo = jnp.sum(jnp.where(low_half, sq, 0.0), axis=-1, keepdims=True)
        s_hi = jnp.sum(jnp.where(low_half, 0.0, sq), axis=-1, keepdims=True)
        m = jnp.where(low_half, s_lo, s_hi) * (1.0 / B_QK_DIM)
        return t * lax.rsqrt(m + EPS) * g

    a_scale = (HEAD_DIM ** -0.5) * LOG2E
    b_scale = (B_QK_DIM ** -0.5) * LOG2E
    for seg in range(IN_W // PROJ_SEG):
        acc = jnp.dot(hb, w_ref[:, seg * PROJ_SEG:(seg + 1) * PROJ_SEG],
                      preferred_element_type=F32)
        for j in range(PROJ_SEG // LANES):
            col = seg * PROJ_SEG + j * LANES
            t = acc[:, j * LANES:(j + 1) * LANES]
            if col < OFF_KA:
                t = rope(norm_head(t, gaq_ref[...])) * a_scale
            elif col < OFF_VA:
                t = rope(norm_head(t, gak_ref[...]))
            elif col < OFF_QB:
                pass
            elif col < OFF_KB:
                t = norm_halves(t, gbq_ref[...]) * b_scale
            elif col < OFF_VB:
                t = norm_halves(t, gbk_ref[...])
            o_ref[:, col:col + LANES] = t.astype(BF16)


def _in_proj(x2, mod3, norm_g, w_in_bf, cos_rep, sin_even, sin_odd, gaq, gak, gbq, gbk, seq):
    t, d = x2.shape
    tm = min(PROJ_TM, seq)
    per_b = seq // tm
    vec = lambda: pl.BlockSpec((1, LANES), lambda i: (0, 0))
    tab = lambda: pl.BlockSpec((tm, LANES), lambda i: (i % per_b, 0))
    return pl.pallas_call(
        _inproj_kernel,
        grid=(t // tm,),
        in_specs=[pl.BlockSpec((tm, d), lambda i: (i, 0)),
                  pl.BlockSpec((1, 6, d), lambda i: (i // per_b, 0, 0)),
                  pl.BlockSpec((1, d), lambda i: (0, 0)),
                  pl.BlockSpec((d, IN_W), lambda i: (0, 0), pipeline_mode=pl.Buffered(1)),
                  tab(), tab(), tab(), vec(), vec(), vec(), vec()],
        out_specs=pl.BlockSpec((tm, IN_W), lambda i: (i, 0)),
        out_shape=jax.ShapeDtypeStruct((t, IN_W), BF16),
        compiler_params=_cparams(("arbitrary",)),
        name="in_proj",
    )(x2, mod3, norm_g, w_in_bf, cos_rep, sin_even, sin_odd, gaq, gak, gbq, gbk)


BAND_CHUNKS = 5


def _bias_band_kernel(tab_ref, o_ref, *, tq):
    h = pl.program_id(0)
    width = BAND_CHUNKS * tq
    qq = lax.broadcasted_iota(jnp.int32, (tq, width), 0)
    kk = lax.broadcasted_iota(jnp.int32, (tq, width), 1)
    d = kk - (BAND_CHUNKS // 2) * tq - qq
    n = jnp.abs(d)
    n2 = n * n
    large = jnp.full_like(n, 8)
    for j in range(1, 8):
        large = large + (n2 >= 64 * 2 ** j).astype(jnp.int32)
    bucket = jnp.where(n < 8, n, large) + jnp.where(d > 0, 16, 0)
    acc = jnp.zeros((tq, width), F32)
    for b in range(REL_BUCKETS):
        acc = jnp.where(bucket == b, tab_ref[b, h], acc)
    o_ref[0] = acc * LOG2E


def _bias_band(rel_table, tq):
    return pl.pallas_call(
        functools.partial(_bias_band_kernel, tq=tq),
        grid=(B_HEADS,),
        in_specs=[pl.BlockSpec(memory_space=pltpu.SMEM)],
        out_specs=pl.BlockSpec((1, tq, BAND_CHUNKS * tq), lambda h: (h, 0, 0)),
        out_shape=jax.ShapeDtypeStruct((B_HEADS, tq, BAND_CHUNKS * tq), F32),
        compiler_params=_cparams(("arbitrary",)),
        name="bias_band",
    )(rel_table)


def _flash_step(s, shift, v_c, m_ref, l_ref, acc_ref, idx):
    tk = s.shape[1]
    m_prev = m_ref[idx]
    m_cur = jnp.max(s, axis=1, keepdims=True) + shift
    m_new = jnp.maximum(m_prev, m_cur)
    alpha = jnp.exp2(m_prev - m_new)
    off = m_new - shift
    p = jnp.exp2(s - jnp.tile(off, (1, tk // LANES)))
    l_ref[idx] = alpha * l_ref[idx] + jnp.sum(p, axis=1, keepdims=True)
    acc_ref[idx] = alpha * acc_ref[idx] + jnp.dot(
        p.astype(BF16), v_c, preferred_element_type=F32)
    m_ref[idx] = m_new


_NT = (((1,), (1,)), ((), ()))
SAFE_SPAN = 100.0


def _extend_values(v_ref, vext_ref):
    vext_ref[:, :HEAD_DIM] = v_ref[...]
    vext_ref[:, HEAD_DIM:] = jnp.ones((v_ref.shape[0], HEAD_DIM), BF16)


def _fixed_offset_pass(q_rows, off, k_ref, vext_ref, accx_ref, tk, bias_of):
    rows = q_rows.shape[0]
    offb = jnp.broadcast_to(off, (rows, LANES))
    accx_ref[...] = jnp.zeros(accx_ref.shape, F32)

    def chunk(c, carry):
        r0 = pl.multiple_of(c * tk, tk)
        s = lax.dot_general(q_rows, k_ref[pl.ds(r0, tk), :], _NT, preferred_element_type=F32)
        e = s - jnp.tile(offb, (1, tk // LANES))
        if bias_of is not None:
            e = e + bias_of(c)
        p = jnp.exp2(e).astype(BF16)
        accx_ref[...] += jnp.dot(p, vext_ref[pl.ds(r0, tk), :], preferred_element_type=F32)
        return carry

    lax.fori_loop(0, k_ref.shape[0] // tk, chunk, 0, unroll=True)


def _attn_a_kernel(q_ref, k_ref, v_ref, o_ref, m_ref, l_ref, acc_ref, vext_ref, stat_ref,
                   accx_ref, *, tk):
    tq = q_ref.shape[0]
    seq = k_ref.shape[0]

    @pl.when(pl.program_id(2) == 0)
    def _():
        _extend_values(v_ref, vext_ref)
        kf = k_ref[...].astype(F32)
        k_norm2 = jnp.sum(kf * kf, axis=1, keepdims=True)
        stat_ref[...] = jnp.broadcast_to(jnp.max(k_norm2, axis=0, keepdims=True), stat_ref.shape)

    q_rows = jnp.concatenate(
        [q_ref[:, g * HEAD_DIM:(g + 1) * HEAD_DIM] for g in range(A_GROUP)], axis=0)
    qf = q_rows.astype(F32)
    bound = jnp.sqrt(jnp.sum(qf * qf, axis=1, keepdims=True) * stat_ref[0:1, 0:1])
    safe = 2.0 * jnp.max(bound) <= SAFE_SPAN

    @pl.when(safe)
    def _():
        _fixed_offset_pass(q_rows, bound, k_ref, vext_ref, accx_ref, tk, None)
        for g in range(A_GROUP):
            blk = accx_ref[g * tq:(g + 1) * tq, :]
            o_ref[:, g * HEAD_DIM:(g + 1) * HEAD_DIM] = (
                blk[:, :HEAD_DIM] / blk[:, HEAD_DIM:]).astype(BF16)

    @pl.when(jnp.logical_not(safe))
    def _():
        m_ref[...] = jnp.full(m_ref.shape, -jnp.inf, F32)
        l_ref[...] = jnp.zeros(l_ref.shape, F32)
        acc_ref[...] = jnp.zeros(acc_ref.shape, F32)

        def chunk(c, carry):
            r0 = pl.multiple_of(c * tk, tk)
            k_c = k_ref[pl.ds(r0, tk), :]
            v_c = v_ref[pl.ds(r0, tk), :]
            for g in range(A_GROUP):
                q = q_ref[:, g * HEAD_DIM:(g + 1) * HEAD_DIM]
                s = lax.dot_general(q, k_c, _NT, preferred_element_type=F32)
                _flash_step(s, 0.0, v_c, m_ref, l_ref, acc_ref, g)
            return carry

        lax.fori_loop(0, seq // tk, chunk, 0)
        for g in range(A_GROUP):
            o_ref[:, g * HEAD_DIM:(g + 1) * HEAD_DIM] = (acc_ref[g] / l_ref[g]).astype(BF16)


def _attn_a(proj, batch, seq):
    t = proj.shape[0]
    tq = min(ATT_TQ, seq)
    tk = min(ATT_A_TK, seq)
    nq = seq // tq
    gw = A_GROUP * HEAD_DIM
    return pl.pallas_call(
        functools.partial(_attn_a_kernel, tk=tk),
        grid=(batch, A_KV_HEADS, nq),
        in_specs=[pl.BlockSpec((tq, gw), lambda b, g, i: (b * nq + i, g)),
                  pl.BlockSpec((seq, HEAD_DIM), lambda b, g, i: (b, OFF_KA // HEAD_DIM + g)),
                  pl.BlockSpec((seq, HEAD_DIM), lambda b, g, i: (b, OFF_VA // HEAD_DIM + g))],
        out_specs=pl.BlockSpec((tq, gw), lambda b, g, i: (b * nq + i, g)),
        out_shape=jax.ShapeDtypeStruct((t, A_Q_W), BF16),
        scratch_shapes=[pltpu.VMEM((A_GROUP, tq, LANES), F32),
                        pltpu.VMEM((A_GROUP, tq, LANES), F32),
                        pltpu.VMEM((A_GROUP, tq, HEAD_DIM), F32),
                        pltpu.VMEM((seq, 2 * HEAD_DIM), BF16),
                        pltpu.VMEM((8, LANES), F32),
                        pltpu.VMEM((A_GROUP * tq, 2 * HEAD_DIM), F32)],
        compiler_params=_cparams(("arbitrary", "arbitrary", "arbitrary")),
        name="attn_a",
    )(proj, proj, proj)


def _attn_b_kernel(tab_ref, q_ref, k_ref, v_ref, band_ref, lq1_ref, lk1_ref, lq2_ref,
                   lk2_ref, sg_ref, o_ref, m_ref, l_ref, acc_ref, vext_ref, stat_ref, accx_ref):
    tq = q_ref.shape[0]
    seq = k_ref.shape[0]
    n_chunks = seq // tq
    mid = BAND_CHUNKS // 2
    h = pl.program_id(1)
    i = pl.program_id(2)
    lane = lax.broadcasted_iota(jnp.int32, (1, LANES), 1)
    low_half = lane < B_QK_DIM

    @pl.when(i == 0)
    def _():
        _extend_values(v_ref, vext_ref)
        kf = k_ref[...].astype(F32)
        sq = kf * kf
        n1 = jnp.sum(jnp.where(low_half, sq, 0.0), axis=1, keepdims=True)
        n2 = jnp.sum(jnp.where(low_half, 0.0, sq), axis=1, keepdims=True)
        band = band_ref[0]
        row = lambda v: jnp.broadcast_to(v, (1, LANES))
        stat_ref[0:1, :] = row(jnp.max(n1, axis=0, keepdims=True))
        stat_ref[1:2, :] = row(jnp.max(n2, axis=0, keepdims=True))
        stat_ref[2:3, :] = row(jnp.max(jnp.max(band, axis=1, keepdims=True), axis=0, keepdims=True))
        stat_ref[3:4, :] = row(jnp.min(jnp.min(band, axis=1, keepdims=True), axis=0, keepdims=True))

    q = q_ref[...]
    zero = jnp.zeros_like(q)
    q1 = jnp.where(low_half, q, zero)
    q2 = jnp.where(low_half, zero, q)
    qf = q.astype(F32)
    sq = qf * qf
    qn1 = jnp.sum(jnp.where(low_half, sq, 0.0), axis=1, keepdims=True)
    qn2 = jnp.sum(jnp.where(low_half, 0.0, sq), axis=1, keepdims=True)
    bias_hi = stat_ref[2:3, 0:1]
    bias_lo = stat_ref[3:4, 0:1]
    bound = jnp.concatenate([jnp.sqrt(qn1 * stat_ref[0:1, 0:1]),
                             jnp.sqrt(qn2 * stat_ref[1:2, 0:1])], axis=0)
    safe = jnp.max(2.0 * jnp.max(bound, axis=0, keepdims=True) + (bias_hi - bias_lo)) <= SAFE_SPAN

    lam1 = jnp.exp(jnp.sum(lq1_ref[...] * lk1_ref[...], axis=-1, keepdims=True))
    lam2 = jnp.exp(jnp.sum(lq2_ref[...] * lk2_ref[...], axis=-1, keepdims=True))
    lam = lam1 - lam2 + LAMBDA_INIT

    def finish(o1, o2):
        o = o1 - lam * o2
        ms = jnp.mean(o * o, axis=-1, keepdims=True)
        o = o * lax.rsqrt(ms + EPS) * sg_ref[...] * (1.0 - LAMBDA_INIT)
        o_ref[...] = o.astype(BF16)

    @pl.when(safe)
    def _():
        def bias_of(c):
            j = jnp.clip(c - i + mid, 0, BAND_CHUNKS - 1)
            bias = band_ref[0, :, pl.ds(pl.multiple_of(j * tq, tq), tq)]
            return jnp.concatenate([bias, bias], axis=0)

        _fixed_offset_pass(jnp.concatenate([q1, q2], axis=0), bound + bias_hi, k_ref,
                           vext_ref, accx_ref, tq, bias_of)
        a1 = accx_ref[0:tq, :]
        a2 = accx_ref[tq:2 * tq, :]
        finish(a1[:, :HEAD_DIM] / a1[:, HEAD_DIM:], a2[:, :HEAD_DIM] / a2[:, HEAD_DIM:])

    @pl.when(jnp.logical_not(safe))
    def _():
        m_ref[...] = jnp.full(m_ref.shape, -jnp.inf, F32)
        l_ref[...] = jnp.zeros(l_ref.shape, F32)
        acc_ref[...] = jnp.zeros(acc_ref.shape, F32)
        far_left = tab_ref[REL_BUCKETS // 2 - 1, h] * LOG2E
        far_right = tab_ref[REL_BUCKETS - 1, h] * LOG2E

        def step(c, shift, bias):
            r0 = pl.multiple_of(c * tq, tq)
            k_c = k_ref[pl.ds(r0, tq), :]
            v_c = v_ref[pl.ds(r0, tq), :]
            s1 = lax.dot_general(q1, k_c, _NT, preferred_element_type=F32)
            s2 = lax.dot_general(q2, k_c, _NT, preferred_element_type=F32)
            if bias is not None:
                s1 = s1 + bias
                s2 = s2 + bias
            _flash_step(s1, shift, v_c, m_ref, l_ref, acc_ref, 0)
            _flash_step(s2, shift, v_c, m_ref, l_ref, acc_ref, 1)

        def left(c, carry):
            step(c, far_left, None)
            return carry

        def right(c, carry):
            step(c, far_right, None)
            return carry

        lax.fori_loop(0, jnp.maximum(i - 1, 0), left, 0)
        for jj in range(3):
            c = i - 1 + jj

            @pl.when((c >= 0) & (c < n_chunks))
            def _():
                step(c, 0.0, band_ref[0, :, (mid - 1 + jj) * tq:(mid + jj) * tq])

        lax.fori_loop(jnp.minimum(i + 2, n_chunks), n_chunks, right, 0)
        finish(acc_ref[0] / l_ref[0], acc_ref[1] / l_ref[1])


def _attn_b(rel_table, proj, band, lq1, lk1, lq2, lk2, subln, batch, seq):
    t = proj.shape[0]
    tq = band.shape[1]
    nq = seq // tq
    small = lambda w: pl.BlockSpec((1, w), lambda b, h, i, tab: (0, 0))
    grid_spec = pltpu.PrefetchScalarGridSpec(
        num_scalar_prefetch=1,
        grid=(batch, B_HEADS, nq),
        in_specs=[pl.BlockSpec((tq, HEAD_DIM), lambda b, h, i, tab: (b * nq + i, OFF_QB // HEAD_DIM + h)),
                  pl.BlockSpec((seq, HEAD_DIM), lambda b, h, i, tab: (b, OFF_KB // HEAD_DIM + h)),
                  pl.BlockSpec((seq, HEAD_DIM), lambda b, h, i, tab: (b, OFF_VB // HEAD_DIM + h)),
                  pl.BlockSpec((1, tq, BAND_CHUNKS * tq), lambda b, h, i, tab: (h, 0, 0)),
                  small(B_QK_DIM), small(B_QK_DIM), small(B_QK_DIM), small(B_QK_DIM),
                  small(HEAD_DIM)],
        out_specs=pl.BlockSpec((tq, HEAD_DIM), lambda b, h, i, tab: (b * nq + i, h)),
        scratch_shapes=[pltpu.VMEM((2, tq, LANES), F32),
                        pltpu.VMEM((2, tq, LANES), F32),
                        pltpu.VMEM((2, tq, HEAD_DIM), F32),
                        pltpu.VMEM((seq, 2 * HEAD_DIM), BF16),
                        pltpu.VMEM((8, LANES), F32),
                        pltpu.VMEM((2 * tq, 2 * HEAD_DIM), F32)],
    )
    return pl.pallas_call(
        _attn_b_kernel,
        grid_spec=grid_spec,
        out_shape=jax.ShapeDtypeStruct((t, B_V_W), BF16),
        compiler_params=_cparams(("arbitrary", "arbitrary", "arbitrary")),
        name="attn_b",
    )(rel_table, proj, proj, proj, band, lq1, lk1, lq2, lk2, subln)


def _post_attn_kernel(oa_ref, ob_ref, x_ref, mod_ref, woa_ref, wob_ref, ng_ref, wr_ref,
                      br_ref, x1_ref, h2_ref, lg_ref):
    mix = jnp.dot(oa_ref[...], woa_ref[...], preferred_element_type=F32)
    mix = mix + jnp.dot(ob_ref[...], wob_ref[...], preferred_element_type=F32)
    x1 = x_ref[...] + mod_ref[0, 2:3, :] * mix
    x1_ref[...] = x1
    ms = jnp.mean(x1 * x1, axis=-1, keepdims=True)
    y = x1 * lax.rsqrt(ms + EPS) * ng_ref[...]
    h2 = y * (1.0 + mod_ref[0, 4:5, :]) + mod_ref[0, 3:4, :]
    h2_ref[...] = h2
    lg_ref[...] = jnp.dot(h2, wr_ref[...], preferred_element_type=F32,
                          precision=HIGHEST) + br_ref[...]


def _post_attn(out_a, out_b, x2, mod3, w_out_bf, norm_g, w_router_pad, b_router_pad, seq):
    t, d = x2.shape
    tm = min(POST_TM, seq)
    per_b = seq // tm
    half = w_out_bf.shape[0] // 2
    return pl.pallas_call(
        _post_attn_kernel,
        grid=(t // tm,),
        in_specs=[pl.BlockSpec((tm, half), lambda i: (i, 0)),
                  pl.BlockSpec((tm, half), lambda i: (i, 0)),
                  pl.BlockSpec((tm, d), lambda i: (i, 0)),
                  pl.BlockSpec((1, 6, d), lambda i: (i // per_b, 0, 0)),
                  pl.BlockSpec((half, d), lambda i: (0, 0), pipeline_mode=pl.Buffered(1)),
                  pl.BlockSpec((half, d), lambda i: (1, 0), pipeline_mode=pl.Buffered(1)),
                  pl.BlockSpec((1, d), lambda i: (0, 0)),
                  pl.BlockSpec((d, EXPERT_LANES), lambda i: (0, 0)),
                  pl.BlockSpec((1, EXPERT_LANES), lambda i: (0, 0))],
        out_specs=[pl.BlockSpec((tm, d), lambda i: (i, 0)),
                   pl.BlockSpec((tm, d), lambda i: (i, 0)),
                   pl.BlockSpec((tm, EXPERT_LANES), lambda i: (i, 0))],
        out_shape=[jax.ShapeDtypeStruct((t, d), F32),
                   jax.ShapeDtypeStruct((t, d), F32),
                   jax.ShapeDtypeStruct((t, EXPERT_LANES), F32)],
        compiler_params=_cparams(("arbitrary",)),
        name="post_attn",
    )(out_a, out_b, x2, mod3, w_out_bf, w_out_bf, norm_g, w_router_pad, b_router_pad)


def _route_kernel(lg_ref, pos_ref, gate_ref, cnt_ref, counts, start, carry):
    phase = pl.program_id(0)
    j = pl.program_id(1)
    tb = lg_ref.shape[0]
    lane_i = lax.broadcasted_iota(jnp.int32, (tb, EXPERT_LANES), 1)
    lane_f = lane_i.astype(F32)

    logit = lg_ref[...]
    vals, hots = [], []
    for _ in range(TOP_K):
        mk = jnp.max(logit, axis=1, keepdims=True)
        idx = jnp.min(jnp.where(logit == mk, lane_f, float(EXPERT_LANES)), axis=1, keepdims=True)
        hot = lane_f == idx
        logit = jnp.where(hot, -jnp.inf, logit)
        vals.append(mk)
        hots.append(hot)
    sel = jnp.zeros((tb, EXPERT_LANES), F32)
    for hot in hots:
        sel = sel + hot.astype(F32)
    col_sum = jnp.sum(sel, axis=0, keepdims=True)

    @pl.when((phase == 0) & (j == 0))
    def _():
        counts[...] = jnp.zeros_like(counts)

    @pl.when(phase == 0)
    def _():
        counts[...] += col_sum

    @pl.when((phase == 1) & (j == 0))
    def _():
        r = lax.broadcasted_iota(jnp.int32, (EXPERT_LANES, EXPERT_LANES), 0)
        c = lax.broadcasted_iota(jnp.int32, (EXPERT_LANES, EXPERT_LANES), 1)
        before = (r < c).astype(F32)
        start[...] = jnp.dot(counts[...], before, preferred_element_type=F32,
                             precision=HIGHEST)
        carry[...] = jnp.zeros_like(carry)
        cnt_ref[...] = counts[...].astype(jnp.int32)

    @pl.when(phase == 1)
    def _():
        r = lax.broadcasted_iota(jnp.int32, (tb, tb), 0)
        c = lax.broadcasted_iota(jnp.int32, (tb, tb), 1)
        earlier = (c < r).astype(BF16)
        prefix = jnp.dot(earlier, sel.astype(BF16), preferred_element_type=F32)
        base = prefix + carry[...] + start[...]
        exps = [jnp.exp(v - vals[0]) for v in vals]
        denom = exps[0] + exps[1] + exps[2] + exps[3]
        pos_out = jnp.zeros((tb, EXPERT_LANES), F32)
        gate_out = jnp.zeros((tb, EXPERT_LANES), F32)
        for k in range(TOP_K):
            pos_k = jnp.sum(jnp.where(hots[k], base, 0.0), axis=1, keepdims=True)
            pos_out = jnp.where(lane_i == k, pos_k, pos_out)
            gate_out = jnp.where(lane_i == k, exps[k] / denom, gate_out)
        pos_ref[...] = pos_out.astype(jnp.int32)
        gate_ref[...] = gate_out
        carry[...] += col_sum


def _route(logits):
    t = logits.shape[0]
    tb = min(ROUTE_TB, t)
    return pl.pallas_call(
        _route_kernel,
        grid=(2, t // tb),
        in_specs=[pl.BlockSpec((tb, EXPERT_LANES), lambda p, j: (j, 0))],
        out_specs=[pl.BlockSpec((tb, EXPERT_LANES), lambda p, j: (j * p, 0)),
                   pl.BlockSpec((tb, EXPERT_LANES), lambda p, j: (j * p, 0)),
                   pl.BlockSpec((1, EXPERT_LANES), lambda p, j: (0, 0))],
        out_shape=[jax.ShapeDtypeStruct((t, EXPERT_LANES), jnp.int32),
                   jax.ShapeDtypeStruct((t, EXPERT_LANES), F32),
                   jax.ShapeDtypeStruct((1, EXPERT_LANES), jnp.int32)],
        scratch_shapes=[pltpu.VMEM((1, EXPERT_LANES), F32),
                        pltpu.VMEM((1, EXPERT_LANES), F32),
                        pltpu.VMEM((1, EXPERT_LANES), F32)],
        compiler_params=_cparams(("arbitrary", "arbitrary")),
        name="route",
    )(logits)


def _row_copy(src_ref, src_row, dst_ref, dst_row, sem):
    return pltpu.make_async_copy(src_ref.at[pl.ds(src_row, 1), :],
                                 dst_ref.at[pl.ds(dst_row, 1), :], sem)


def _pack_bf16_pairs(x):
    half = x.shape[1] // 2
    bits = lax.bitcast_convert_type(x, jnp.int32)

    def rounded(b):
        lsb = lax.shift_right_logical(b, 16) & 1
        return b + 0x7FFF + lsb

    lo = lax.shift_right_logical(rounded(bits[:, :half]), 16)
    hi = rounded(bits[:, half:]) & jnp.int32(-65536)
    return lo | hi


def _unpack_bf16_pairs(w):
    lo = lax.bitcast_convert_type(lax.shift_left(w, 16), F32)
    hi = lax.bitcast_convert_type(w & jnp.int32(-65536), F32)
    return jnp.concatenate([lo, hi], axis=1)


def _dispatch_kernel(pos_ref, h_ref, xs_ref, packed, sem):
    tt = h_ref.shape[0]
    packed[...] = _pack_bf16_pairs(h_ref[...])

    def issue(r, carry):
        for k in range(TOP_K):
            _row_copy(packed, r, xs_ref, pos_ref[TOP_K * r + k], sem).start()
        return carry

    def drain(r, carry):
        for k in range(TOP_K):
            _row_copy(packed, 0, xs_ref, 0, sem).wait()
        return carry

    lax.fori_loop(0, tt, issue, 0)
    lax.fori_loop(0, tt, drain, 0)


def _dispatch(pos_flat, h2):
    t, d = h2.shape
    tt = min(ROW_TT, t)
    return pl.pallas_call(
        _dispatch_kernel,
        grid=(t // tt,),
        in_specs=[pl.BlockSpec((tt * TOP_K,), lambda i: (i,), memory_space=pltpu.SMEM),
                  pl.BlockSpec((tt, d), lambda i: (i, 0))],
        out_specs=pl.BlockSpec(memory_space=pl.ANY),
        out_shape=jax.ShapeDtypeStruct((t * TOP_K, d // 2), jnp.int32),
        scratch_shapes=[pltpu.VMEM((tt, d // 2), jnp.int32), pltpu.SemaphoreType.DMA(())],
        compiler_params=_cparams(("arbitrary",)),
        name="dispatch",
    )(pos_flat, h2)


def _experts_kernel(blk_ref, exp_ref, lo_ref, hi_ref, first_ref, feff_ref,
                    x_ref, wg_ref, wu_ref, wd_ref, bg_ref, bu_ref, bd_ref, o_ref, *, sub):
    v = pl.program_id(0)
    f = pl.program_id(1)
    lo = lo_ref[v]
    hi = hi_ref[v]

    @pl.when((first_ref[v] == 1) & (f == 0))
    def _():
        o_ref[...] = jnp.zeros_like(o_ref)

    @pl.when(hi > lo)
    def _():
        bias_down = jnp.where(f == 0, bd_ref[0], 0.0)

        def sub_block(j, carry):
            r0 = pl.multiple_of(j * sub, sub)
            xb = _unpack_bf16_pairs(x_ref[pl.ds(r0, sub), :])
            g = jnp.dot(xb, wg_ref[0], preferred_element_type=F32) + bg_ref[0]
            u = jnp.dot(xb, wu_ref[0], preferred_element_type=F32) + bu_ref[0]
            g = jnp.minimum(g, SWIGLU_LIMIT)
            u = jnp.clip(u, -SWIGLU_LIMIT, SWIGLU_LIMIT)
            glu = g * jax.nn.sigmoid(SWIGLU_ALPHA * g)
            a = (u + 1.0) * glu
            y = jnp.dot(a, wd_ref[0], preferred_element_type=F32) + bias_down
            rows = r0 + lax.broadcasted_iota(jnp.int32, (sub, 1), 0)
            mine = (rows >= lo) & (rows < hi)
            o_ref[pl.ds(r0, sub), :] += jnp.where(mine, y, 0.0)
            return carry

        lax.fori_loop(lo // sub, (hi + sub - 1) // sub, sub_block, 0)


def _experts(meta, xs, w_gu, b_gu3, w_down, b_down3):
    na, dp = xs.shape
    n_e, d, two_f = w_gu.shape
    ff = two_f // 2
    tm = min(EXP_TM, na)
    sub = min(EXP_SUB, tm)
    tf = EXP_TF
    nf = ff // tf
    n_visits = meta[0].shape[0]
    grid_spec = pltpu.PrefetchScalarGridSpec(
        num_scalar_prefetch=6,
        grid=(n_visits, nf),
        in_specs=[
            pl.BlockSpec((tm, dp), lambda v, f, blk, ex, lo, hi, fi, fe: (blk[v], 0)),
            pl.BlockSpec((1, d, tf), lambda v, f, blk, ex, lo, hi, fi, fe: (ex[v], 0, fe[v * nf + f])),
            pl.BlockSpec((1, d, tf), lambda v, f, blk, ex, lo, hi, fi, fe: (ex[v], 0, nf + fe[v * nf + f])),
            pl.BlockSpec((1, tf, d), lambda v, f, blk, ex, lo, hi, fi, fe: (ex[v], fe[v * nf + f], 0)),
            pl.BlockSpec((1, 1, tf), lambda v, f, blk, ex, lo, hi, fi, fe: (ex[v], 0, fe[v * nf + f])),
            pl.BlockSpec((1, 1, tf), lambda v, f, blk, ex, lo, hi, fi, fe: (ex[v], 0, nf + fe[v * nf + f])),
            pl.BlockSpec((1, 1, d), lambda v, f, blk, ex, lo, hi, fi, fe: (ex[v], 0, 0)),
        ],
        out_specs=pl.BlockSpec((tm, d), lambda v, f, blk, ex, lo, hi, fi, fe: (blk[v], 0)),
    )
    return pl.pallas_call(
        functools.partial(_experts_kernel, sub=sub),
        grid_spec=grid_spec,
        out_shape=jax.ShapeDtypeStruct((na, d), F32),
        compiler_params=_cparams(("arbitrary", "arbitrary"), vmem=60 * 1024 * 1024),
        name="experts",
    )(*meta, xs, w_gu, w_gu, w_down, b_gu3, b_gu3, b_down3)


def _visit_plan(counts, na, tm, nf):
    n_e = counts.shape[0]
    nb = na // tm
    n_visits = nb + n_e - 1
    ends = jnp.cumsum(counts)
    starts = ends - counts
    fb = starts // tm
    lb = jnp.maximum(ends - 1, 0) // tm
    nvis = jnp.where(counts > 0, lb - fb + 1, 0)
    vend = jnp.cumsum(nvis)
    vstart = vend - nvis
    total = vend[-1]
    v = jnp.arange(n_visits, dtype=jnp.int32)
    valid = v < total
    e_v = jnp.minimum(jnp.searchsorted(vend, jnp.minimum(v, total - 1), side="right"),
                      n_e - 1).astype(jnp.int32)
    r_v = jnp.where(valid, fb[e_v] + v - vstart[e_v], nb - 1).astype(jnp.int32)
    lo = jnp.where(valid, jnp.maximum(starts[e_v], r_v * tm) - r_v * tm, 0).astype(jnp.int32)
    hi = jnp.where(valid, jnp.minimum(ends[e_v], (r_v + 1) * tm) - r_v * tm, 0).astype(jnp.int32)
    prev_r = jnp.concatenate([jnp.full((1,), -1, jnp.int32), r_v[:-1]])
    first = (r_v != prev_r).astype(jnp.int32)
    f = jnp.arange(nf, dtype=jnp.int32)
    feff = jnp.where(valid[:, None], f[None, :], nf - 1).reshape(-1).astype(jnp.int32)
    return r_v, e_v, lo, hi, first, feff


def _combine_kernel(pos_ref, ys_ref, gate_ref, x1_ref, mod_ref, o_ref, rows, sem):
    tt = x1_ref.shape[0]

    def issue(r, carry):
        for k in range(TOP_K):
            pltpu.make_async_copy(ys_ref.at[pl.ds(pos_ref[TOP_K * r + k], 1), :],
                                  rows.at[k, pl.ds(r, 1), :], sem).start()
        return carry

    def drain(r, carry):
        for k in range(TOP_K):
            pltpu.make_async_copy(ys_ref.at[pl.ds(0, 1), :],
                                  rows.at[0, pl.ds(0, 1), :], sem).wait()
        return carry

    lax.fori_loop(0, tt, issue, 0)
    lax.fori_loop(0, tt, drain, 0)
    gate = gate_ref[...]
    y = gate[:, 0:1] * rows[0]
    for k in range(1, TOP_K):
        y = y + gate[:, k:k + 1] * rows[k]
    o_ref[...] = x1_ref[...] + mod_ref[0, 5:6, :] * y


def _combine(pos_flat, ys, gates, x1, mod3, seq):
    t, d = x1.shape
    tt = min(ROW_TT, seq)
    per_b = seq // tt
    return pl.pallas_call(
        _combine_kernel,
        grid=(t // tt,),
        in_specs=[pl.BlockSpec((tt * TOP_K,), lambda i: (i,), memory_space=pltpu.SMEM),
                  pl.BlockSpec(memory_space=pl.ANY),
                  pl.BlockSpec((tt, EXPERT_LANES), lambda i: (i, 0)),
                  pl.BlockSpec((tt, d), lambda i: (i, 0)),
                  pl.BlockSpec((1, 6, d), lambda i: (i // per_b, 0, 0))],
        out_specs=pl.BlockSpec((tt, d), lambda i: (i, 0)),
        out_shape=jax.ShapeDtypeStruct((t, d), F32),
        scratch_shapes=[pltpu.VMEM((TOP_K, tt, d), F32), pltpu.SemaphoreType.DMA(())],
        compiler_params=_cparams(("arbitrary",)),
        name="combine",
    )(pos_flat, ys, gates, x1, mod3)


def _rope_tables(seq):
    rows = seq // GRID_W
    row = jnp.repeat(jnp.arange(rows, dtype=F32), GRID_W)
    col = jnp.tile(jnp.arange(GRID_W, dtype=F32), rows)
    axis_dim = HEAD_DIM // 2
    inv = ROPE_THETA ** (-jnp.arange(0, axis_dim, 2, dtype=F32) / axis_dim)
    ang = jnp.concatenate([row[:, None] * inv, col[:, None] * inv], axis=-1)
    cos_rep = jnp.repeat(jnp.cos(ang), 2, axis=-1)
    sin_rep = jnp.repeat(jnp.sin(ang), 2, axis=-1)
    even = (jnp.arange(HEAD_DIM) % 2 == 0)[None, :]
    return cos_rep, jnp.where(even, -sin_rep, 0.0), jnp.where(even, 0.0, sin_rep)


def _layer(x, c, rel_table, w_ada, b_ada, norm_attn, norm_ffn, w_in, w_out, a_q_norm,
           a_k_norm, b_q_norm, b_k_norm, lq1, lk1, lq2, lk2, b_subln, w_router, b_router,
           w_gu, b_gu, w_down, b_down):
    batch, seq, d = x.shape
    t = batch * seq
    x2 = x.reshape(t, d)

    c_pad = jnp.pad(c, ((0, 8 - batch % 8 if batch % 8 else 0), (0, 0)))
    mod = _ada_mod(c_pad, w_ada, b_ada.reshape(1, -1))
    mod3 = mod[:batch].reshape(batch, 6, d)

    cos_rep, sin_even, sin_odd = _rope_tables(seq)
    row = lambda p: p.reshape(1, -1)
    proj = _in_proj(x2, mod3, row(norm_attn), w_in.astype(BF16), cos_rep, sin_even, sin_odd,
                    row(a_q_norm), row(a_k_norm), row(jnp.tile(b_q_norm, 2)),
                    row(jnp.tile(b_k_norm, 2)), seq)

    tq = min(ATT_TQ, seq)
    band = _bias_band(rel_table, tq)
    out_a = _attn_a(proj, batch, seq)
    out_b = _attn_b(rel_table, proj, band, row(lq1), row(lk1), row(lq2), row(lk2),
                    row(b_subln), batch, seq)

    n_e = w_router.shape[1]
    w_router_pad = jnp.pad(w_router, ((0, 0), (0, EXPERT_LANES - n_e)))
    b_router_pad = jnp.pad(b_router, (0, EXPERT_LANES - n_e), constant_values=NEG_BIG)
    x1, h2, logits = _post_attn(out_a, out_b, x2, mod3, w_out.astype(BF16), row(norm_ffn),
                                w_router_pad, row(b_router_pad), seq)

    pos, gates, counts = _route(logits)
    pos_flat = pos[:, :TOP_K].reshape(-1)
    xs = _dispatch(pos_flat, h2)

    na = t * TOP_K
    tm = min(EXP_TM, na)
    meta = _visit_plan(counts[0, :n_e], na, tm, (w_gu.shape[2] // 2) // EXP_TF)
    ys = _experts(meta, xs, w_gu, b_gu.reshape(n_e, 1, -1), w_down, b_down.reshape(n_e, 1, -1))

    out = _combine(pos_flat, ys, gates, x1, mod3, seq)
    return out.reshape(batch, seq, d)


def kernel(x, c, rel_bias_table, w_ada, b_ada, norm_attn, norm_ffn, w_in, w_out, a_q_norm,
           a_k_norm, b_q_norm, b_k_norm, lambda_q1, lambda_k1, lambda_q2, lambda_k2, b_subln,
           w_router, b_router, w_gu, b_gu, w_down, b_down):
    return _layer(x, c, rel_bias_table, w_ada[0], b_ada[0], norm_attn[0], norm_ffn[0],
                  w_in[0], w_out[0], a_q_norm[0], a_k_norm[0], b_q_norm[0], b_k_norm[0],
                  lambda_q1[0], lambda_k1[0], lambda_q2[0], lambda_k2[0], b_subln[0],
                  w_router[0], b_router[0], w_gu[0], b_gu[0], w_down[0], b_down[0])
```

```python
import functools
import math

import jax
import jax.numpy as jnp
from jax import lax
from jax.experimental import pallas as pl
from jax.experimental.pallas import tpu as pltpu

F32 = jnp.float32
BF16 = jnp.bfloat16
HIGHEST = lax.Precision.HIGHEST

D_MODEL = 2048
HEAD_DIM = 128
A_Q_HEADS = 8
A_KV_HEADS = 2
A_GROUP = A_Q_HEADS // A_KV_HEADS
ROPE_THETA = 10000.0
GRID_W = 64
B_HEADS = 8
B_QK_DIM = 64
REL_BUCKETS = 32
N_EXPERTS = 32
TOP_K = 4
D_FF = D_MODEL
SWIGLU_LIMIT = 7.0
SWIGLU_ALPHA = 1.702
EPS = 1e-6
LAMBDA_INIT = 0.8 - 0.6 * math.exp(-0.3 * 0)

A_Q_W = A_Q_HEADS * HEAD_DIM
A_KV_W = A_KV_HEADS * HEAD_DIM
B_QK_W = B_HEADS * 2 * B_QK_DIM
B_V_W = B_HEADS * HEAD_DIM
OFF_KA = A_Q_W
OFF_VA = OFF_KA + A_KV_W
OFF_QB = OFF_VA + A_KV_W
OFF_KB = OFF_QB + B_QK_W
OFF_VB = OFF_KB + B_QK_W
IN_W = OFF_VB + B_V_W

LOG2E = 1.4426950408889634
LANES = 128
EXPERT_LANES = LANES
NEG_BIG = -1e30

VMEM_LIMIT = 56 * 1024 * 1024

ADA_TN = 1536
PROJ_TM = 512
PROJ_SEG = 512
ATT_TQ = 256
ATT_A_TK = 512
POST_TM = 512
POST_HALF = 256
ROUTE_TB = 512
ROW_TT = 256
EXP_SUB = 256
EXP_MAX_SUBS = 10
EXP_TF = 512


def _cparams(sem, vmem=VMEM_LIMIT):
    return pltpu.CompilerParams(dimension_semantics=sem, vmem_limit_bytes=vmem)


def _ada_kernel(c_ref, w_ref, b_ref, o_ref):
    c = c_ref[...]
    ca = c * jax.nn.sigmoid(c)
    o_ref[...] = jnp.dot(ca, w_ref[...], preferred_element_type=F32,
                         precision=HIGHEST) + b_ref[...]


def _ada_mod(c_pad, w_ada, b_ada):
    rows, d = c_pad.shape
    n = w_ada.shape[1]
    return pl.pallas_call(
        _ada_kernel,
        grid=(n // ADA_TN,),
        in_specs=[pl.BlockSpec((rows, d), lambda j: (0, 0)),
                  pl.BlockSpec((d, ADA_TN), lambda j: (0, j)),
                  pl.BlockSpec((1, ADA_TN), lambda j: (0, j))],
        out_specs=pl.BlockSpec((rows, ADA_TN), lambda j: (0, j)),
        out_shape=jax.ShapeDtypeStruct((rows, n), F32),
        compiler_params=_cparams(("arbitrary",)),
        name="ada_mod",
    )(c_pad, w_ada, b_ada)


def _inproj_kernel(x_ref, mod_ref, ng_ref, w_ref, cos_ref, se_ref, so_ref,
                   gaq_ref, gak_ref, gbq_ref, gbk_ref, o_ref):
    x = x_ref[...]
    ms = jnp.mean(x * x, axis=-1, keepdims=True)
    y = x * lax.rsqrt(ms + EPS) * ng_ref[...]
    h = y * (1.0 + mod_ref[0, 1:2, :]) + mod_ref[0, 0:1, :]
    hb = h.astype(BF16)

    cos = cos_ref[...]
    sin_even = se_ref[...]
    sin_odd = so_ref[...]
    lane = lax.broadcasted_iota(jnp.int32, (1, LANES), 1)
    low_half = lane < B_QK_DIM

    def rope(t):
        return (t * cos + pltpu.roll(t, LANES - 1, 1) * sin_even
                + pltpu.roll(t, 1, 1) * sin_odd)

    def norm_head(t, g):
        m = jnp.mean(t * t, axis=-1, keepdims=True)
        return t * lax.rsqrt(m + EPS) * g

    def norm_halves(t, g):
        sq = t * t
        s_lo = jnp.sum(jnp.where(low_half, sq, 0.0), axis=-1, keepdims=True)
        s_hi = jnp.sum(jnp.where(low_half, 0.0, sq), axis=-1, keepdims=True)
        m = jnp.where(low_half, s_lo, s_hi) * (1.0 / B_QK_DIM)
        return t * lax.rsqrt(m + EPS) * g

    a_scale = (HEAD_DIM ** -0.5) * LOG2E
    b_scale = (B_QK_DIM ** -0.5) * LOG2E
    for seg in range(IN_W // PROJ_SEG):
        acc = jnp.dot(hb, w_ref[:, seg * PROJ_SEG:(seg + 1) * PROJ_SEG],
                      preferred_element_type=F32)
        for j in range(PROJ_SEG // LANES):
            col = seg * PROJ_SEG + j * LANES
            t = acc[:, j * LANES:(j + 1) * LANES]
            if col < OFF_KA:
                t = rope(norm_head(t, gaq_ref[...])) * a_scale
            elif col < OFF_VA:
                t = rope(norm_head(t, gak_ref[...]))
            elif col < OFF_QB:
                pass
            elif col < OFF_KB:
                t = norm_halves(t, gbq_ref[...]) * b_scale
            elif col < OFF_VB:
                t = norm_halves(t, gbk_ref[...])
            o_ref[:, col:col + LANES] = t.astype(BF16)


def _in_proj(x2, mod3, norm_g, w_in_bf, cos_rep, sin_even, sin_odd, gaq, gak, gbq, gbk, seq):
    t, d = x2.shape
    tm = min(PROJ_TM, seq)
    per_b = seq // tm
    vec = lambda: pl.BlockSpec((1, LANES), lambda i: (0, 0))
    tab = lambda: pl.BlockSpec((tm, LANES), lambda i: (i % per_b, 0))
    return pl.pallas_call(
        _inproj_kernel,
        grid=(t // tm,),
        in_specs=[pl.BlockSpec((tm, d), lambda i: (i, 0)),
                  pl.BlockSpec((1, 6, d), lambda i: (i // per_b, 0, 0)),
                  pl.BlockSpec((1, d), lambda i: (0, 0)),
                  pl.BlockSpec((d, IN_W), lambda i: (0, 0), pipeline_mode=pl.Buffered(1)),
                  tab(), tab(), tab(), vec(), vec(), vec(), vec()],
        out_specs=pl.BlockSpec((tm, IN_W), lambda i: (i, 0)),
        out_shape=jax.ShapeDtypeStruct((t, IN_W), BF16),
        compiler_params=_cparams(("arbitrary",)),
        name="in_proj",
    )(x2, mod3, norm_g, w_in_bf, cos_rep, sin_even, sin_odd, gaq, gak, gbq, gbk)


BAND_CHUNKS = 5


def _bias_band_kernel(tab_ref, o_ref, *, tq):
    h = pl.program_id(0)
    width = BAND_CHUNKS * tq
    qq = lax.broadcasted_iota(jnp.int32, (tq, width), 0)
    kk = lax.broadcasted_iota(jnp.int32, (tq, width), 1)
    d = kk - (BAND_CHUNKS // 2) * tq - qq
    n = jnp.abs(d)
    n2 = n * n
    large = jnp.full_like(n, 8)
    for j in range(1, 8):
        large = large + (n2 >= 64 * 2 ** j).astype(jnp.int32)
    bucket = jnp.where(n < 8, n, large) + jnp.where(d > 0, 16, 0)
    acc = jnp.zeros((tq, width), F32)
    for b in range(REL_BUCKETS):
        acc = jnp.where(bucket == b, tab_ref[b, h], acc)
    o_ref[0] = acc * LOG2E


def _bias_band(rel_table, tq):
    return pl.pallas_call(
        functools.partial(_bias_band_kernel, tq=tq),
        grid=(B_HEADS,),
        in_specs=[pl.BlockSpec(memory_space=pltpu.SMEM)],
        out_specs=pl.BlockSpec((1, tq, BAND_CHUNKS * tq), lambda h: (h, 0, 0)),
        out_shape=jax.ShapeDtypeStruct((B_HEADS, tq, BAND_CHUNKS * tq), F32),
        compiler_params=_cparams(("arbitrary",)),
        name="bias_band",
    )(rel_table)


def _flash_step(s, shift, v_c, m_ref, l_ref, acc_ref, idx):
    tk = s.shape[1]
    m_prev = m_ref[idx]
    m_cur = jnp.max(s, axis=1, keepdims=True) + shift
    m_new = jnp.maximum(m_prev, m_cur)
    alpha = jnp.exp2(m_prev - m_new)
    off = m_new - shift
    p = jnp.exp2(s - jnp.tile(off, (1, tk // LANES)))
    l_ref[idx] = alpha * l_ref[idx] + jnp.sum(p, axis=1, keepdims=True)
    acc_ref[idx] = alpha * acc_ref[idx] + jnp.dot(
        p.astype(BF16), v_c, preferred_element_type=F32)
    m_ref[idx] = m_new


_NT = (((1,), (1,)), ((), ()))
SAFE_SPAN = 100.0


def _extend_values(v_ref, vext_ref):
    vext_ref[:, :HEAD_DIM] = v_ref[...]
    vext_ref[:, HEAD_DIM:] = jnp.ones((v_ref.shape[0], HEAD_DIM), BF16)


def _fixed_offset_pass(q_rows, off, k_ref, vext_ref, accx_ref, tk, bias_of):
    rows = q_rows.shape[0]
    offb = jnp.broadcast_to(off, (rows, LANES))
    accx_ref[...] = jnp.zeros(accx_ref.shape, F32)

    def chunk(c, carry):
        r0 = pl.multiple_of(c * tk, tk)
        s = lax.dot_general(q_rows, k_ref[pl.ds(r0, tk), :], _NT, preferred_element_type=F32)
        e = s - jnp.tile(offb, (1, tk // LANES))
        if bias_of is not None:
            e = e + bias_of(c)
        p = jnp.exp2(e).astype(BF16)
        accx_ref[...] += jnp.dot(p, vext_ref[pl.ds(r0, tk), :], preferred_element_type=F32)
        return carry

    lax.fori_loop(0, k_ref.shape[0] // tk, chunk, 0, unroll=True)


def _attn_a_kernel(q_ref, k_ref, v_ref, o_ref, m_ref, l_ref, acc_ref, vext_ref, stat_ref,
                   accx_ref, *, tk):
    tq = q_ref.shape[0]
    seq = k_ref.shape[0]

    @pl.when(pl.program_id(2) == 0)
    def _():
        _extend_values(v_ref, vext_ref)
        kf = k_ref[...].astype(F32)
        k_norm2 = jnp.sum(kf * kf, axis=1, keepdims=True)
        stat_ref[...] = jnp.broadcast_to(jnp.max(k_norm2, axis=0, keepdims=True), stat_ref.shape)

    q_rows = jnp.concatenate(
        [q_ref[:, g * HEAD_DIM:(g + 1) * HEAD_DIM] for g in range(A_GROUP)], axis=0)
    qf = q_rows.astype(F32)
    bound = jnp.sqrt(jnp.sum(qf * qf, axis=1, keepdims=True) * stat_ref[0:1, 0:1])
    safe = 2.0 * jnp.max(bound) <= SAFE_SPAN

    @pl.when(safe)
    def _():
        _fixed_offset_pass(q_rows, bound, k_ref, vext_ref, accx_ref, tk, None)
        for g in range(A_GROUP):
            blk = accx_ref[g * tq:(g + 1) * tq, :]
            o_ref[:, g * HEAD_DIM:(g + 1) * HEAD_DIM] = (
                blk[:, :HEAD_DIM] / blk[:, HEAD_DIM:]).astype(BF16)

    @pl.when(jnp.logical_not(safe))
    def _():
        m_ref[...] = jnp.full(m_ref.shape, -jnp.inf, F32)
        l_ref[...] = jnp.zeros(l_ref.shape, F32)
        acc_ref[...] = jnp.zeros(acc_ref.shape, F32)

        def chunk(c, carry):
            r0 = pl.multiple_of(c * tk, tk)
            k_c = k_ref[pl.ds(r0, tk), :]
            v_c = v_ref[pl.ds(r0, tk), :]
            for g in range(A_GROUP):
                q = q_ref[:, g * HEAD_DIM:(g + 1) * HEAD_DIM]
                s = lax.dot_general(q, k_c, _NT, preferred_element_type=F32)
                _flash_step(s, 0.0, v_c, m_ref, l_ref, acc_ref, g)
            return carry

        lax.fori_loop(0, seq // tk, chunk, 0)
        for g in range(A_GROUP):
            o_ref[:, g * HEAD_DIM:(g + 1) * HEAD_DIM] = (acc_ref[g] / l_ref[g]).astype(BF16)


def _attn_a(proj, batch, seq):
    t = proj.shape[0]
    tq = min(ATT_TQ, seq)
    tk = min(ATT_A_TK, seq)
    nq = seq // tq
    gw = A_GROUP * HEAD_DIM
    return pl.pallas_call(
        functools.partial(_attn_a_kernel, tk=tk),
        grid=(batch, A_KV_HEADS, nq),
        in_specs=[pl.BlockSpec((tq, gw), lambda b, g, i: (b * nq + i, g)),
                  pl.BlockSpec((seq, HEAD_DIM), lambda b, g, i: (b, OFF_KA // HEAD_DIM + g)),
                  pl.BlockSpec((seq, HEAD_DIM), lambda b, g, i: (b, OFF_VA // HEAD_DIM + g))],
        out_specs=pl.BlockSpec((tq, gw), lambda b, g, i: (b * nq + i, g)),
        out_shape=jax.ShapeDtypeStruct((t, A_Q_W), BF16),
        scratch_shapes=[pltpu.VMEM((A_GROUP, tq, LANES), F32),
                        pltpu.VMEM((A_GROUP, tq, LANES), F32),
                        pltpu.VMEM((A_GROUP, tq, HEAD_DIM), F32),
                        pltpu.VMEM((seq, 2 * HEAD_DIM), BF16),
                        pltpu.VMEM((8, LANES), F32),
                        pltpu.VMEM((A_GROUP * tq, 2 * HEAD_DIM), F32)],
        compiler_params=_cparams(("arbitrary", "arbitrary", "arbitrary")),
        name="attn_a",
    )(proj, proj, proj)


def _attn_b_kernel(tab_ref, q_ref, k_ref, v_ref, band_ref, lq1_ref, lk1_ref, lq2_ref,
                   lk2_ref, sg_ref, o_ref, m_ref, l_ref, acc_ref, vext_ref, stat_ref, accx_ref):
    tq = q_ref.shape[0]
    seq = k_ref.shape[0]
    n_chunks = seq // tq
    mid = BAND_CHUNKS // 2
    h = pl.program_id(1)
    i = pl.program_id(2)
    lane = lax.broadcasted_iota(jnp.int32, (1, LANES), 1)
    low_half = lane < B_QK_DIM

    @pl.when(i == 0)
    def _():
        _extend_values(v_ref, vext_ref)
        kf = k_ref[...].astype(F32)
        sq = kf * kf
        n1 = jnp.sum(jnp.where(low_half, sq, 0.0), axis=1, keepdims=True)
        n2 = jnp.sum(jnp.where(low_half, 0.0, sq), axis=1, keepdims=True)
        band = band_ref[0]
        row = lambda v: jnp.broadcast_to(v, (1, LANES))
        stat_ref[0:1, :] = row(jnp.max(n1, axis=0, keepdims=True))
        stat_ref[1:2, :] = row(jnp.max(n2, axis=0, keepdims=True))
        stat_ref[2:3, :] = row(jnp.max(jnp.max(band, axis=1, keepdims=True), axis=0, keepdims=True))
        stat_ref[3:4, :] = row(jnp.min(jnp.min(band, axis=1, keepdims=True), axis=0, keepdims=True))

    q = q_ref[...]
    zero = jnp.zeros_like(q)
    q1 = jnp.where(low_half, q, zero)
    q2 = jnp.where(low_half, zero, q)
    qf = q.astype(F32)
    sq = qf * qf
    qn1 = jnp.sum(jnp.where(low_half, sq, 0.0), axis=1, keepdims=True)
    qn2 = jnp.sum(jnp.where(low_half, 0.0, sq), axis=1, keepdims=True)
    bias_hi = stat_ref[2:3, 0:1]
    bias_lo = stat_ref[3:4, 0:1]
    bound = jnp.concatenate([jnp.sqrt(qn1 * stat_ref[0:1, 0:1]),
                             jnp.sqrt(qn2 * stat_ref[1:2, 0:1])], axis=0)
    safe = jnp.max(2.0 * jnp.max(bound, axis=0, keepdims=True) + (bias_hi - bias_lo)) <= SAFE_SPAN

    lam1 = jnp.exp(jnp.sum(lq1_ref[...] * lk1_ref[...], axis=-1, keepdims=True))
    lam2 = jnp.exp(jnp.sum(lq2_ref[...] * lk2_ref[...], axis=-1, keepdims=True))
    lam = lam1 - lam2 + LAMBDA_INIT

    def finish(o1, o2):
        o = o1 - lam * o2
        ms = jnp.mean(o * o, axis=-1, keepdims=True)
        o = o * lax.rsqrt(ms + EPS) * sg_ref[...] * (1.0 - LAMBDA_INIT)
        o_ref[...] = o.astype(BF16)

    @pl.when(safe)
    def _():
        def bias_of(c):
            j = jnp.clip(c - i + mid, 0, BAND_CHUNKS - 1)
            bias = band_ref[0, :, pl.ds(pl.multiple_of(j * tq, tq), tq)]
            return jnp.concatenate([bias, bias], axis=0)

        _fixed_offset_pass(jnp.concatenate([q1, q2], axis=0), bound + bias_hi, k_ref,
                           vext_ref, accx_ref, tq, bias_of)
        a1 = accx_ref[0:tq, :]
        a2 = accx_ref[tq:2 * tq, :]
        finish(a1[:, :HEAD_DIM] / a1[:, HEAD_DIM:], a2[:, :HEAD_DIM] / a2[:, HEAD_DIM:])

    @pl.when(jnp.logical_not(safe))
    def _():
        m_ref[...] = jnp.full(m_ref.shape, -jnp.inf, F32)
        l_ref[...] = jnp.zeros(l_ref.shape, F32)
        acc_ref[...] = jnp.zeros(acc_ref.shape, F32)
        far_left = tab_ref[REL_BUCKETS // 2 - 1, h] * LOG2E
        far_right = tab_ref[REL_BUCKETS - 1, h] * LOG2E

        def step(c, shift, bias):
            r0 = pl.multiple_of(c * tq, tq)
            k_c = k_ref[pl.ds(r0, tq), :]
            v_c = v_ref[pl.ds(r0, tq), :]
            s1 = lax.dot_general(q1, k_c, _NT, preferred_element_type=F32)
            s2 = lax.dot_general(q2, k_c, _NT, preferred_element_type=F32)
            if bias is not None:
                s1 = s1 + bias
                s2 = s2 + bias
            _flash_step(s1, shift, v_c, m_ref, l_ref, acc_ref, 0)
            _flash_step(s2, shift, v_c, m_ref, l_ref, acc_ref, 1)

        def left(c, carry):
            step(c, far_left, None)
            return carry

        def right(c, carry):
            step(c, far_right, None)
            return carry

        lax.fori_loop(0, jnp.maximum(i - 1, 0), left, 0)
        for jj in range(3):
            c = i - 1 + jj

            @pl.when((c >= 0) & (c < n_chunks))
            def _():
                step(c, 0.0, band_ref[0, :, (mid - 1 + jj) * tq:(mid + jj) * tq])

        lax.fori_loop(jnp.minimum(i + 2, n_chunks), n_chunks, right, 0)
        finish(acc_ref[0] / l_ref[0], acc_ref[1] / l_ref[1])


def _attn_b(rel_table, proj, band, lq1, lk1, lq2, lk2, subln, batch, seq):
    t = proj.shape[0]
    tq = band.shape[1]
    nq = seq // tq
    small = lambda w: pl.BlockSpec((1, w), lambda b, h, i, tab: (0, 0))
    grid_spec = pltpu.PrefetchScalarGridSpec(
        num_scalar_prefetch=1,
        grid=(batch, B_HEADS, nq),
        in_specs=[pl.BlockSpec((tq, HEAD_DIM), lambda b, h, i, tab: (b * nq + i, OFF_QB // HEAD_DIM + h)),
                  pl.BlockSpec((seq, HEAD_DIM), lambda b, h, i, tab: (b, OFF_KB // HEAD_DIM + h)),
                  pl.BlockSpec((seq, HEAD_DIM), lambda b, h, i, tab: (b, OFF_VB // HEAD_DIM + h)),
                  pl.BlockSpec((1, tq, BAND_CHUNKS * tq), lambda b, h, i, tab: (h, 0, 0)),
                  small(B_QK_DIM), small(B_QK_DIM), small(B_QK_DIM), small(B_QK_DIM),
                  small(HEAD_DIM)],
        out_specs=pl.BlockSpec((tq, HEAD_DIM), lambda b, h, i, tab: (b * nq + i, h)),
        scratch_shapes=[pltpu.VMEM((2, tq, LANES), F32),
                        pltpu.VMEM((2, tq, LANES), F32),
                        pltpu.VMEM((2, tq, HEAD_DIM), F32),
                        pltpu.VMEM((seq, 2 * HEAD_DIM), BF16),
                        pltpu.VMEM((8, LANES), F32),
                        pltpu.VMEM((2 * tq, 2 * HEAD_DIM), F32)],
    )
    return pl.pallas_call(
        _attn_b_kernel,
        grid_spec=grid_spec,
        out_shape=jax.ShapeDtypeStruct((t, B_V_W), BF16),
        compiler_params=_cparams(("arbitrary", "arbitrary", "arbitrary")),
        name="attn_b",
    )(rel_table, proj, proj, proj, band, lq1, lk1, lq2, lk2, subln)


def _post_attn_kernel(oa_ref, ob_ref, x_ref, mod_ref, woa_ref, wob_ref, ng_ref, wr_ref,
                      br_ref, x1_ref, hp_ref, lg_ref):
    tm = x_ref.shape[0]
    d = x_ref.shape[1]
    for r in range(0, tm, POST_HALF):
        rows = slice(r, r + POST_HALF)
        mix = jnp.dot(oa_ref[rows, :], woa_ref[...], preferred_element_type=F32)
        mix = mix + jnp.dot(ob_ref[rows, :], wob_ref[...], preferred_element_type=F32)
        x1 = x_ref[rows, :] + mod_ref[0, 2:3, :] * mix
        x1_ref[rows, :] = x1
        ms = jnp.mean(x1 * x1, axis=-1, keepdims=True)
        y = x1 * lax.rsqrt(ms + EPS) * ng_ref[...]
        h2 = y * (1.0 + mod_ref[0, 4:5, :]) + mod_ref[0, 3:4, :]
        hp_ref[rows, :] = _pack_bf16_pairs(h2)
        hi = h2.astype(BF16)
        lo = (h2 - hi.astype(F32)).astype(BF16)
        lhs = jnp.concatenate([hi, lo, hi], axis=1)
        lg_ref[rows, :] = jnp.dot(lhs, wr_ref[...], preferred_element_type=F32) + br_ref[...]


def _post_attn(out_a, out_b, x2, mod3, w_out_bf, norm_g, w_router_pad, b_router_pad, seq):
    t, d = x2.shape
    tm = min(POST_TM, seq)
    per_b = seq // tm
    half = w_out_bf.shape[0] // 2
    return pl.pallas_call(
        _post_attn_kernel,
        grid=(t // tm,),
        in_specs=[pl.BlockSpec((tm, half), lambda i: (i, 0)),
                  pl.BlockSpec((tm, half), lambda i: (i, 0)),
                  pl.BlockSpec((tm, d), lambda i: (i, 0)),
                  pl.BlockSpec((1, 6, d), lambda i: (i // per_b, 0, 0)),
                  pl.BlockSpec((half, d), lambda i: (0, 0), pipeline_mode=pl.Buffered(1)),
                  pl.BlockSpec((half, d), lambda i: (1, 0), pipeline_mode=pl.Buffered(1)),
                  pl.BlockSpec((1, d), lambda i: (0, 0)),
                  pl.BlockSpec((3 * d, EXPERT_LANES), lambda i: (0, 0)),
                  pl.BlockSpec((1, EXPERT_LANES), lambda i: (0, 0))],
        out_specs=[pl.BlockSpec((tm, d), lambda i: (i, 0)),
                   pl.BlockSpec((tm, d // 2), lambda i: (i, 0)),
                   pl.BlockSpec((tm, EXPERT_LANES), lambda i: (i, 0))],
        out_shape=[jax.ShapeDtypeStruct((t, d), F32),
                   jax.ShapeDtypeStruct((t, d // 2), jnp.int32),
                   jax.ShapeDtypeStruct((t, EXPERT_LANES), F32)],
        compiler_params=_cparams(("arbitrary",)),
        name="post_attn",
    )(out_a, out_b, x2, mod3, w_out_bf, w_out_bf, norm_g, w_router_pad, b_router_pad)


def _route_kernel(lg_ref, pos_ref, gate_ref, cnt_ref, counts, start, carry):
    phase = pl.program_id(0)
    j = pl.program_id(1)
    tb = lg_ref.shape[0]
    lane_i = lax.broadcasted_iota(jnp.int32, (tb, EXPERT_LANES), 1)
    lane_f = lane_i.astype(F32)

    logit = lg_ref[...]
    vals, hots = [], []
    for _ in range(TOP_K):
        mk = jnp.max(logit, axis=1, keepdims=True)
        idx = jnp.min(jnp.where(logit == mk, lane_f, float(EXPERT_LANES)), axis=1, keepdims=True)
        hot = lane_f == idx
        logit = jnp.where(hot, -jnp.inf, logit)
        vals.append(mk)
        hots.append(hot)
    sel = jnp.zeros((tb, EXPERT_LANES), F32)
    for hot in hots:
        sel = sel + hot.astype(F32)
    col_sum = jnp.sum(sel, axis=0, keepdims=True)

    @pl.when((phase == 0) & (j == 0))
    def _():
        counts[...] = jnp.zeros_like(counts)

    @pl.when(phase == 0)
    def _():
        counts[...] += col_sum

    @pl.when((phase == 1) & (j == 0))
    def _():
        r = lax.broadcasted_iota(jnp.int32, (EXPERT_LANES, EXPERT_LANES), 0)
        c = lax.broadcasted_iota(jnp.int32, (EXPERT_LANES, EXPERT_LANES), 1)
        before = (r < c).astype(F32)
        padded = jnp.floor((counts[...] + (EXP_SUB - 1)) * (1.0 / EXP_SUB)) * EXP_SUB
        start[...] = jnp.dot(padded, before, preferred_element_type=F32, precision=HIGHEST)
        carry[...] = jnp.zeros_like(carry)
        cnt_ref[...] = counts[...].astype(jnp.int32)

    @pl.when(phase == 1)
    def _():
        r = lax.broadcasted_iota(jnp.int32, (tb, tb), 0)
        c = lax.broadcasted_iota(jnp.int32, (tb, tb), 1)
        earlier = (c < r).astype(BF16)
        prefix = jnp.dot(earlier, sel.astype(BF16), preferred_element_type=F32)
        base = prefix + carry[...] + start[...]
        exps = [jnp.exp(v - vals[0]) for v in vals]
        denom = exps[0] + exps[1] + exps[2] + exps[3]
        pos_out = jnp.zeros((tb, EXPERT_LANES), F32)
        gate_out = jnp.zeros((tb, EXPERT_LANES), F32)
        for k in range(TOP_K):
            pos_k = jnp.sum(jnp.where(hots[k], base, 0.0), axis=1, keepdims=True)
            pos_out = jnp.where(lane_i == k, pos_k, pos_out)
            gate_out = jnp.where(lane_i == k, exps[k] / denom, gate_out)
        pos_ref[...] = pos_out.astype(jnp.int32)
        gate_ref[...] = gate_out
        carry[...] += col_sum


def _route(logits):
    t = logits.shape[0]
    tb = min(ROUTE_TB, t)
    return pl.pallas_call(
        _route_kernel,
        grid=(2, t // tb),
        in_specs=[pl.BlockSpec((tb, EXPERT_LANES), lambda p, j: (j, 0))],
        out_specs=[pl.BlockSpec((tb, EXPERT_LANES), lambda p, j: (j * p, 0)),
                   pl.BlockSpec((tb, EXPERT_LANES), lambda p, j: (j * p, 0)),
                   pl.BlockSpec((1, EXPERT_LANES), lambda p, j: (0, 0))],
        out_shape=[jax.ShapeDtypeStruct((t, EXPERT_LANES), jnp.int32),
                   jax.ShapeDtypeStruct((t, EXPERT_LANES), F32),
                   jax.ShapeDtypeStruct((1, EXPERT_LANES), jnp.int32)],
        scratch_shapes=[pltpu.VMEM((1, EXPERT_LANES), F32),
                        pltpu.VMEM((1, EXPERT_LANES), F32),
                        pltpu.VMEM((1, EXPERT_LANES), F32)],
        compiler_params=_cparams(("arbitrary", "arbitrary")),
        name="route",
    )(logits)


def _row_copy(src_ref, src_row, dst_ref, dst_row, sem):
    return pltpu.make_async_copy(src_ref.at[pl.ds(src_row, 1), :],
                                 dst_ref.at[pl.ds(dst_row, 1), :], sem)


def _pack_bf16_pairs(x):
    half = x.shape[1] // 2
    bits = lax.bitcast_convert_type(x, jnp.int32)

    def rounded(b):
        lsb = lax.shift_right_logical(b, 16) & 1
        return b + 0x7FFF + lsb

    lo = lax.shift_right_logical(rounded(bits[:, :half]), 16)
    hi = rounded(bits[:, half:]) & jnp.int32(-65536)
    return lo | hi


def _unpack_bf16_pairs(w):
    lo = lax.bitcast_convert_type(lax.shift_left(w, 16), F32)
    hi = lax.bitcast_convert_type(w & jnp.int32(-65536), F32)
    return jnp.concatenate([lo, hi], axis=1)


def _dispatch_kernel(dst_ref, h_ref, xs_ref, zero_row, sem, *, n_zero):
    tt = h_ref.shape[0]
    zero_row[...] = jnp.zeros_like(zero_row)

    def issue(r, carry):
        for k in range(TOP_K):
            _row_copy(h_ref, r, xs_ref, dst_ref[TOP_K * r + k], sem).start()
        return carry

    def clear(r, carry):
        _row_copy(zero_row, 0, xs_ref, dst_ref[TOP_K * tt + r], sem).start()
        return carry

    def drain(r, carry):
        _row_copy(zero_row, 0, xs_ref, 0, sem).wait()
        return carry

    lax.fori_loop(0, tt, issue, 0)
    lax.fori_loop(0, n_zero, clear, 0)
    lax.fori_loop(0, TOP_K * tt + n_zero, drain, 0)


def _dispatch(dst, hp, n_rows, n_zero):
    t, dp = hp.shape
    tt = min(ROW_TT, t)
    steps = t // tt
    return pl.pallas_call(
        functools.partial(_dispatch_kernel, n_zero=n_zero),
        grid=(steps,),
        in_specs=[pl.BlockSpec((dst.shape[0] // steps,), lambda i: (i,), memory_space=pltpu.SMEM),
                  pl.BlockSpec((tt, dp), lambda i: (i, 0))],
        out_specs=pl.BlockSpec(memory_space=pl.ANY),
        out_shape=jax.ShapeDtypeStruct((n_rows, dp), jnp.int32),
        scratch_shapes=[pltpu.VMEM((8, dp), jnp.int32), pltpu.SemaphoreType.DMA(())],
        compiler_params=_cparams(("arbitrary",)),
        name="dispatch",
    )(dst, hp)


def _dispatch_plan(pos_flat, counts, n_rows, steps):
    n_e = counts.shape[0]
    n_clear = n_rows - pos_flat.shape[0]
    pc = (counts + EXP_SUB - 1) // EXP_SUB * EXP_SUB
    pend = jnp.cumsum(pc)
    pstart = pend - pc
    pad = pc - counts
    cpad = jnp.cumsum(pad)
    idx = jnp.arange(n_clear, dtype=jnp.int32)
    e = jnp.minimum(jnp.searchsorted(cpad, idx, side="right"), n_e - 1).astype(jnp.int32)
    in_pad = idx < cpad[-1]
    row = jnp.where(in_pad, pstart[e] + counts[e] + idx - (cpad[e] - pad[e]),
                    pend[-1] + idx - cpad[-1]).astype(jnp.int32)
    per_tok = pos_flat.shape[0] // steps
    per_clear = n_clear // steps
    width = -(-(per_tok + per_clear) // 1024) * 1024
    dst = jnp.concatenate([pos_flat.reshape(steps, per_tok), row.reshape(steps, per_clear),
                           jnp.zeros((steps, width - per_tok - per_clear), jnp.int32)], axis=1)
    return dst.reshape(-1), per_clear


VISIT_IDLE, VISIT_COMPUTE, VISIT_CLEAR = 0, 1, 2


def _experts_kernel(exp_ref, row0_ref, nsub_ref, kind_ref, feff_ref,
                    xs_ref, wg_ref, wu_ref, wd_ref, bg_ref, bu_ref, bd_ref, ys_ref,
                    xbuf, acc, xsem, osem, *, sub):
    v = pl.program_id(0)
    f = pl.program_id(1)
    last_f = pl.num_programs(1) - 1
    row0 = row0_ref[v]
    nsub = nsub_ref[v]
    kind = kind_ref[v]

    def x_copy(j, slot):
        rows = pl.ds(pl.multiple_of(row0 + j * sub, sub), sub)
        return pltpu.make_async_copy(xs_ref.at[rows, :], xbuf.at[slot], xsem.at[slot])

    def y_copy(j, src_row):
        rows = pl.ds(pl.multiple_of(row0 + j * sub, sub), sub)
        return pltpu.make_async_copy(acc.at[pl.ds(src_row, sub), :], ys_ref.at[rows, :], osem)

    def drain(j, carry):
        y_copy(0, 0).wait()
        return carry

    def run_visit(first, last):
        x_copy(0, 0).start()

        def sub_block(j, carry):
            slot = j % 2
            x_copy(j, slot).wait()
            x_copy(jnp.minimum(j + 1, nsub - 1), 1 - slot).start()
            xb = _unpack_bf16_pairs(xbuf[slot])
            g = jnp.dot(xb, wg_ref[0], preferred_element_type=F32) + bg_ref[0]
            u = jnp.dot(xb, wu_ref[0], preferred_element_type=F32) + bu_ref[0]
            g = jnp.minimum(g, SWIGLU_LIMIT)
            u = jnp.clip(u, -SWIGLU_LIMIT, SWIGLU_LIMIT)
            glu = g * jax.nn.sigmoid(SWIGLU_ALPHA * g)
            a = (u + 1.0) * glu
            y = jnp.dot(a, wd_ref[0], preferred_element_type=F32)
            r0 = pl.multiple_of(j * sub, sub)
            if first:
                acc[pl.ds(r0, sub), :] = y + bd_ref[0]
            else:
                acc[pl.ds(r0, sub), :] += y
            if last:
                y_copy(j, r0).start()
            return carry

        lax.fori_loop(0, nsub, sub_block, 0)
        x_copy(0, nsub % 2).wait()
        if last:
            lax.fori_loop(0, nsub, drain, 0)

    active = (kind == VISIT_COMPUTE) & (nsub > 0)
    pl.when(active & (f == 0))(lambda: run_visit(True, False))
    pl.when(active & (f > 0) & (f < last_f))(lambda: run_visit(False, False))
    pl.when(active & (f == last_f))(lambda: run_visit(False, True))

    @pl.when((kind == VISIT_CLEAR) & (f == 0) & (nsub > 0))
    def _():
        acc[pl.ds(0, sub), :] = jnp.zeros((sub, acc.shape[1]), F32)

        def clear(j, carry):
            y_copy(j, 0).start()
            return carry

        lax.fori_loop(0, nsub, clear, 0)
        lax.fori_loop(0, nsub, drain, 0)


def _experts(meta, xs, w_gu, b_gu3, w_down, b_down3):
    n_rows, dp = xs.shape
    n_e, d, two_f = w_gu.shape
    ff = two_f // 2
    tf = EXP_TF
    nf = ff // tf
    n_visits = meta[0].shape[0]
    wspec = lambda shape, imap: pl.BlockSpec(shape, imap)
    grid_spec = pltpu.PrefetchScalarGridSpec(
        num_scalar_prefetch=5,
        grid=(n_visits, nf),
        in_specs=[
            pl.BlockSpec(memory_space=pl.ANY),
            wspec((1, d, tf), lambda v, f, ex, r0, ns, kd, fe: (ex[v], 0, fe[v * nf + f])),
            wspec((1, d, tf), lambda v, f, ex, r0, ns, kd, fe: (ex[v], 0, nf + fe[v * nf + f])),
            wspec((1, tf, d), lambda v, f, ex, r0, ns, kd, fe: (ex[v], fe[v * nf + f], 0)),
            wspec((1, 1, tf), lambda v, f, ex, r0, ns, kd, fe: (ex[v], 0, fe[v * nf + f])),
            wspec((1, 1, tf), lambda v, f, ex, r0, ns, kd, fe: (ex[v], 0, nf + fe[v * nf + f])),
            wspec((1, 1, d), lambda v, f, ex, r0, ns, kd, fe: (ex[v], 0, 0)),
        ],
        out_specs=pl.BlockSpec(memory_space=pl.ANY),
        scratch_shapes=[pltpu.VMEM((2, EXP_SUB, dp), jnp.int32),
                        pltpu.VMEM((EXP_MAX_SUBS * EXP_SUB, d), F32),
                        pltpu.SemaphoreType.DMA((2,)),
                        pltpu.SemaphoreType.DMA(())],
    )
    return pl.pallas_call(
        functools.partial(_experts_kernel, sub=EXP_SUB),
        grid_spec=grid_spec,
        out_shape=jax.ShapeDtypeStruct((n_rows, d), F32),
        compiler_params=_cparams(("arbitrary", "arbitrary")),
        name="experts",
    )(*meta, xs, w_gu, w_gu, w_down, b_gu3, b_gu3, b_down3)


def _visit_plan(counts, n_rows, nf):
    n_e = counts.shape[0]
    sub = EXP_SUB
    tmx = EXP_MAX_SUBS * sub
    pc = (counts + sub - 1) // sub * sub
    pend = jnp.cumsum(pc)
    pstart = pend - pc
    nvis = (pc + tmx - 1) // tmx
    vend = jnp.cumsum(nvis)
    vstart = vend - nvis
    total = vend[-1]
    n_visits = n_rows // tmx + n_e + 1
    v = jnp.arange(n_visits, dtype=jnp.int32)
    compute = v < total
    e_v = jnp.minimum(jnp.searchsorted(vend, jnp.minimum(v, total - 1), side="right"),
                      n_e - 1).astype(jnp.int32)
    part = v - vstart[e_v]
    row0 = jnp.where(compute, pstart[e_v] + part * tmx, 0)
    nsub = jnp.where(compute, jnp.minimum(pc[e_v] - part * tmx, tmx) // sub, 0)
    clear = v == total
    row0 = jnp.where(clear, jnp.minimum(pend[-1], n_rows - sub), row0).astype(jnp.int32)
    nsub = jnp.where(clear, (n_rows - pend[-1]) // sub, nsub).astype(jnp.int32)
    kind = jnp.where(compute, VISIT_COMPUTE, jnp.where(clear, VISIT_CLEAR, VISIT_IDLE))
    f = jnp.arange(nf, dtype=jnp.int32)
    feff = jnp.where(compute[:, None], f[None, :], nf - 1).reshape(-1).astype(jnp.int32)
    return e_v, row0, nsub, kind.astype(jnp.int32), feff


def _combine_kernel(pos_ref, ys_ref, gate_ref, x1_ref, mod_ref, o_ref, rows, sem):
    tt = x1_ref.shape[0]

    def issue(r, carry):
        for k in range(TOP_K):
            pltpu.make_async_copy(ys_ref.at[pl.ds(pos_ref[TOP_K * r + k], 1), :],
                                  rows.at[k, pl.ds(r, 1), :], sem).start()
        return carry

    def drain(r, carry):
        for k in range(TOP_K):
            pltpu.make_async_copy(ys_ref.at[pl.ds(0, 1), :],
                                  rows.at[0, pl.ds(0, 1), :], sem).wait()
        return carry

    lax.fori_loop(0, tt, issue, 0)
    lax.fori_loop(0, tt, drain, 0)
    gate = gate_ref[...]
    y = gate[:, 0:1] * rows[0]
    for k in range(1, TOP_K):
        y = y + gate[:, k:k + 1] * rows[k]
    o_ref[...] = x1_ref[...] + mod_ref[0, 5:6, :] * y


def _combine(pos_flat, ys, gates, x1, mod3, seq):
    t, d = x1.shape
    tt = min(ROW_TT, seq)
    per_b = seq // tt
    return pl.pallas_call(
        _combine_kernel,
        grid=(t // tt,),
        in_specs=[pl.BlockSpec((tt * TOP_K,), lambda i: (i,), memory_space=pltpu.SMEM),
                  pl.BlockSpec(memory_space=pl.ANY),
                  pl.BlockSpec((tt, EXPERT_LANES), lambda i: (i, 0)),
                  pl.BlockSpec((tt, d), lambda i: (i, 0)),
                  pl.BlockSpec((1, 6, d), lambda i: (i // per_b, 0, 0))],
        out_specs=pl.BlockSpec((tt, d), lambda i: (i, 0)),
        out_shape=jax.ShapeDtypeStruct((t, d), F32),
        scratch_shapes=[pltpu.VMEM((TOP_K, tt, d), F32), pltpu.SemaphoreType.DMA(())],
        compiler_params=_cparams(("arbitrary",)),
        name="combine",
    )(pos_flat, ys, gates, x1, mod3)


def _rope_tables(seq):
    rows = seq // GRID_W
    row = jnp.repeat(jnp.arange(rows, dtype=F32), GRID_W)
    col = jnp.tile(jnp.arange(GRID_W, dtype=F32), rows)
    axis_dim = HEAD_DIM // 2
    inv = ROPE_THETA ** (-jnp.arange(0, axis_dim, 2, dtype=F32) / axis_dim)
    ang = jnp.concatenate([row[:, None] * inv, col[:, None] * inv], axis=-1)
    cos_rep = jnp.repeat(jnp.cos(ang), 2, axis=-1)
    sin_rep = jnp.repeat(jnp.sin(ang), 2, axis=-1)
    even = (jnp.arange(HEAD_DIM) % 2 == 0)[None, :]
    return cos_rep, jnp.where(even, -sin_rep, 0.0), jnp.where(even, 0.0, sin_rep)


def _layer(x, c, rel_table, w_ada, b_ada, norm_attn, norm_ffn, w_in, w_out, a_q_norm,
           a_k_norm, b_q_norm, b_k_norm, lq1, lk1, lq2, lk2, b_subln, w_router, b_router,
           w_gu, b_gu, w_down, b_down):
    batch, seq, d = x.shape
    t = batch * seq
    x2 = x.reshape(t, d)

    c_pad = jnp.pad(c, ((0, 8 - batch % 8 if batch % 8 else 0), (0, 0)))
    mod = _ada_mod(c_pad, w_ada, b_ada.reshape(1, -1))
    mod3 = mod[:batch].reshape(batch, 6, d)

    cos_rep, sin_even, sin_odd = _rope_tables(seq)
    row = lambda p: p.reshape(1, -1)
    proj = _in_proj(x2, mod3, row(norm_attn), w_in.astype(BF16), cos_rep, sin_even, sin_odd,
                    row(a_q_norm), row(a_k_norm), row(jnp.tile(b_q_norm, 2)),
                    row(jnp.tile(b_k_norm, 2)), seq)

    tq = min(ATT_TQ, seq)
    band = _bias_band(rel_table, tq)
    out_a = _attn_a(proj, batch, seq)
    out_b = _attn_b(rel_table, proj, band, row(lq1), row(lk1), row(lq2), row(lk2),
                    row(b_subln), batch, seq)

    n_e = w_router.shape[1]
    w_router_pad = jnp.pad(w_router, ((0, 0), (0, EXPERT_LANES - n_e)))
    wr_hi = w_router_pad.astype(BF16)
    wr_lo = (w_router_pad - wr_hi.astype(F32)).astype(BF16)
    wr_split = jnp.concatenate([wr_hi, wr_hi, wr_lo], axis=0)
    b_router_pad = jnp.pad(b_router, (0, EXPERT_LANES - n_e), constant_values=NEG_BIG)
    x1, h2, logits = _post_attn(out_a, out_b, x2, mod3, w_out.astype(BF16), row(norm_ffn),
                                wr_split, row(b_router_pad), seq)

    pos, gates, counts = _route(logits)
    pos_flat = pos[:, :TOP_K].reshape(-1)
    cnt = counts[0, :n_e]
    n_rows = t * TOP_K + n_e * EXP_SUB
    dst, n_zero = _dispatch_plan(pos_flat, cnt, n_rows, t // min(ROW_TT, t))
    xs = _dispatch(dst, h2, n_rows, n_zero)

    meta = _visit_plan(cnt, n_rows, (w_gu.shape[2] // 2) // EXP_TF)
    ys = _experts(meta, xs, w_gu, b_gu.reshape(n_e, 1, -1), w_down, b_down.reshape(n_e, 1, -1))

    out = _combine(pos_flat, ys, gates, x1, mod3, seq)
    return out.reshape(batch, seq, d)


def kernel(x, c, rel_bias_table, w_ada, b_ada, norm_attn, norm_ffn, w_in, w_out, a_q_norm,
           a_k_norm, b_q_norm, b_k_norm, lambda_q1, lambda_k1, lambda_q2, lambda_k2, b_subln,
           w_router, b_router, w_gu, b_gu, w_down, b_down):
    return _layer(x, c, rel_bias_table, w_ada[0], b_ada[0], norm_attn[0], norm_ffn[0],
                  w_in[0], w_out[0], a_q_norm[0], a_k_norm[0], b_q_norm[0], b_k_norm[0],
                  lambda_q1[0], lambda_k1[0], lambda_q2[0], lambda_k2[0], b_subln[0],
                  w_router[0], b_router[0], w_gu[0], b_gu[0], w_down[0], b_down[0])
```

```python
import functools
import math

import jax
import jax.numpy as jnp
from jax import lax
from jax.experimental import pallas as pl
from jax.experimental.pallas import tpu as pltpu

F32 = jnp.float32
BF16 = jnp.bfloat16
HIGHEST = lax.Precision.HIGHEST

D_MODEL = 2048
HEAD_DIM = 128
A_Q_HEADS = 8
A_KV_HEADS = 2
A_GROUP = A_Q_HEADS // A_KV_HEADS
ROPE_THETA = 10000.0
GRID_W = 64
B_HEADS = 8
B_QK_DIM = 64
REL_BUCKETS = 32
N_EXPERTS = 32
TOP_K = 4
D_FF = D_MODEL
SWIGLU_LIMIT = 7.0
SWIGLU_ALPHA = 1.702
EPS = 1e-6
LAMBDA_INIT = 0.8 - 0.6 * math.exp(-0.3 * 0)

A_Q_W = A_Q_HEADS * HEAD_DIM
A_KV_W = A_KV_HEADS * HEAD_DIM
B_QK_W = B_HEADS * 2 * B_QK_DIM
B_V_W = B_HEADS * HEAD_DIM
OFF_KA = A_Q_W
OFF_VA = OFF_KA + A_KV_W
OFF_QB = OFF_VA + A_KV_W
OFF_KB = OFF_QB + B_QK_W
OFF_VB = OFF_KB + B_QK_W
IN_W = OFF_VB + B_V_W

LOG2E = 1.4426950408889634
LANES = 128
EXPERT_LANES = LANES
NEG_BIG = -1e30

VMEM_LIMIT = 56 * 1024 * 1024

ADA_TN = 1536
PROJ_TM = 512
PROJ_SEG = 512
ATT_TQ = 256
ATT_A_TK = 512
POST_TM = 512
POST_HALF = 256
ROUTE_TB = 512
ROW_TT = 256
EXP_SUB = 256
EXP_MAX_SUBS = 10
EXP_TF = 512


def _cparams(sem, vmem=VMEM_LIMIT):
    return pltpu.CompilerParams(dimension_semantics=sem, vmem_limit_bytes=vmem)


def _ada_kernel(c_ref, w_ref, b_ref, o_ref):
    c = c_ref[...]
    ca = c * jax.nn.sigmoid(c)
    o_ref[...] = jnp.dot(ca, w_ref[...], preferred_element_type=F32,
                         precision=HIGHEST) + b_ref[...]


def _ada_mod(c_pad, w_ada, b_ada):
    rows, d = c_pad.shape
    n = w_ada.shape[1]
    return pl.pallas_call(
        _ada_kernel,
        grid=(n // ADA_TN,),
        in_specs=[pl.BlockSpec((rows, d), lambda j: (0, 0)),
                  pl.BlockSpec((d, ADA_TN), lambda j: (0, j)),
                  pl.BlockSpec((1, ADA_TN), lambda j: (0, j))],
        out_specs=pl.BlockSpec((rows, ADA_TN), lambda j: (0, j)),
        out_shape=jax.ShapeDtypeStruct((rows, n), F32),
        compiler_params=_cparams(("arbitrary",)),
        name="ada_mod",
    )(c_pad, w_ada, b_ada)


def _inproj_kernel(x_ref, mod_ref, ng_ref, w_ref, cos_ref, se_ref, so_ref,
                   gaq_ref, gak_ref, gbq_ref, gbk_ref, o_ref):
    x = x_ref[...]
    ms = jnp.mean(x * x, axis=-1, keepdims=True)
    y = x * lax.rsqrt(ms + EPS) * ng_ref[...]
    h = y * (1.0 + mod_ref[0, 1:2, :]) + mod_ref[0, 0:1, :]
    hb = h.astype(BF16)

    cos = cos_ref[...]
    sin_even = se_ref[...]
    sin_odd = so_ref[...]
    lane = lax.broadcasted_iota(jnp.int32, (1, LANES), 1)
    low_half = lane < B_QK_DIM

    def rope(t):
        return (t * cos + pltpu.roll(t, LANES - 1, 1) * sin_even
                + pltpu.roll(t, 1, 1) * sin_odd)

    def norm_head(t, g):
        m = jnp.mean(t * t, axis=-1, keepdims=True)
        return t * lax.rsqrt(m + EPS) * g

    def norm_halves(t, g):
        sq = t * t
        s_lo = jnp.sum(jnp.where(low_half, sq, 0.0), axis=-1, keepdims=True)
        s_hi = jnp.sum(jnp.where(low_half, 0.0, sq), axis=-1, keepdims=True)
        m = jnp.where(low_half, s_lo, s_hi) * (1.0 / B_QK_DIM)
        return t * lax.rsqrt(m + EPS) * g

    a_scale = (HEAD_DIM ** -0.5) * LOG2E
    b_scale = (B_QK_DIM ** -0.5) * LOG2E
    for seg in range(IN_W // PROJ_SEG):
        acc = jnp.dot(hb, w_ref[:, seg * PROJ_SEG:(seg + 1) * PROJ_SEG],
                      preferred_element_type=F32)
        for j in range(PROJ_SEG // LANES):
            col = seg * PROJ_SEG + j * LANES
            t = acc[:, j * LANES:(j + 1) * LANES]
            if col < OFF_KA:
                t = rope(norm_head(t, gaq_ref[...])) * a_scale
            elif col < OFF_VA:
                t = rope(norm_head(t, gak_ref[...]))
            elif col < OFF_QB:
                pass
            elif col < OFF_KB:
                t = norm_halves(t, gbq_ref[...]) * b_scale
            elif col < OFF_VB:
                t = norm_halves(t, gbk_ref[...])
            o_ref[:, col:col + LANES] = t.astype(BF16)


def _in_proj(x2, mod3, norm_g, w_in_bf, cos_rep, sin_even, sin_odd, gaq, gak, gbq, gbk, seq):
    t, d = x2.shape
    tm = min(PROJ_TM, seq)
    per_b = seq // tm
    vec = lambda: pl.BlockSpec((1, LANES), lambda i: (0, 0))
    tab = lambda: pl.BlockSpec((tm, LANES), lambda i: (i % per_b, 0))
    return pl.pallas_call(
        _inproj_kernel,
        grid=(t // tm,),
        in_specs=[pl.BlockSpec((tm, d), lambda i: (i, 0)),
                  pl.BlockSpec((1, 6, d), lambda i: (i // per_b, 0, 0)),
                  pl.BlockSpec((1, d), lambda i: (0, 0)),
                  pl.BlockSpec((d, IN_W), lambda i: (0, 0), pipeline_mode=pl.Buffered(1)),
                  tab(), tab(), tab(), vec(), vec(), vec(), vec()],
        out_specs=pl.BlockSpec((tm, IN_W), lambda i: (i, 0)),
        out_shape=jax.ShapeDtypeStruct((t, IN_W), BF16),
        compiler_params=_cparams(("arbitrary",)),
        name="in_proj",
    )(x2, mod3, norm_g, w_in_bf, cos_rep, sin_even, sin_odd, gaq, gak, gbq, gbk)


BAND_CHUNKS = 5


def _bias_band_kernel(tab_ref, o_ref, *, tq):
    h = pl.program_id(0)
    width = BAND_CHUNKS * tq
    qq = lax.broadcasted_iota(jnp.int32, (tq, width), 0)
    kk = lax.broadcasted_iota(jnp.int32, (tq, width), 1)
    d = kk - (BAND_CHUNKS // 2) * tq - qq
    n = jnp.abs(d)
    n2 = n * n
    large = jnp.full_like(n, 8)
    for j in range(1, 8):
        large = large + (n2 >= 64 * 2 ** j).astype(jnp.int32)
    bucket = jnp.where(n < 8, n, large) + jnp.where(d > 0, 16, 0)
    acc = jnp.zeros((tq, width), F32)
    for b in range(REL_BUCKETS):
        acc = jnp.where(bucket == b, tab_ref[b, h], acc)
    o_ref[0] = acc * LOG2E


def _bias_band(rel_table, tq):
    return pl.pallas_call(
        functools.partial(_bias_band_kernel, tq=tq),
        grid=(B_HEADS,),
        in_specs=[pl.BlockSpec(memory_space=pltpu.SMEM)],
        out_specs=pl.BlockSpec((1, tq, BAND_CHUNKS * tq), lambda h: (h, 0, 0)),
        out_shape=jax.ShapeDtypeStruct((B_HEADS, tq, BAND_CHUNKS * tq), F32),
        compiler_params=_cparams(("arbitrary",)),
        name="bias_band",
    )(rel_table)


def _flash_step(s, shift, v_c, m_ref, l_ref, acc_ref, idx):
    tk = s.shape[1]
    m_prev = m_ref[idx]
    m_cur = jnp.max(s, axis=1, keepdims=True) + shift
    m_new = jnp.maximum(m_prev, m_cur)
    alpha = jnp.exp2(m_prev - m_new)
    off = m_new - shift
    p = jnp.exp2(s - jnp.tile(off, (1, tk // LANES)))
    l_ref[idx] = alpha * l_ref[idx] + jnp.sum(p, axis=1, keepdims=True)
    acc_ref[idx] = alpha * acc_ref[idx] + jnp.dot(
        p.astype(BF16), v_c, preferred_element_type=F32)
    m_ref[idx] = m_new


_NT = (((1,), (1,)), ((), ()))
SAFE_SPAN = 100.0


def _extend_values(v_ref, vext_ref):
    vext_ref[:, :HEAD_DIM] = v_ref[...]
    vext_ref[:, HEAD_DIM:] = jnp.ones((v_ref.shape[0], HEAD_DIM), BF16)


def _fixed_offset_pass(q_rows, off, k_ref, vext_ref, accx_ref, tk, bias_of):
    rows = q_rows.shape[0]
    offb = jnp.broadcast_to(off, (rows, LANES))
    accx_ref[...] = jnp.zeros(accx_ref.shape, F32)

    def chunk(c, carry):
        r0 = pl.multiple_of(c * tk, tk)
        s = lax.dot_general(q_rows, k_ref[pl.ds(r0, tk), :], _NT, preferred_element_type=F32)
        e = s - jnp.tile(offb, (1, tk // LANES))
        if bias_of is not None:
            e = e + bias_of(c)
        p = jnp.exp2(e).astype(BF16)
        accx_ref[...] += jnp.dot(p, vext_ref[pl.ds(r0, tk), :], preferred_element_type=F32)
        return carry

    lax.fori_loop(0, k_ref.shape[0] // tk, chunk, 0, unroll=True)


def _attn_a_kernel(q_ref, k_ref, v_ref, o_ref, m_ref, l_ref, acc_ref, vext_ref, stat_ref,
                   accx_ref, *, tk):
    tq = q_ref.shape[0]
    seq = k_ref.shape[0]

    @pl.when(pl.program_id(2) == 0)
    def _():
        _extend_values(v_ref, vext_ref)
        kf = k_ref[...].astype(F32)
        k_norm2 = jnp.sum(kf * kf, axis=1, keepdims=True)
        stat_ref[...] = jnp.broadcast_to(jnp.max(k_norm2, axis=0, keepdims=True), stat_ref.shape)

    q_rows = jnp.concatenate(
        [q_ref[:, g * HEAD_DIM:(g + 1) * HEAD_DIM] for g in range(A_GROUP)], axis=0)
    qf = q_rows.astype(F32)
    bound = jnp.sqrt(jnp.sum(qf * qf, axis=1, keepdims=True) * stat_ref[0:1, 0:1])
    safe = 2.0 * jnp.max(bound) <= SAFE_SPAN

    @pl.when(safe)
    def _():
        _fixed_offset_pass(q_rows, bound, k_ref, vext_ref, accx_ref, tk, None)
        for g in range(A_GROUP):
            blk = accx_ref[g * tq:(g + 1) * tq, :]
            o_ref[:, g * HEAD_DIM:(g + 1) * HEAD_DIM] = (
                blk[:, :HEAD_DIM] / blk[:, HEAD_DIM:]).astype(BF16)

    @pl.when(jnp.logical_not(safe))
    def _():
        m_ref[...] = jnp.full(m_ref.shape, -jnp.inf, F32)
        l_ref[...] = jnp.zeros(l_ref.shape, F32)
        acc_ref[...] = jnp.zeros(acc_ref.shape, F32)

        def chunk(c, carry):
            r0 = pl.multiple_of(c * tk, tk)
            k_c = k_ref[pl.ds(r0, tk), :]
            v_c = v_ref[pl.ds(r0, tk), :]
            for g in range(A_GROUP):
                q = q_ref[:, g * HEAD_DIM:(g + 1) * HEAD_DIM]
                s = lax.dot_general(q, k_c, _NT, preferred_element_type=F32)
                _flash_step(s, 0.0, v_c, m_ref, l_ref, acc_ref, g)
            return carry

        lax.fori_loop(0, seq // tk, chunk, 0)
        for g in range(A_GROUP):
            o_ref[:, g * HEAD_DIM:(g + 1) * HEAD_DIM] = (acc_ref[g] / l_ref[g]).astype(BF16)


def _attn_a(proj, batch, seq):
    t = proj.shape[0]
    tq = min(ATT_TQ, seq)
    tk = min(ATT_A_TK, seq)
    nq = seq // tq
    gw = A_GROUP * HEAD_DIM
    return pl.pallas_call(
        functools.partial(_attn_a_kernel, tk=tk),
        grid=(batch, A_KV_HEADS, nq),
        in_specs=[pl.BlockSpec((tq, gw), lambda b, g, i: (b * nq + i, g)),
                  pl.BlockSpec((seq, HEAD_DIM), lambda b, g, i: (b, OFF_KA // HEAD_DIM + g)),
                  pl.BlockSpec((seq, HEAD_DIM), lambda b, g, i: (b, OFF_VA // HEAD_DIM + g))],
        out_specs=pl.BlockSpec((tq, gw), lambda b, g, i: (b * nq + i, g)),
        out_shape=jax.ShapeDtypeStruct((t, A_Q_W), BF16),
        scratch_shapes=[pltpu.VMEM((A_GROUP, tq, LANES), F32),
                        pltpu.VMEM((A_GROUP, tq, LANES), F32),
                        pltpu.VMEM((A_GROUP, tq, HEAD_DIM), F32),
                        pltpu.VMEM((seq, 2 * HEAD_DIM), BF16),
                        pltpu.VMEM((8, LANES), F32),
                        pltpu.VMEM((A_GROUP * tq, 2 * HEAD_DIM), F32)],
        compiler_params=_cparams(("arbitrary", "arbitrary", "arbitrary")),
        name="attn_a",
    )(proj, proj, proj)


def _attn_b_kernel(tab_ref, q_ref, k_ref, v_ref, band_ref, lq1_ref, lk1_ref, lq2_ref,
                   lk2_ref, sg_ref, o_ref, m_ref, l_ref, acc_ref, vext_ref, stat_ref, accx_ref):
    tq = q_ref.shape[0]
    seq = k_ref.shape[0]
    n_chunks = seq // tq
    mid = BAND_CHUNKS // 2
    h = pl.program_id(1)
    i = pl.program_id(2)
    lane = lax.broadcasted_iota(jnp.int32, (1, LANES), 1)
    low_half = lane < B_QK_DIM

    @pl.when(i == 0)
    def _():
        _extend_values(v_ref, vext_ref)
        kf = k_ref[...].astype(F32)
        sq = kf * kf
        n1 = jnp.sum(jnp.where(low_half, sq, 0.0), axis=1, keepdims=True)
        n2 = jnp.sum(jnp.where(low_half, 0.0, sq), axis=1, keepdims=True)
        band = band_ref[0]
        row = lambda v: jnp.broadcast_to(v, (1, LANES))
        stat_ref[0:1, :] = row(jnp.max(n1, axis=0, keepdims=True))
        stat_ref[1:2, :] = row(jnp.max(n2, axis=0, keepdims=True))
        stat_ref[2:3, :] = row(jnp.max(jnp.max(band, axis=1, keepdims=True), axis=0, keepdims=True))
        stat_ref[3:4, :] = row(jnp.min(jnp.min(band, axis=1, keepdims=True), axis=0, keepdims=True))

    q = q_ref[...]
    zero = jnp.zeros_like(q)
    q1 = jnp.where(low_half, q, zero)
    q2 = jnp.where(low_half, zero, q)
    qf = q.astype(F32)
    sq = qf * qf
    qn1 = jnp.sum(jnp.where(low_half, sq, 0.0), axis=1, keepdims=True)
    qn2 = jnp.sum(jnp.where(low_half, 0.0, sq), axis=1, keepdims=True)
    bias_hi = stat_ref[2:3, 0:1]
    bias_lo = stat_ref[3:4, 0:1]
    bound = jnp.concatenate([jnp.sqrt(qn1 * stat_ref[0:1, 0:1]),
                             jnp.sqrt(qn2 * stat_ref[1:2, 0:1])], axis=0)
    safe = jnp.max(2.0 * jnp.max(bound, axis=0, keepdims=True) + (bias_hi - bias_lo)) <= SAFE_SPAN

    lam1 = jnp.exp(jnp.sum(lq1_ref[...] * lk1_ref[...], axis=-1, keepdims=True))
    lam2 = jnp.exp(jnp.sum(lq2_ref[...] * lk2_ref[...], axis=-1, keepdims=True))
    lam = lam1 - lam2 + LAMBDA_INIT

    def finish(o1, o2):
        o = o1 - lam * o2
        ms = jnp.mean(o * o, axis=-1, keepdims=True)
        o = o * lax.rsqrt(ms + EPS) * sg_ref[...] * (1.0 - LAMBDA_INIT)
        o_ref[...] = o.astype(BF16)

    @pl.when(safe)
    def _():
        def bias_of(c):
            j = jnp.clip(c - i + mid, 0, BAND_CHUNKS - 1)
            bias = band_ref[0, :, pl.ds(pl.multiple_of(j * tq, tq), tq)]
            return jnp.concatenate([bias, bias], axis=0)

        _fixed_offset_pass(jnp.concatenate([q1, q2], axis=0), bound + bias_hi, k_ref,
                           vext_ref, accx_ref, tq, bias_of)
        a1 = accx_ref[0:tq, :]
        a2 = accx_ref[tq:2 * tq, :]
        finish(a1[:, :HEAD_DIM] / a1[:, HEAD_DIM:], a2[:, :HEAD_DIM] / a2[:, HEAD_DIM:])

    @pl.when(jnp.logical_not(safe))
    def _():
        m_ref[...] = jnp.full(m_ref.shape, -jnp.inf, F32)
        l_ref[...] = jnp.zeros(l_ref.shape, F32)
        acc_ref[...] = jnp.zeros(acc_ref.shape, F32)
        far_left = tab_ref[REL_BUCKETS // 2 - 1, h] * LOG2E
        far_right = tab_ref[REL_BUCKETS - 1, h] * LOG2E

        def step(c, shift, bias):
            r0 = pl.multiple_of(c * tq, tq)
            k_c = k_ref[pl.ds(r0, tq), :]
            v_c = v_ref[pl.ds(r0, tq), :]
            s1 = lax.dot_general(q1, k_c, _NT, preferred_element_type=F32)
            s2 = lax.dot_general(q2, k_c, _NT, preferred_element_type=F32)
            if bias is not None:
                s1 = s1 + bias
                s2 = s2 + bias
            _flash_step(s1, shift, v_c, m_ref, l_ref, acc_ref, 0)
            _flash_step(s2, shift, v_c, m_ref, l_ref, acc_ref, 1)

        def left(c, carry):
            step(c, far_left, None)
            return carry

        def right(c, carry):
            step(c, far_right, None)
            return carry

        lax.fori_loop(0, jnp.maximum(i - 1, 0), left, 0)
        for jj in range(3):
            c = i - 1 + jj

            @pl.when((c >= 0) & (c < n_chunks))
            def _():
                step(c, 0.0, band_ref[0, :, (mid - 1 + jj) * tq:(mid + jj) * tq])

        lax.fori_loop(jnp.minimum(i + 2, n_chunks), n_chunks, right, 0)
        finish(acc_ref[0] / l_ref[0], acc_ref[1] / l_ref[1])


def _attn_b(rel_table, proj, band, lq1, lk1, lq2, lk2, subln, batch, seq):
    t = proj.shape[0]
    tq = band.shape[1]
    nq = seq // tq
    small = lambda w: pl.BlockSpec((1, w), lambda b, h, i, tab: (0, 0))
    grid_spec = pltpu.PrefetchScalarGridSpec(
        num_scalar_prefetch=1,
        grid=(batch, B_HEADS, nq),
        in_specs=[pl.BlockSpec((tq, HEAD_DIM), lambda b, h, i, tab: (b * nq + i, OFF_QB // HEAD_DIM + h)),
                  pl.BlockSpec((seq, HEAD_DIM), lambda b, h, i, tab: (b, OFF_KB // HEAD_DIM + h)),
                  pl.BlockSpec((seq, HEAD_DIM), lambda b, h, i, tab: (b, OFF_VB // HEAD_DIM + h)),
                  pl.BlockSpec((1, tq, BAND_CHUNKS * tq), lambda b, h, i, tab: (h, 0, 0)),
                  small(B_QK_DIM), small(B_QK_DIM), small(B_QK_DIM), small(B_QK_DIM),
                  small(HEAD_DIM)],
        out_specs=pl.BlockSpec((tq, HEAD_DIM), lambda b, h, i, tab: (b * nq + i, h)),
        scratch_shapes=[pltpu.VMEM((2, tq, LANES), F32),
                        pltpu.VMEM((2, tq, LANES), F32),
                        pltpu.VMEM((2, tq, HEAD_DIM), F32),
                        pltpu.VMEM((seq, 2 * HEAD_DIM), BF16),
                        pltpu.VMEM((8, LANES), F32),
                        pltpu.VMEM((2 * tq, 2 * HEAD_DIM), F32)],
    )
    return pl.pallas_call(
        _attn_b_kernel,
        grid_spec=grid_spec,
        out_shape=jax.ShapeDtypeStruct((t, B_V_W), BF16),
        compiler_params=_cparams(("arbitrary", "arbitrary", "arbitrary")),
        name="attn_b",
    )(rel_table, proj, proj, proj, band, lq1, lk1, lq2, lk2, subln)


def _post_attn_kernel(oa_ref, ob_ref, x_ref, mod_ref, woa_ref, wob_ref, ng_ref, wr_ref,
                      br_ref, x1_ref, hp_ref, lg_ref):
    tm = x_ref.shape[0]
    d = x_ref.shape[1]
    for r in range(0, tm, POST_HALF):
        rows = slice(r, r + POST_HALF)
        mix = jnp.dot(oa_ref[rows, :], woa_ref[...], preferred_element_type=F32)
        mix = mix + jnp.dot(ob_ref[rows, :], wob_ref[...], preferred_element_type=F32)
        x1 = x_ref[rows, :] + mod_ref[0, 2:3, :] * mix
        x1_ref[rows, :] = x1
        ms = jnp.mean(x1 * x1, axis=-1, keepdims=True)
        y = x1 * lax.rsqrt(ms + EPS) * ng_ref[...]
        h2 = y * (1.0 + mod_ref[0, 4:5, :]) + mod_ref[0, 3:4, :]
        hp_ref[rows, :] = _pack_bf16_pairs(h2)
        hi = h2.astype(BF16)
        lo = (h2 - hi.astype(F32)).astype(BF16)
        lhs = jnp.concatenate([hi, lo, hi], axis=1)
        lg_ref[rows, :] = jnp.dot(lhs, wr_ref[...], preferred_element_type=F32) + br_ref[...]


def _post_attn(out_a, out_b, x2, mod3, w_out_bf, norm_g, w_router_pad, b_router_pad, seq):
    t, d = x2.shape
    tm = min(POST_TM, seq)
    per_b = seq // tm
    half = w_out_bf.shape[0] // 2
    return pl.pallas_call(
        _post_attn_kernel,
        grid=(t // tm,),
        in_specs=[pl.BlockSpec((tm, half), lambda i: (i, 0)),
                  pl.BlockSpec((tm, half), lambda i: (i, 0)),
                  pl.BlockSpec((tm, d), lambda i: (i, 0)),
                  pl.BlockSpec((1, 6, d), lambda i: (i // per_b, 0, 0)),
                  pl.BlockSpec((half, d), lambda i: (0, 0), pipeline_mode=pl.Buffered(1)),
                  pl.BlockSpec((half, d), lambda i: (1, 0), pipeline_mode=pl.Buffered(1)),
                  pl.BlockSpec((1, d), lambda i: (0, 0)),
                  pl.BlockSpec((3 * d, EXPERT_LANES), lambda i: (0, 0)),
                  pl.BlockSpec((1, EXPERT_LANES), lambda i: (0, 0))],
        out_specs=[pl.BlockSpec((tm, d), lambda i: (i, 0)),
                   pl.BlockSpec((tm, d // 2), lambda i: (i, 0)),
                   pl.BlockSpec((tm, EXPERT_LANES), lambda i: (i, 0))],
        out_shape=[jax.ShapeDtypeStruct((t, d), F32),
                   jax.ShapeDtypeStruct((t, d // 2), jnp.int32),
                   jax.ShapeDtypeStruct((t, EXPERT_LANES), F32)],
        compiler_params=_cparams(("arbitrary",)),
        name="post_attn",
    )(out_a, out_b, x2, mod3, w_out_bf, w_out_bf, norm_g, w_router_pad, b_router_pad)


def _route_kernel(lg_ref, pos_ref, gate_ref, cnt_ref, counts, start, carry):
    phase = pl.program_id(0)
    j = pl.program_id(1)
    tb = lg_ref.shape[0]
    lane_i = lax.broadcasted_iota(jnp.int32, (tb, EXPERT_LANES), 1)
    lane_f = lane_i.astype(F32)

    logit = lg_ref[...]
    vals, hots = [], []
    for _ in range(TOP_K):
        mk = jnp.max(logit, axis=1, keepdims=True)
        idx = jnp.min(jnp.where(logit == mk, lane_f, float(EXPERT_LANES)), axis=1, keepdims=True)
        hot = lane_f == idx
        logit = jnp.where(hot, -jnp.inf, logit)
        vals.append(mk)
        hots.append(hot)
    sel = jnp.zeros((tb, EXPERT_LANES), F32)
    for hot in hots:
        sel = sel + hot.astype(F32)
    col_sum = jnp.sum(sel, axis=0, keepdims=True)

    @pl.when((phase == 0) & (j == 0))
    def _():
        counts[...] = jnp.zeros_like(counts)

    @pl.when(phase == 0)
    def _():
        counts[...] += col_sum

    @pl.when((phase == 1) & (j == 0))
    def _():
        r = lax.broadcasted_iota(jnp.int32, (EXPERT_LANES, EXPERT_LANES), 0)
        c = lax.broadcasted_iota(jnp.int32, (EXPERT_LANES, EXPERT_LANES), 1)
        before = (r < c).astype(F32)
        padded = jnp.floor((counts[...] + (EXP_SUB - 1)) * (1.0 / EXP_SUB)) * EXP_SUB
        start[...] = jnp.dot(padded, before, preferred_element_type=F32, precision=HIGHEST)
        carry[...] = jnp.zeros_like(carry)
        cnt_ref[...] = counts[...].astype(jnp.int32)

    @pl.when(phase == 1)
    def _():
        r = lax.broadcasted_iota(jnp.int32, (tb, tb), 0)
        c = lax.broadcasted_iota(jnp.int32, (tb, tb), 1)
        earlier = (c < r).astype(BF16)
        prefix = jnp.dot(earlier, sel.astype(BF16), preferred_element_type=F32)
        base = prefix + carry[...] + start[...]
        exps = [jnp.exp(v - vals[0]) for v in vals]
        denom = exps[0] + exps[1] + exps[2] + exps[3]
        pos_out = jnp.zeros((tb, EXPERT_LANES), F32)
        gate_out = jnp.zeros((tb, EXPERT_LANES), F32)
        for k in range(TOP_K):
            pos_k = jnp.sum(jnp.where(hots[k], base, 0.0), axis=1, keepdims=True)
            pos_out = jnp.where(lane_i == k, pos_k, pos_out)
            gate_out = jnp.where(lane_i == k, exps[k] / denom, gate_out)
        pos_ref[...] = pos_out.astype(jnp.int32)
        gate_ref[...] = gate_out
        carry[...] += col_sum


def _route(logits):
    t = logits.shape[0]
    tb = min(ROUTE_TB, t)
    return pl.pallas_call(
        _route_kernel,
        grid=(2, t // tb),
        in_specs=[pl.BlockSpec((tb, EXPERT_LANES), lambda p, j: (j, 0))],
        out_specs=[pl.BlockSpec((tb, EXPERT_LANES), lambda p, j: (j * p, 0)),
                   pl.BlockSpec((tb, EXPERT_LANES), lambda p, j: (j * p, 0)),
                   pl.BlockSpec((1, EXPERT_LANES), lambda p, j: (0, 0))],
        out_shape=[jax.ShapeDtypeStruct((t, EXPERT_LANES), jnp.int32),
                   jax.ShapeDtypeStruct((t, EXPERT_LANES), F32),
                   jax.ShapeDtypeStruct((1, EXPERT_LANES), jnp.int32)],
        scratch_shapes=[pltpu.VMEM((1, EXPERT_LANES), F32),
                        pltpu.VMEM((1, EXPERT_LANES), F32),
                        pltpu.VMEM((1, EXPERT_LANES), F32)],
        compiler_params=_cparams(("arbitrary", "arbitrary")),
        name="route",
    )(logits)


def _row_copy(src_ref, src_row, dst_ref, dst_row, sem):
    return pltpu.make_async_copy(src_ref.at[pl.ds(src_row, 1), :],
                                 dst_ref.at[pl.ds(dst_row, 1), :], sem)


def _pack_bf16_pairs(x):
    half = x.shape[1] // 2
    bits = lax.bitcast_convert_type(x, jnp.int32)

    def rounded(b):
        lsb = lax.shift_right_logical(b, 16) & 1
        return b + 0x7FFF + lsb

    lo = lax.shift_right_logical(rounded(bits[:, :half]), 16)
    hi = rounded(bits[:, half:]) & jnp.int32(-65536)
    return lo | hi


def _unpack_bf16_pairs(w):
    lo = lax.bitcast_convert_type(lax.shift_left(w, 16), F32)
    hi = lax.bitcast_convert_type(w & jnp.int32(-65536), F32)
    return jnp.concatenate([lo, hi], axis=1)


def _dispatch_kernel(clr_ref, dst_ref, h_ref, xs_ref, zeros, csem, sem):
    tt = h_ref.shape[0]
    sub = zeros.shape[0]
    n_clear = clr_ref.shape[0]

    @pl.when(pl.program_id(0) == 0)
    def _():
        zeros[...] = jnp.zeros_like(zeros)

        def block_copy(blk):
            rows = pl.ds(pl.multiple_of(blk * sub, sub), sub)
            return pltpu.make_async_copy(zeros, xs_ref.at[rows, :], csem)

        def clear(b, carry):
            @pl.when(clr_ref[b] >= 0)
            def _():
                block_copy(clr_ref[b]).start()

            return carry

        def clear_done(b, carry):
            @pl.when(clr_ref[b] >= 0)
            def _():
                block_copy(0).wait()

            return carry

        lax.fori_loop(0, n_clear, clear, 0)
        lax.fori_loop(0, n_clear, clear_done, 0)

    def issue(r, carry):
        for k in range(TOP_K):
            _row_copy(h_ref, r, xs_ref, dst_ref[TOP_K * r + k], sem).start()
        return carry

    def drain(r, carry):
        for k in range(TOP_K):
            _row_copy(h_ref, 0, xs_ref, 0, sem).wait()
        return carry

    lax.fori_loop(0, tt, issue, 0)
    lax.fori_loop(0, tt, drain, 0)


def _dispatch(clear_blocks, pos_flat, hp, n_rows):
    t, dp = hp.shape
    tt = min(ROW_TT, t)
    grid_spec = pltpu.PrefetchScalarGridSpec(
        num_scalar_prefetch=1,
        grid=(t // tt,),
        in_specs=[pl.BlockSpec((tt * TOP_K,), lambda i, clr: (i,), memory_space=pltpu.SMEM),
                  pl.BlockSpec((tt, dp), lambda i, clr: (i, 0))],
        out_specs=pl.BlockSpec(memory_space=pl.ANY),
        scratch_shapes=[pltpu.VMEM((EXP_SUB, dp), jnp.int32), pltpu.SemaphoreType.DMA(()),
                        pltpu.SemaphoreType.DMA(())],
    )
    return pl.pallas_call(
        _dispatch_kernel,
        grid_spec=grid_spec,
        out_shape=jax.ShapeDtypeStruct((n_rows, dp), jnp.int32),
        compiler_params=_cparams(("arbitrary",)),
        name="dispatch",
    )(clear_blocks, pos_flat, hp)


def _clear_plan(counts, n_rows):
    n_e = counts.shape[0]
    n_blocks = n_rows // EXP_SUB
    pcb = (counts + EXP_SUB - 1) // EXP_SUB
    pend = jnp.cumsum(pcb)
    seg_last = jnp.where(pcb > 0, pend - 1, -1)
    tail = pend[-1] + jnp.arange(n_e, dtype=jnp.int32)
    tail = jnp.where(tail < n_blocks, tail, -1)
    return jnp.concatenate([seg_last, tail]).astype(jnp.int32)


VISIT_IDLE, VISIT_COMPUTE, VISIT_CLEAR = 0, 1, 2


def _experts_kernel(exp_ref, row0_ref, nsub_ref, kind_ref, feff_ref,
                    xs_ref, wg_ref, wu_ref, wd_ref, bg_ref, bu_ref, bd_ref, ys_ref,
                    xbuf, acc, xsem, osem, *, sub):
    v = pl.program_id(0)
    f = pl.program_id(1)
    last_f = pl.num_programs(1) - 1
    row0 = row0_ref[v]
    nsub = nsub_ref[v]
    kind = kind_ref[v]

    def x_copy(j, slot):
        rows = pl.ds(pl.multiple_of(row0 + j * sub, sub), sub)
        return pltpu.make_async_copy(xs_ref.at[rows, :], xbuf.at[slot], xsem.at[slot])

    def y_copy(j, src_row):
        rows = pl.ds(pl.multiple_of(row0 + j * sub, sub), sub)
        return pltpu.make_async_copy(acc.at[pl.ds(src_row, sub), :], ys_ref.at[rows, :], osem)

    def drain(j, carry):
        y_copy(0, 0).wait()
        return carry

    def x0_copy(visit):
        rows = pl.ds(pl.multiple_of(row0_ref[visit], sub), sub)
        return pltpu.make_async_copy(xs_ref.at[rows, :], xbuf.at[0], xsem.at[0])

    def run_visit(first, last):
        def compute(j, slot):
            xb = _unpack_bf16_pairs(xbuf[slot])
            g = jnp.dot(xb, wg_ref[0], preferred_element_type=F32) + bg_ref[0]
            u = jnp.dot(xb, wu_ref[0], preferred_element_type=F32) + bu_ref[0]
            g = jnp.minimum(g, SWIGLU_LIMIT)
            u = jnp.clip(u, -SWIGLU_LIMIT, SWIGLU_LIMIT)
            glu = g * jax.nn.sigmoid(SWIGLU_ALPHA * g)
            a = (u + 1.0) * glu
            y = jnp.dot(a, wd_ref[0], preferred_element_type=F32)
            r0 = pl.multiple_of(j * sub, sub)
            if first:
                acc[pl.ds(r0, sub), :] = y + bd_ref[0]
            else:
                acc[pl.ds(r0, sub), :] += y
            if last:
                y_copy(j, r0).start()

        if first:
            @pl.when(v == 0)
            def _():
                x0_copy(0).start()

            x0_copy(v).wait()
        more = nsub > 1

        @pl.when(more)
        def _():
            x_copy(1, 1).start()

        compute(0, 0)
        if last:
            nxt = jnp.minimum(v + 1, pl.num_programs(0) - 1)

            @pl.when(kind_ref[nxt] == VISIT_COMPUTE)
            def _():
                x0_copy(nxt).start()

        @pl.when(more)
        def _():
            def sub_block(j, carry):
                slot = 1 + (j - 1) % 2
                x_copy(j, slot).wait()
                x_copy(jnp.minimum(j + 1, nsub - 1), 3 - slot).start()
                compute(j, slot)
                return carry

            lax.fori_loop(1, nsub, sub_block, 0)
            x_copy(0, 1 + (nsub - 1) % 2).wait()

        if last:
            lax.fori_loop(0, nsub, drain, 0)

    active = (kind == VISIT_COMPUTE) & (nsub > 0)
    pl.when(active & (f == 0))(lambda: run_visit(True, False))
    pl.when(active & (f > 0) & (f < last_f))(lambda: run_visit(False, False))
    pl.when(active & (f == last_f))(lambda: run_visit(False, True))

    @pl.when((kind == VISIT_CLEAR) & (f == 0) & (nsub > 0))
    def _():
        acc[pl.ds(0, sub), :] = jnp.zeros((sub, acc.shape[1]), F32)

        def clear(j, carry):
            y_copy(j, 0).start()
            return carry

        lax.fori_loop(0, nsub, clear, 0)
        lax.fori_loop(0, nsub, drain, 0)


def _experts(meta, xs, w_gu, b_gu3, w_down, b_down3):
    n_rows, dp = xs.shape
    n_e, d, two_f = w_gu.shape
    ff = two_f // 2
    tf = EXP_TF
    nf = ff // tf
    n_visits = meta[0].shape[0]
    wspec = lambda shape, imap: pl.BlockSpec(shape, imap)
    grid_spec = pltpu.PrefetchScalarGridSpec(
        num_scalar_prefetch=5,
        grid=(n_visits, nf),
        in_specs=[
            pl.BlockSpec(memory_space=pl.ANY),
            wspec((1, d, tf), lambda v, f, ex, r0, ns, kd, fe: (ex[v], 0, fe[v * nf + f])),
            wspec((1, d, tf), lambda v, f, ex, r0, ns, kd, fe: (ex[v], 0, nf + fe[v * nf + f])),
            wspec((1, tf, d), lambda v, f, ex, r0, ns, kd, fe: (ex[v], fe[v * nf + f], 0)),
            wspec((1, 1, tf), lambda v, f, ex, r0, ns, kd, fe: (ex[v], 0, fe[v * nf + f])),
            wspec((1, 1, tf), lambda v, f, ex, r0, ns, kd, fe: (ex[v], 0, nf + fe[v * nf + f])),
            wspec((1, 1, d), lambda v, f, ex, r0, ns, kd, fe: (ex[v], 0, 0)),
        ],
        out_specs=pl.BlockSpec(memory_space=pl.ANY),
        scratch_shapes=[pltpu.VMEM((3, EXP_SUB, dp), jnp.int32),
                        pltpu.VMEM((EXP_MAX_SUBS * EXP_SUB, d), F32),
                        pltpu.SemaphoreType.DMA((3,)),
                        pltpu.SemaphoreType.DMA(())],
    )
    return pl.pallas_call(
        functools.partial(_experts_kernel, sub=EXP_SUB),
        grid_spec=grid_spec,
        out_shape=jax.ShapeDtypeStruct((n_rows, d), F32),
        compiler_params=_cparams(("arbitrary", "arbitrary")),
        name="experts",
    )(*meta, xs, w_gu, w_gu, w_down, b_gu3, b_gu3, b_down3)


def _visit_plan(counts, n_rows, nf):
    n_e = counts.shape[0]
    sub = EXP_SUB
    tmx = EXP_MAX_SUBS * sub
    pc = (counts + sub - 1) // sub * sub
    pend = jnp.cumsum(pc)
    pstart = pend - pc
    nvis = (pc + tmx - 1) // tmx
    vend = jnp.cumsum(nvis)
    vstart = vend - nvis
    total = vend[-1]
    n_visits = n_rows // tmx + n_e + 1
    v = jnp.arange(n_visits, dtype=jnp.int32)
    compute = v < total
    v_clamped = jnp.minimum(v, total - 1)
    e_v = jnp.minimum(jnp.sum(vend[None, :] <= v_clamped[:, None], axis=1), n_e - 1).astype(jnp.int32)
    part = v - vstart[e_v]
    row0 = jnp.where(compute, pstart[e_v] + part * tmx, 0)
    nsub = jnp.where(compute, jnp.minimum(pc[e_v] - part * tmx, tmx) // sub, 0)
    clear = v == total
    row0 = jnp.where(clear, jnp.minimum(pend[-1], n_rows - sub), row0).astype(jnp.int32)
    nsub = jnp.where(clear, (n_rows - pend[-1]) // sub, nsub).astype(jnp.int32)
    kind = jnp.where(compute, VISIT_COMPUTE, jnp.where(clear, VISIT_CLEAR, VISIT_IDLE))
    f = jnp.arange(nf, dtype=jnp.int32)
    feff = jnp.where(compute[:, None], f[None, :], nf - 1).reshape(-1).astype(jnp.int32)
    return e_v, row0, nsub, kind.astype(jnp.int32), feff


def _combine_kernel(pos_ref, ys_ref, gate_ref, x1_ref, mod_ref, o_ref, rows, sem):
    tt = x1_ref.shape[0]

    def issue(r, carry):
        for k in range(TOP_K):
            pltpu.make_async_copy(ys_ref.at[pl.ds(pos_ref[TOP_K * r + k], 1), :],
                                  rows.at[k, pl.ds(r, 1), :], sem).start()
        return carry

    def drain(r, carry):
        for k in range(TOP_K):
            pltpu.make_async_copy(ys_ref.at[pl.ds(0, 1), :],
                                  rows.at[0, pl.ds(0, 1), :], sem).wait()
        return carry

    lax.fori_loop(0, tt, issue, 0)
    lax.fori_loop(0, tt, drain, 0)
    gate = gate_ref[...]
    y = gate[:, 0:1] * rows[0]
    for k in range(1, TOP_K):
        y = y + gate[:, k:k + 1] * rows[k]
    o_ref[...] = x1_ref[...] + mod_ref[0, 5:6, :] * y


def _combine(pos_flat, ys, gates, x1, mod3, seq):
    t, d = x1.shape
    tt = min(ROW_TT, seq)
    per_b = seq // tt
    return pl.pallas_call(
        _combine_kernel,
        grid=(t // tt,),
        in_specs=[pl.BlockSpec((tt * TOP_K,), lambda i: (i,), memory_space=pltpu.SMEM),
                  pl.BlockSpec(memory_space=pl.ANY),
                  pl.BlockSpec((tt, EXPERT_LANES), lambda i: (i, 0)),
                  pl.BlockSpec((tt, d), lambda i: (i, 0)),
                  pl.BlockSpec((1, 6, d), lambda i: (i // per_b, 0, 0))],
        out_specs=pl.BlockSpec((tt, d), lambda i: (i, 0)),
        out_shape=jax.ShapeDtypeStruct((t, d), F32),
        scratch_shapes=[pltpu.VMEM((TOP_K, tt, d), F32), pltpu.SemaphoreType.DMA(())],
        compiler_params=_cparams(("arbitrary",)),
        name="combine",
    )(pos_flat, ys, gates, x1, mod3)


def _rope_tables(seq):
    rows = seq // GRID_W
    row = jnp.repeat(jnp.arange(rows, dtype=F32), GRID_W)
    col = jnp.tile(jnp.arange(GRID_W, dtype=F32), rows)
    axis_dim = HEAD_DIM // 2
    inv = ROPE_THETA ** (-jnp.arange(0, axis_dim, 2, dtype=F32) / axis_dim)
    ang = jnp.concatenate([row[:, None] * inv, col[:, None] * inv], axis=-1)
    cos_rep = jnp.repeat(jnp.cos(ang), 2, axis=-1)
    sin_rep = jnp.repeat(jnp.sin(ang), 2, axis=-1)
    even = (jnp.arange(HEAD_DIM) % 2 == 0)[None, :]
    return cos_rep, jnp.where(even, -sin_rep, 0.0), jnp.where(even, 0.0, sin_rep)


def _layer(x, c, rel_table, w_ada, b_ada, norm_attn, norm_ffn, w_in, w_out, a_q_norm,
           a_k_norm, b_q_norm, b_k_norm, lq1, lk1, lq2, lk2, b_subln, w_router, b_router,
           w_gu, b_gu, w_down, b_down):
    batch, seq, d = x.shape
    t = batch * seq
    x2 = x.reshape(t, d)

    c_pad = jnp.pad(c, ((0, 8 - batch % 8 if batch % 8 else 0), (0, 0)))
    mod = _ada_mod(c_pad, w_ada, b_ada.reshape(1, -1))
    mod3 = mod[:batch].reshape(batch, 6, d)

    cos_rep, sin_even, sin_odd = _rope_tables(seq)
    row = lambda p: p.reshape(1, -1)
    proj = _in_proj(x2, mod3, row(norm_attn), w_in.astype(BF16), cos_rep, sin_even, sin_odd,
                    row(a_q_norm), row(a_k_norm), row(jnp.tile(b_q_norm, 2)),
                    row(jnp.tile(b_k_norm, 2)), seq)

    tq = min(ATT_TQ, seq)
    band = _bias_band(rel_table, tq)
    out_a = _attn_a(proj, batch, seq)
    out_b = _attn_b(rel_table, proj, band, row(lq1), row(lk1), row(lq2), row(lk2),
                    row(b_subln), batch, seq)

    n_e = w_router.shape[1]
    w_router_pad = jnp.pad(w_router, ((0, 0), (0, EXPERT_LANES - n_e)))
    wr_hi = w_router_pad.astype(BF16)
    wr_lo = (w_router_pad - wr_hi.astype(F32)).astype(BF16)
    wr_split = jnp.concatenate([wr_hi, wr_hi, wr_lo], axis=0)
    b_router_pad = jnp.pad(b_router, (0, EXPERT_LANES - n_e), constant_values=NEG_BIG)
    x1, h2, logits = _post_attn(out_a, out_b, x2, mod3, w_out.astype(BF16), row(norm_ffn),
                                wr_split, row(b_router_pad), seq)

    pos, gates, counts = _route(logits)
    pos_flat = pos[:, :TOP_K].reshape(-1)
    cnt = counts[0, :n_e]
    n_rows = t * TOP_K + n_e * EXP_SUB
    xs = _dispatch(_clear_plan(cnt, n_rows), pos_flat, h2, n_rows)

    meta = _visit_plan(cnt, n_rows, (w_gu.shape[2] // 2) // EXP_TF)
    ys = _experts(meta, xs, w_gu, b_gu.reshape(n_e, 1, -1), w_down, b_down.reshape(n_e, 1, -1))

    out = _combine(pos_flat, ys, gates, x1, mod3, seq)
    return out.reshape(batch, seq, d)


def kernel(x, c, rel_bias_table, w_ada, b_ada, norm_attn, norm_ffn, w_in, w_out, a_q_norm,
           a_k_norm, b_q_norm, b_k_norm, lambda_q1, lambda_k1, lambda_q2, lambda_k2, b_subln,
           w_router, b_router, w_gu, b_gu, w_down, b_down):
    return _layer(x, c, rel_bias_table, w_ada[0], b_ada[0], norm_attn[0], norm_ffn[0],
                  w_in[0], w_out[0], a_q_norm[0], a_k_norm[0], b_q_norm[0], b_k_norm[0],
                  lambda_q1[0], lambda_k1[0], lambda_q2[0], lambda_k2[0], b_subln[0],
                  w_router[0], b_router[0], w_gu[0], b_gu[0], w_down[0], b_down[0])
```

```python
import functools
import math

import jax
import jax.numpy as jnp
from jax import lax
from jax.experimental import pallas as pl
from jax.experimental.pallas import tpu as pltpu

F32 = jnp.float32
BF16 = jnp.bfloat16
HIGHEST = lax.Precision.HIGHEST

D_MODEL = 2048
HEAD_DIM = 128
A_Q_HEADS = 8
A_KV_HEADS = 2
A_GROUP = A_Q_HEADS // A_KV_HEADS
ROPE_THETA = 10000.0
GRID_W = 64
B_HEADS = 8
B_QK_DIM = 64
REL_BUCKETS = 32
N_EXPERTS = 32
TOP_K = 4
D_FF = D_MODEL
SWIGLU_LIMIT = 7.0
SWIGLU_ALPHA = 1.702
EPS = 1e-6
LAMBDA_INIT = 0.8 - 0.6 * math.exp(-0.3 * 0)

A_Q_W = A_Q_HEADS * HEAD_DIM
A_KV_W = A_KV_HEADS * HEAD_DIM
B_QK_W = B_HEADS * 2 * B_QK_DIM
B_V_W = B_HEADS * HEAD_DIM
OFF_KA = A_Q_W
OFF_VA = OFF_KA + A_KV_W
OFF_QB = OFF_VA + A_KV_W
OFF_KB = OFF_QB + B_QK_W
OFF_VB = OFF_KB + B_QK_W
IN_W = OFF_VB + B_V_W

LOG2E = 1.4426950408889634
LANES = 128
EXPERT_LANES = LANES
NEG_BIG = -1e30

VMEM_LIMIT = 56 * 1024 * 1024

ADA_TN = 1536
PROJ_TM = 512
PROJ_SEG = 512
ATT_TQ = 256
ATT_A_TK = 512
POST_TM = 512
POST_HALF = 256
ROUTE_TB = 512
ROW_TT = 256
EXP_SUB = 256
EXP_MAX_SUBS = 10
EXP_TF = 512


def _cparams(sem, vmem=VMEM_LIMIT):
    return pltpu.CompilerParams(dimension_semantics=sem, vmem_limit_bytes=vmem)


def _ada_kernel(c_ref, w_ref, b_ref, o_ref):
    c = c_ref[...]
    ca = c * jax.nn.sigmoid(c)
    o_ref[...] = jnp.dot(ca, w_ref[...], preferred_element_type=F32,
                         precision=HIGHEST) + b_ref[...]


def _ada_mod(c_pad, w_ada, b_ada):
    rows, d = c_pad.shape
    n = w_ada.shape[1]
    return pl.pallas_call(
        _ada_kernel,
        grid=(n // ADA_TN,),
        in_specs=[pl.BlockSpec((rows, d), lambda j: (0, 0)),
                  pl.BlockSpec((d, ADA_TN), lambda j: (0, j)),
                  pl.BlockSpec((1, ADA_TN), lambda j: (0, j))],
        out_specs=pl.BlockSpec((rows, ADA_TN), lambda j: (0, j)),
        out_shape=jax.ShapeDtypeStruct((rows, n), F32),
        compiler_params=_cparams(("arbitrary",)),
        name="ada_mod",
    )(c_pad, w_ada, b_ada)


def _inproj_kernel(x_ref, mod_ref, ng_ref, w_ref, cos_ref, se_ref, so_ref,
                   gaq_ref, gak_ref, gbq_ref, gbk_ref, o_ref):
    x = x_ref[...]
    ms = jnp.mean(x * x, axis=-1, keepdims=True)
    y = x * lax.rsqrt(ms + EPS) * ng_ref[...]
    h = y * (1.0 + mod_ref[0, 1:2, :]) + mod_ref[0, 0:1, :]
    hb = h.astype(BF16)

    cos = cos_ref[...]
    sin_even = se_ref[...]
    sin_odd = so_ref[...]
    lane = lax.broadcasted_iota(jnp.int32, (1, LANES), 1)
    low_half = lane < B_QK_DIM

    def rope(t):
        return (t * cos + pltpu.roll(t, LANES - 1, 1) * sin_even
                + pltpu.roll(t, 1, 1) * sin_odd)

    def norm_head(t, g):
        m = jnp.mean(t * t, axis=-1, keepdims=True)
        return t * lax.rsqrt(m + EPS) * g

    def norm_halves(t, g):
        sq = t * t
        s_lo = jnp.sum(jnp.where(low_half, sq, 0.0), axis=-1, keepdims=True)
        s_hi = jnp.sum(jnp.where(low_half, 0.0, sq), axis=-1, keepdims=True)
        m = jnp.where(low_half, s_lo, s_hi) * (1.0 / B_QK_DIM)
        return t * lax.rsqrt(m + EPS) * g

    a_scale = (HEAD_DIM ** -0.5) * LOG2E
    b_scale = (B_QK_DIM ** -0.5) * LOG2E
    for seg in range(IN_W // PROJ_SEG):
        acc = jnp.dot(hb, w_ref[:, seg * PROJ_SEG:(seg + 1) * PROJ_SEG],
                      preferred_element_type=F32)
        for j in range(PROJ_SEG // LANES):
            col = seg * PROJ_SEG + j * LANES
            t = acc[:, j * LANES:(j + 1) * LANES]
            if col < OFF_KA:
                t = rope(norm_head(t, gaq_ref[...])) * a_scale
            elif col < OFF_VA:
                t = rope(norm_head(t, gak_ref[...]))
            elif col < OFF_QB:
                pass
            elif col < OFF_KB:
                t = norm_halves(t, gbq_ref[...]) * b_scale
            elif col < OFF_VB:
                t = norm_halves(t, gbk_ref[...])
            o_ref[:, col:col + LANES] = t.astype(BF16)


def _in_proj(x2, mod3, norm_g, w_in_bf, cos_rep, sin_even, sin_odd, gaq, gak, gbq, gbk, seq):
    t, d = x2.shape
    tm = min(PROJ_TM, seq)
    per_b = seq // tm
    vec = lambda: pl.BlockSpec((1, LANES), lambda i: (0, 0))
    tab = lambda: pl.BlockSpec((tm, LANES), lambda i: (i % per_b, 0))
    return pl.pallas_call(
        _inproj_kernel,
        grid=(t // tm,),
        in_specs=[pl.BlockSpec((tm, d), lambda i: (i, 0)),
                  pl.BlockSpec((1, 6, d), lambda i: (i // per_b, 0, 0)),
                  pl.BlockSpec((1, d), lambda i: (0, 0)),
                  pl.BlockSpec((d, IN_W), lambda i: (0, 0), pipeline_mode=pl.Buffered(1)),
                  tab(), tab(), tab(), vec(), vec(), vec(), vec()],
        out_specs=pl.BlockSpec((tm, IN_W), lambda i: (i, 0)),
        out_shape=jax.ShapeDtypeStruct((t, IN_W), BF16),
        compiler_params=_cparams(("arbitrary",)),
        name="in_proj",
    )(x2, mod3, norm_g, w_in_bf, cos_rep, sin_even, sin_odd, gaq, gak, gbq, gbk)


BAND_CHUNKS = 5


def _bias_band_kernel(tab_ref, o_ref, *, tq):
    h = pl.program_id(0)
    width = BAND_CHUNKS * tq
    qq = lax.broadcasted_iota(jnp.int32, (tq, width), 0)
    kk = lax.broadcasted_iota(jnp.int32, (tq, width), 1)
    d = kk - (BAND_CHUNKS // 2) * tq - qq
    n = jnp.abs(d)
    n2 = n * n
    large = jnp.full_like(n, 8)
    for j in range(1, 8):
        large = large + (n2 >= 64 * 2 ** j).astype(jnp.int32)
    bucket = jnp.where(n < 8, n, large) + jnp.where(d > 0, 16, 0)
    acc = jnp.zeros((tq, width), F32)
    for b in range(REL_BUCKETS):
        acc = jnp.where(bucket == b, tab_ref[b, h], acc)
    o_ref[0] = acc * LOG2E


def _bias_band(rel_table, tq):
    return pl.pallas_call(
        functools.partial(_bias_band_kernel, tq=tq),
        grid=(B_HEADS,),
        in_specs=[pl.BlockSpec(memory_space=pltpu.SMEM)],
        out_specs=pl.BlockSpec((1, tq, BAND_CHUNKS * tq), lambda h: (h, 0, 0)),
        out_shape=jax.ShapeDtypeStruct((B_HEADS, tq, BAND_CHUNKS * tq), F32),
        compiler_params=_cparams(("arbitrary",)),
        name="bias_band",
    )(rel_table)


STAT_ROWS = 32
NORM_SLACK = 1.02


def _qk_stats_kernel(p_ref, o_ref):
    r = lax.broadcasted_iota(jnp.int32, (LANES, LANES), 0)
    c = lax.broadcasted_iota(jnp.int32, (LANES, LANES), 1)
    same_half = ((r < B_QK_DIM) == (c < B_QK_DIM)).astype(BF16)

    @pl.when(pl.program_id(1) == 0)
    def _():
        o_ref[...] = jnp.zeros_like(o_ref)

    for slab in range(OFF_VB // LANES):
        t = p_ref[:, slab * LANES:(slab + 1) * LANES].astype(F32)
        half_norm2 = jnp.dot((t * t).astype(BF16), same_half, preferred_element_type=F32)
        top = jnp.max(half_norm2, axis=0, keepdims=True)
        o_ref[0, slab:slab + 1, :] = jnp.maximum(o_ref[0, slab:slab + 1, :], top)


def _qk_stats(proj, batch, seq):
    tm = min(PROJ_TM, seq)
    per_b = seq // tm
    return pl.pallas_call(
        _qk_stats_kernel,
        grid=(batch, per_b),
        in_specs=[pl.BlockSpec((tm, IN_W), lambda b, j: (b * per_b + j, 0))],
        out_specs=pl.BlockSpec((1, STAT_ROWS, LANES), lambda b, j: (b, 0, 0)),
        out_shape=jax.ShapeDtypeStruct((batch, STAT_ROWS, LANES), F32),
        compiler_params=_cparams(("arbitrary", "arbitrary")),
        name="qk_stats",
    )(proj)


def _softmax_offsets(stats, rel_table):
    lo = stats[:, :, 0] * NORM_SLACK
    hi = stats[:, :, B_QK_DIM] * NORM_SLACK
    batch = stats.shape[0]
    s_ka, s_qb, s_kb, s_vb = (OFF_KA // LANES, OFF_QB // LANES, OFF_KB // LANES, OFF_VB // LANES)
    qa = jnp.max((lo + hi)[:, :s_ka].reshape(batch, A_KV_HEADS, A_GROUP), axis=-1)
    ka = (lo + hi)[:, s_ka:s_ka + A_KV_HEADS]
    off_a = jnp.sqrt(qa * ka)
    safe_a = 2.0 * off_a <= SAFE_SPAN
    bound1 = jnp.sqrt(lo[:, s_qb:s_kb] * lo[:, s_kb:s_vb])
    bound2 = jnp.sqrt(hi[:, s_qb:s_kb] * hi[:, s_kb:s_vb])
    bias_hi = jnp.max(rel_table, axis=0) * LOG2E
    bias_lo = jnp.min(rel_table, axis=0) * LOG2E
    off_b = jnp.stack([bound1 + bias_hi, bound2 + bias_hi], axis=-1)
    safe_b = 2.0 * jnp.maximum(bound1, bound2) + (bias_hi - bias_lo) <= SAFE_SPAN
    return (off_a.reshape(-1), safe_a.reshape(-1).astype(jnp.int32),
            off_b.reshape(-1), safe_b.reshape(-1).astype(jnp.int32))


def _flash_step(s, shift, v_c, m_ref, l_ref, acc_ref, idx):
    tk = s.shape[1]
    m_prev = m_ref[idx]
    m_cur = jnp.max(s, axis=1, keepdims=True) + shift
    m_new = jnp.maximum(m_prev, m_cur)
    alpha = jnp.exp2(m_prev - m_new)
    off = m_new - shift
    p = jnp.exp2(s - jnp.tile(off, (1, tk // LANES)))
    l_ref[idx] = alpha * l_ref[idx] + jnp.sum(p, axis=1, keepdims=True)
    acc_ref[idx] = alpha * acc_ref[idx] + jnp.dot(
        p.astype(BF16), v_c, preferred_element_type=F32)
    m_ref[idx] = m_new


_NT = (((1,), (1,)), ((), ()))
SAFE_SPAN = 100.0


def _extend_values(v_ref, vext_ref):
    vext_ref[:, :HEAD_DIM] = v_ref[...]
    vext_ref[:, HEAD_DIM:] = jnp.ones((v_ref.shape[0], HEAD_DIM), BF16)


def _fixed_offset_pass(q_rows, off, k_ref, vext_ref, accx_ref, tk, bias_of):
    offb = off
    accx_ref[...] = jnp.zeros(accx_ref.shape, F32)

    def chunk(c, carry):
        r0 = pl.multiple_of(c * tk, tk)
        s = lax.dot_general(q_rows, k_ref[pl.ds(r0, tk), :], _NT, preferred_element_type=F32)
        e = s - jnp.tile(offb, (1, tk // LANES))
        if bias_of is not None:
            e = e + bias_of(c)
        p = jnp.exp2(e).astype(BF16)
        accx_ref[...] += jnp.dot(p, vext_ref[pl.ds(r0, tk), :], preferred_element_type=F32)
        return carry

    lax.fori_loop(0, k_ref.shape[0] // tk, chunk, 0, unroll=True)


def _attn_a_kernel(off_ref, safe_ref, q_ref, k_ref, v_ref, o_ref, m_ref, l_ref, acc_ref,
                   vext_ref, accx_ref, *, tk):
    tq = q_ref.shape[0]
    seq = k_ref.shape[0]
    group = pl.program_id(0) * pl.num_programs(1) + pl.program_id(1)

    @pl.when(pl.program_id(2) == 0)
    def _():
        _extend_values(v_ref, vext_ref)

    q_rows = jnp.concatenate(
        [q_ref[:, g * HEAD_DIM:(g + 1) * HEAD_DIM] for g in range(A_GROUP)], axis=0)
    safe = safe_ref[group] == 1

    @pl.when(safe)
    def _():
        bound = jnp.full((A_GROUP * tq, LANES), off_ref[group], F32)
        _fixed_offset_pass(q_rows, bound, k_ref, vext_ref, accx_ref, tk, None)
        for g in range(A_GROUP):
            blk = accx_ref[g * tq:(g + 1) * tq, :]
            o_ref[:, g * HEAD_DIM:(g + 1) * HEAD_DIM] = (
                blk[:, :HEAD_DIM] / blk[:, HEAD_DIM:]).astype(BF16)

    @pl.when(jnp.logical_not(safe))
    def _():
        m_ref[...] = jnp.full(m_ref.shape, -jnp.inf, F32)
        l_ref[...] = jnp.zeros(l_ref.shape, F32)
        acc_ref[...] = jnp.zeros(acc_ref.shape, F32)

        def chunk(c, carry):
            r0 = pl.multiple_of(c * tk, tk)
            k_c = k_ref[pl.ds(r0, tk), :]
            v_c = v_ref[pl.ds(r0, tk), :]
            for g in range(A_GROUP):
                q = q_ref[:, g * HEAD_DIM:(g + 1) * HEAD_DIM]
                s = lax.dot_general(q, k_c, _NT, preferred_element_type=F32)
                _flash_step(s, 0.0, v_c, m_ref, l_ref, acc_ref, g)
            return carry

        lax.fori_loop(0, seq // tk, chunk, 0)
        for g in range(A_GROUP):
            o_ref[:, g * HEAD_DIM:(g + 1) * HEAD_DIM] = (acc_ref[g] / l_ref[g]).astype(BF16)


def _attn_a(off, safe, proj, batch, seq):
    t = proj.shape[0]
    tq = min(ATT_TQ, seq)
    tk = min(ATT_A_TK, seq)
    nq = seq // tq
    gw = A_GROUP * HEAD_DIM
    grid_spec = pltpu.PrefetchScalarGridSpec(
        num_scalar_prefetch=2,
        grid=(batch, A_KV_HEADS, nq),
        in_specs=[pl.BlockSpec((tq, gw), lambda b, g, i, *_: (b * nq + i, g)),
                  pl.BlockSpec((seq, HEAD_DIM), lambda b, g, i, *_: (b, OFF_KA // HEAD_DIM + g)),
                  pl.BlockSpec((seq, HEAD_DIM), lambda b, g, i, *_: (b, OFF_VA // HEAD_DIM + g))],
        out_specs=pl.BlockSpec((tq, gw), lambda b, g, i, *_: (b * nq + i, g)),
        scratch_shapes=[pltpu.VMEM((A_GROUP, tq, LANES), F32),
                        pltpu.VMEM((A_GROUP, tq, LANES), F32),
                        pltpu.VMEM((A_GROUP, tq, HEAD_DIM), F32),
                        pltpu.VMEM((seq, 2 * HEAD_DIM), BF16),
                        pltpu.VMEM((A_GROUP * tq, 2 * HEAD_DIM), F32)],
    )
    return pl.pallas_call(
        functools.partial(_attn_a_kernel, tk=tk),
        grid_spec=grid_spec,
        out_shape=jax.ShapeDtypeStruct((t, A_Q_W), BF16),
        compiler_params=_cparams(("arbitrary", "arbitrary", "arbitrary")),
        name="attn_a",
    )(off, safe, proj, proj, proj)


def _attn_b_kernel(tab_ref, off_ref, safe_ref, q_ref, k_ref, v_ref, band_ref, lq1_ref, lk1_ref,
                   lq2_ref, lk2_ref, sg_ref, o_ref, m_ref, l_ref, acc_ref, vext_ref, accx_ref):
    tq = q_ref.shape[0]
    seq = k_ref.shape[0]
    n_chunks = seq // tq
    mid = BAND_CHUNKS // 2
    h = pl.program_id(1)
    i = pl.program_id(2)
    head = pl.program_id(0) * pl.num_programs(1) + h
    lane = lax.broadcasted_iota(jnp.int32, (1, LANES), 1)
    low_half = lane < B_QK_DIM

    @pl.when(i == 0)
    def _():
        _extend_values(v_ref, vext_ref)

    q = q_ref[...]
    zero = jnp.zeros_like(q)
    q1 = jnp.where(low_half, q, zero)
    q2 = jnp.where(low_half, zero, q)
    safe = safe_ref[head] == 1

    lam1 = jnp.exp(jnp.sum(lq1_ref[...] * lk1_ref[...], axis=-1, keepdims=True))
    lam2 = jnp.exp(jnp.sum(lq2_ref[...] * lk2_ref[...], axis=-1, keepdims=True))
    lam = lam1 - lam2 + LAMBDA_INIT

    def finish(o1, o2):
        o = o1 - lam * o2
        ms = jnp.mean(o * o, axis=-1, keepdims=True)
        o = o * lax.rsqrt(ms + EPS) * sg_ref[...] * (1.0 - LAMBDA_INIT)
        o_ref[...] = o.astype(BF16)

    @pl.when(safe)
    def _():
        def bias_of(c):
            j = jnp.clip(c - i + mid, 0, BAND_CHUNKS - 1)
            bias = band_ref[0, :, pl.ds(pl.multiple_of(j * tq, tq), tq)]
            return jnp.concatenate([bias, bias], axis=0)

        off = jnp.concatenate([jnp.full((tq, LANES), off_ref[2 * head], F32),
                               jnp.full((tq, LANES), off_ref[2 * head + 1], F32)], axis=0)
        _fixed_offset_pass(jnp.concatenate([q1, q2], axis=0), off, k_ref,
                           vext_ref, accx_ref, tq, bias_of)
        a1 = accx_ref[0:tq, :]
        a2 = accx_ref[tq:2 * tq, :]
        finish(a1[:, :HEAD_DIM] / a1[:, HEAD_DIM:], a2[:, :HEAD_DIM] / a2[:, HEAD_DIM:])

    @pl.when(jnp.logical_not(safe))
    def _():
        m_ref[...] = jnp.full(m_ref.shape, -jnp.inf, F32)
        l_ref[...] = jnp.zeros(l_ref.shape, F32)
        acc_ref[...] = jnp.zeros(acc_ref.shape, F32)
        far_left = tab_ref[REL_BUCKETS // 2 - 1, h] * LOG2E
        far_right = tab_ref[REL_BUCKETS - 1, h] * LOG2E

        def step(c, shift, bias):
            r0 = pl.multiple_of(c * tq, tq)
            k_c = k_ref[pl.ds(r0, tq), :]
            v_c = v_ref[pl.ds(r0, tq), :]
            s1 = lax.dot_general(q1, k_c, _NT, preferred_element_type=F32)
            s2 = lax.dot_general(q2, k_c, _NT, preferred_element_type=F32)
            if bias is not None:
                s1 = s1 + bias
                s2 = s2 + bias
            _flash_step(s1, shift, v_c, m_ref, l_ref, acc_ref, 0)
            _flash_step(s2, shift, v_c, m_ref, l_ref, acc_ref, 1)

        def left(c, carry):
            step(c, far_left, None)
            return carry

        def right(c, carry):
            step(c, far_right, None)
            return carry

        lax.fori_loop(0, jnp.maximum(i - 1, 0), left, 0)
        for jj in range(3):
            c = i - 1 + jj

            @pl.when((c >= 0) & (c < n_chunks))
            def _():
                step(c, 0.0, band_ref[0, :, (mid - 1 + jj) * tq:(mid + jj) * tq])

        lax.fori_loop(jnp.minimum(i + 2, n_chunks), n_chunks, right, 0)
        finish(acc_ref[0] / l_ref[0], acc_ref[1] / l_ref[1])


def _attn_b(rel_table, off, safe, proj, band, lq1, lk1, lq2, lk2, subln, batch, seq):
    t = proj.shape[0]
    tq = band.shape[1]
    nq = seq // tq
    small = lambda w: pl.BlockSpec((1, w), lambda b, h, i, *_: (0, 0))
    grid_spec = pltpu.PrefetchScalarGridSpec(
        num_scalar_prefetch=3,
        grid=(batch, B_HEADS, nq),
        in_specs=[pl.BlockSpec((tq, HEAD_DIM), lambda b, h, i, *_: (b * nq + i, OFF_QB // HEAD_DIM + h)),
                  pl.BlockSpec((seq, HEAD_DIM), lambda b, h, i, *_: (b, OFF_KB // HEAD_DIM + h)),
                  pl.BlockSpec((seq, HEAD_DIM), lambda b, h, i, *_: (b, OFF_VB // HEAD_DIM + h)),
                  pl.BlockSpec((1, tq, BAND_CHUNKS * tq), lambda b, h, i, *_: (h, 0, 0)),
                  small(B_QK_DIM), small(B_QK_DIM), small(B_QK_DIM), small(B_QK_DIM),
                  small(HEAD_DIM)],
        out_specs=pl.BlockSpec((tq, HEAD_DIM), lambda b, h, i, *_: (b * nq + i, h)),
        scratch_shapes=[pltpu.VMEM((2, tq, LANES), F32),
                        pltpu.VMEM((2, tq, LANES), F32),
                        pltpu.VMEM((2, tq, HEAD_DIM), F32),
                        pltpu.VMEM((seq, 2 * HEAD_DIM), BF16),
                        pltpu.VMEM((2 * tq, 2 * HEAD_DIM), F32)],
    )
    return pl.pallas_call(
        _attn_b_kernel,
        grid_spec=grid_spec,
        out_shape=jax.ShapeDtypeStruct((t, B_V_W), BF16),
        compiler_params=_cparams(("arbitrary", "arbitrary", "arbitrary")),
        name="attn_b",
    )(rel_table, off, safe, proj, proj, proj, band, lq1, lk1, lq2, lk2, subln)


def _post_attn_kernel(oa_ref, ob_ref, x_ref, mod_ref, woa_ref, wob_ref, ng_ref, wr_ref,
                      br_ref, x1_ref, hp_ref, lg_ref):
    tm = x_ref.shape[0]
    d = x_ref.shape[1]
    for r in range(0, tm, POST_HALF):
        rows = slice(r, r + POST_HALF)
        mix = jnp.dot(oa_ref[rows, :], woa_ref[...], preferred_element_type=F32)
        mix = mix + jnp.dot(ob_ref[rows, :], wob_ref[...], preferred_element_type=F32)
        x1 = x_ref[rows, :] + mod_ref[0, 2:3, :] * mix
        x1_ref[rows, :] = x1
        ms = jnp.mean(x1 * x1, axis=-1, keepdims=True)
        y = x1 * lax.rsqrt(ms + EPS) * ng_ref[...]
        h2 = y * (1.0 + mod_ref[0, 4:5, :]) + mod_ref[0, 3:4, :]
        hp_ref[rows, :] = _pack_bf16_pairs(h2)
        hi = h2.astype(BF16)
        lo = (h2 - hi.astype(F32)).astype(BF16)
        lhs = jnp.concatenate([hi, lo, hi], axis=1)
        lg_ref[rows, :] = jnp.dot(lhs, wr_ref[...], preferred_element_type=F32) + br_ref[...]


def _post_attn(out_a, out_b, x2, mod3, w_out_bf, norm_g, w_router_pad, b_router_pad, seq):
    t, d = x2.shape
    tm = min(POST_TM, seq)
    per_b = seq // tm
    half = w_out_bf.shape[0] // 2
    return pl.pallas_call(
        _post_attn_kernel,
        grid=(t // tm,),
        in_specs=[pl.BlockSpec((tm, half), lambda i: (i, 0)),
                  pl.BlockSpec((tm, half), lambda i: (i, 0)),
                  pl.BlockSpec((tm, d), lambda i: (i, 0)),
                  pl.BlockSpec((1, 6, d), lambda i: (i // per_b, 0, 0)),
                  pl.BlockSpec((half, d), lambda i: (0, 0), pipeline_mode=pl.Buffered(1)),
                  pl.BlockSpec((half, d), lambda i: (1, 0), pipeline_mode=pl.Buffered(1)),
                  pl.BlockSpec((1, d), lambda i: (0, 0)),
                  pl.BlockSpec((3 * d, EXPERT_LANES), lambda i: (0, 0)),
                  pl.BlockSpec((1, EXPERT_LANES), lambda i: (0, 0))],
        out_specs=[pl.BlockSpec((tm, d), lambda i: (i, 0)),
                   pl.BlockSpec((tm, d // 2), lambda i: (i, 0)),
                   pl.BlockSpec((tm, EXPERT_LANES), lambda i: (i, 0))],
        out_shape=[jax.ShapeDtypeStruct((t, d), F32),
                   jax.ShapeDtypeStruct((t, d // 2), jnp.int32),
                   jax.ShapeDtypeStruct((t, EXPERT_LANES), F32)],
        compiler_params=_cparams(("arbitrary",)),
        name="post_attn",
    )(out_a, out_b, x2, mod3, w_out_bf, w_out_bf, norm_g, w_router_pad, b_router_pad)


def _route_kernel(lg_ref, pos_ref, gate_ref, cnt_ref, counts, start, carry):
    phase = pl.program_id(0)
    j = pl.program_id(1)
    tb = lg_ref.shape[0]
    lane_i = lax.broadcasted_iota(jnp.int32, (tb, EXPERT_LANES), 1)
    lane_f = lane_i.astype(F32)

    logit = lg_ref[...]
    vals, hots = [], []
    for _ in range(TOP_K):
        mk = jnp.max(logit, axis=1, keepdims=True)
        idx = jnp.min(jnp.where(logit == mk, lane_f, float(EXPERT_LANES)), axis=1, keepdims=True)
        hot = lane_f == idx
        logit = jnp.where(hot, -jnp.inf, logit)
        vals.append(mk)
        hots.append(hot)
    sel = jnp.zeros((tb, EXPERT_LANES), F32)
    for hot in hots:
        sel = sel + hot.astype(F32)
    col_sum = jnp.sum(sel, axis=0, keepdims=True)

    @pl.when((phase == 0) & (j == 0))
    def _():
        counts[...] = jnp.zeros_like(counts)

    @pl.when(phase == 0)
    def _():
        counts[...] += col_sum

    @pl.when((phase == 1) & (j == 0))
    def _():
        r = lax.broadcasted_iota(jnp.int32, (EXPERT_LANES, EXPERT_LANES), 0)
        c = lax.broadcasted_iota(jnp.int32, (EXPERT_LANES, EXPERT_LANES), 1)
        before = (r < c).astype(F32)
        padded = jnp.floor((counts[...] + (EXP_SUB - 1)) * (1.0 / EXP_SUB)) * EXP_SUB
        start[...] = jnp.dot(padded, before, preferred_element_type=F32, precision=HIGHEST)
        carry[...] = jnp.zeros_like(carry)
        cnt_ref[...] = counts[...].astype(jnp.int32)

    @pl.when(phase == 1)
    def _():
        r = lax.broadcasted_iota(jnp.int32, (tb, tb), 0)
        c = lax.broadcasted_iota(jnp.int32, (tb, tb), 1)
        earlier = (c < r).astype(BF16)
        prefix = jnp.dot(earlier, sel.astype(BF16), preferred_element_type=F32)
        base = prefix + carry[...] + start[...]
        exps = [jnp.exp(v - vals[0]) for v in vals]
        denom = exps[0] + exps[1] + exps[2] + exps[3]
        pos_out = jnp.zeros((tb, EXPERT_LANES), F32)
        gate_out = jnp.zeros((tb, EXPERT_LANES), F32)
        for k in range(TOP_K):
            pos_k = jnp.sum(jnp.where(hots[k], base, 0.0), axis=1, keepdims=True)
            pos_out = jnp.where(lane_i == k, pos_k, pos_out)
            gate_out = jnp.where(lane_i == k, exps[k] / denom, gate_out)
        pos_ref[...] = pos_out.astype(jnp.int32)
        gate_ref[...] = gate_out
        carry[...] += col_sum


def _route(logits):
    t = logits.shape[0]
    tb = min(ROUTE_TB, t)
    return pl.pallas_call(
        _route_kernel,
        grid=(2, t // tb),
        in_specs=[pl.BlockSpec((tb, EXPERT_LANES), lambda p, j: (j, 0))],
        out_specs=[pl.BlockSpec((tb, EXPERT_LANES), lambda p, j: (j * p, 0)),
                   pl.BlockSpec((tb, EXPERT_LANES), lambda p, j: (j * p, 0)),
                   pl.BlockSpec((1, EXPERT_LANES), lambda p, j: (0, 0))],
        out_shape=[jax.ShapeDtypeStruct((t, EXPERT_LANES), jnp.int32),
                   jax.ShapeDtypeStruct((t, EXPERT_LANES), F32),
                   jax.ShapeDtypeStruct((1, EXPERT_LANES), jnp.int32)],
        scratch_shapes=[pltpu.VMEM((1, EXPERT_LANES), F32),
                        pltpu.VMEM((1, EXPERT_LANES), F32),
                        pltpu.VMEM((1, EXPERT_LANES), F32)],
        compiler_params=_cparams(("arbitrary", "arbitrary")),
        name="route",
    )(logits)


def _row_copy(src_ref, src_row, dst_ref, dst_row, sem):
    return pltpu.make_async_copy(src_ref.at[pl.ds(src_row, 1), :],
                                 dst_ref.at[pl.ds(dst_row, 1), :], sem)


def _pack_bf16_pairs(x):
    half = x.shape[1] // 2
    bits = lax.bitcast_convert_type(x, jnp.int32)

    def rounded(b):
        lsb = lax.shift_right_logical(b, 16) & 1
        return b + 0x7FFF + lsb

    lo = lax.shift_right_logical(rounded(bits[:, :half]), 16)
    hi = rounded(bits[:, half:]) & jnp.int32(-65536)
    return lo | hi


def _unpack_bf16_pairs(w):
    lo = lax.bitcast_convert_type(lax.shift_left(w, 16), F32)
    hi = lax.bitcast_convert_type(w & jnp.int32(-65536), F32)
    return jnp.concatenate([lo, hi], axis=1)


def _dispatch_kernel(clr_ref, dst_ref, h_ref, xs_ref, zeros, csem, sem):
    tt = h_ref.shape[0]
    sub = zeros.shape[0]
    n_clear = clr_ref.shape[0]

    @pl.when(pl.program_id(0) == 0)
    def _():
        zeros[...] = jnp.zeros_like(zeros)

        def block_copy(blk):
            rows = pl.ds(pl.multiple_of(blk * sub, sub), sub)
            return pltpu.make_async_copy(zeros, xs_ref.at[rows, :], csem)

        def clear(b, carry):
            @pl.when(clr_ref[b] >= 0)
            def _():
                block_copy(clr_ref[b]).start()

            return carry

        def clear_done(b, carry):
            @pl.when(clr_ref[b] >= 0)
            def _():
                block_copy(0).wait()

            return carry

        lax.fori_loop(0, n_clear, clear, 0)
        lax.fori_loop(0, n_clear, clear_done, 0)

    def issue(r, carry):
        for k in range(TOP_K):
            _row_copy(h_ref, r, xs_ref, dst_ref[TOP_K * r + k], sem).start()
        return carry

    def drain(r, carry):
        for k in range(TOP_K):
            _row_copy(h_ref, 0, xs_ref, 0, sem).wait()
        return carry

    lax.fori_loop(0, tt, issue, 0)
    lax.fori_loop(0, tt, drain, 0)


def _dispatch(clear_blocks, pos_flat, hp, n_rows):
    t, dp = hp.shape
    tt = min(ROW_TT, t)
    grid_spec = pltpu.PrefetchScalarGridSpec(
        num_scalar_prefetch=1,
        grid=(t // tt,),
        in_specs=[pl.BlockSpec((tt * TOP_K,), lambda i, clr: (i,), memory_space=pltpu.SMEM),
                  pl.BlockSpec((tt, dp), lambda i, clr: (i, 0))],
        out_specs=pl.BlockSpec(memory_space=pl.ANY),
        scratch_shapes=[pltpu.VMEM((EXP_SUB, dp), jnp.int32), pltpu.SemaphoreType.DMA(()),
                        pltpu.SemaphoreType.DMA(())],
    )
    return pl.pallas_call(
        _dispatch_kernel,
        grid_spec=grid_spec,
        out_shape=jax.ShapeDtypeStruct((n_rows, dp), jnp.int32),
        compiler_params=_cparams(("arbitrary",)),
        name="dispatch",
    )(clear_blocks, pos_flat, hp)


def _clear_plan(counts, n_rows):
    n_e = counts.shape[0]
    n_blocks = n_rows // EXP_SUB
    pcb = (counts + EXP_SUB - 1) // EXP_SUB
    pend = jnp.cumsum(pcb)
    seg_last = jnp.where(pcb > 0, pend - 1, -1)
    tail = pend[-1] + jnp.arange(n_e, dtype=jnp.int32)
    tail = jnp.where(tail < n_blocks, tail, -1)
    return jnp.concatenate([seg_last, tail]).astype(jnp.int32)


VISIT_IDLE, VISIT_COMPUTE, VISIT_CLEAR = 0, 1, 2


def _experts_kernel(exp_ref, row0_ref, nsub_ref, kind_ref, feff_ref,
                    xs_ref, wg_ref, wu_ref, wd_ref, bg_ref, bu_ref, bd_ref, ys_ref,
                    xbuf, acc, xsem, osem, *, sub):
    v = pl.program_id(0)
    f = pl.program_id(1)
    last_f = pl.num_programs(1) - 1
    row0 = row0_ref[v]
    nsub = nsub_ref[v]
    kind = kind_ref[v]

    def x_copy(j, slot):
        rows = pl.ds(pl.multiple_of(row0 + j * sub, sub), sub)
        return pltpu.make_async_copy(xs_ref.at[rows, :], xbuf.at[slot], xsem.at[slot])

    def y_copy(j, src_row):
        rows = pl.ds(pl.multiple_of(row0 + j * sub, sub), sub)
        return pltpu.make_async_copy(acc.at[pl.ds(src_row, sub), :], ys_ref.at[rows, :], osem)

    def drain(j, carry):
        y_copy(0, 0).wait()
        return carry

    def x0_copy(visit):
        rows = pl.ds(pl.multiple_of(row0_ref[visit], sub), sub)
        return pltpu.make_async_copy(xs_ref.at[rows, :], xbuf.at[0], xsem.at[0])

    def run_visit(first, last):
        def compute(j, slot):
            xb = _unpack_bf16_pairs(xbuf[slot]).astype(BF16)
            g = jnp.dot(xb, wg_ref[0].astype(BF16), preferred_element_type=F32) + bg_ref[0]
            u = jnp.dot(xb, wu_ref[0].astype(BF16), preferred_element_type=F32) + bu_ref[0]
            g = jnp.minimum(g, SWIGLU_LIMIT)
            u = jnp.clip(u, -SWIGLU_LIMIT, SWIGLU_LIMIT)
            glu = g * jax.nn.sigmoid(SWIGLU_ALPHA * g)
            a = ((u + 1.0) * glu).astype(BF16)
            y = jnp.dot(a, wd_ref[0].astype(BF16), preferred_element_type=F32)
            r0 = pl.multiple_of(j * sub, sub)
            if first:
                acc[pl.ds(r0, sub), :] = y + bd_ref[0]
            else:
                acc[pl.ds(r0, sub), :] += y
            if last:
                y_copy(j, r0).start()

        if first:
            @pl.when(v == 0)
            def _():
                x0_copy(0).start()

            x0_copy(v).wait()
        more = nsub > 1

        @pl.when(more)
        def _():
            x_copy(1, 1).start()

        compute(0, 0)
        if last:
            nxt = jnp.minimum(v + 1, pl.num_programs(0) - 1)

            @pl.when(kind_ref[nxt] == VISIT_COMPUTE)
            def _():
                x0_copy(nxt).start()

        @pl.when(more)
        def _():
            def sub_block(j, carry):
                slot = 1 + (j - 1) % 2
                x_copy(j, slot).wait()
                x_copy(jnp.minimum(j + 1, nsub - 1), 3 - slot).start()
                compute(j, slot)
                return carry

            lax.fori_loop(1, nsub, sub_block, 0)
            x_copy(0, 1 + (nsub - 1) % 2).wait()

        if last:
            lax.fori_loop(0, nsub, drain, 0)

    active = (kind == VISIT_COMPUTE) & (nsub > 0)
    pl.when(active & (f == 0))(lambda: run_visit(True, False))
    pl.when(active & (f > 0) & (f < last_f))(lambda: run_visit(False, False))
    pl.when(active & (f == last_f))(lambda: run_visit(False, True))

    @pl.when((kind == VISIT_CLEAR) & (f == 0) & (nsub > 0))
    def _():
        acc[pl.ds(0, sub), :] = jnp.zeros((sub, acc.shape[1]), F32)

        def clear(j, carry):
            y_copy(j, 0).start()
            return carry

        lax.fori_loop(0, nsub, clear, 0)
        lax.fori_loop(0, nsub, drain, 0)


def _experts(meta, xs, w_gu, b_gu3, w_down, b_down3):
    n_rows, dp = xs.shape
    n_e, d, two_f = w_gu.shape
    ff = two_f // 2
    tf = EXP_TF
    nf = ff // tf
    n_visits = meta[0].shape[0]
    wspec = lambda shape, imap: pl.BlockSpec(shape, imap)
    grid_spec = pltpu.PrefetchScalarGridSpec(
        num_scalar_prefetch=5,
        grid=(n_visits, nf),
        in_specs=[
            pl.BlockSpec(memory_space=pl.ANY),
            wspec((1, d, tf), lambda v, f, ex, r0, ns, kd, fe: (ex[v], 0, fe[v * nf + f])),
            wspec((1, d, tf), lambda v, f, ex, r0, ns, kd, fe: (ex[v], 0, nf + fe[v * nf + f])),
            wspec((1, tf, d), lambda v, f, ex, r0, ns, kd, fe: (ex[v], fe[v * nf + f], 0)),
            wspec((1, 1, tf), lambda v, f, ex, r0, ns, kd, fe: (ex[v], 0, fe[v * nf + f])),
            wspec((1, 1, tf), lambda v, f, ex, r0, ns, kd, fe: (ex[v], 0, nf + fe[v * nf + f])),
            wspec((1, 1, d), lambda v, f, ex, r0, ns, kd, fe: (ex[v], 0, 0)),
        ],
        out_specs=pl.BlockSpec(memory_space=pl.ANY),
        scratch_shapes=[pltpu.VMEM((3, EXP_SUB, dp), jnp.int32),
                        pltpu.VMEM((EXP_MAX_SUBS * EXP_SUB, d), F32),
                        pltpu.SemaphoreType.DMA((3,)),
                        pltpu.SemaphoreType.DMA(())],
    )
    return pl.pallas_call(
        functools.partial(_experts_kernel, sub=EXP_SUB),
        grid_spec=grid_spec,
        out_shape=jax.ShapeDtypeStruct((n_rows, d), F32),
        compiler_params=_cparams(("arbitrary", "arbitrary")),
        name="experts",
    )(*meta, xs, w_gu, w_gu, w_down, b_gu3, b_gu3, b_down3)


def _visit_plan(counts, n_rows, nf):
    n_e = counts.shape[0]
    sub = EXP_SUB
    tmx = EXP_MAX_SUBS * sub
    pc = (counts + sub - 1) // sub * sub
    pend = jnp.cumsum(pc)
    pstart = pend - pc
    nvis = (pc + tmx - 1) // tmx
    vend = jnp.cumsum(nvis)
    vstart = vend - nvis
    total = vend[-1]
    n_visits = n_rows // tmx + n_e + 1
    v = jnp.arange(n_visits, dtype=jnp.int32)
    compute = v < total
    v_clamped = jnp.minimum(v, total - 1)
    e_v = jnp.minimum(jnp.sum(vend[None, :] <= v_clamped[:, None], axis=1), n_e - 1).astype(jnp.int32)
    part = v - vstart[e_v]
    row0 = jnp.where(compute, pstart[e_v] + part * tmx, 0)
    nsub = jnp.where(compute, jnp.minimum(pc[e_v] - part * tmx, tmx) // sub, 0)
    clear = v == total
    row0 = jnp.where(clear, jnp.minimum(pend[-1], n_rows - sub), row0).astype(jnp.int32)
    nsub = jnp.where(clear, (n_rows - pend[-1]) // sub, nsub).astype(jnp.int32)
    kind = jnp.where(compute, VISIT_COMPUTE, jnp.where(clear, VISIT_CLEAR, VISIT_IDLE))
    f = jnp.arange(nf, dtype=jnp.int32)
    feff = jnp.where(compute[:, None], f[None, :], nf - 1).reshape(-1).astype(jnp.int32)
    return e_v, row0, nsub, kind.astype(jnp.int32), feff


def _combine_kernel(pos_ref, ys_ref, gate_ref, x1_ref, mod_ref, o_ref, rows, sem):
    tt = x1_ref.shape[0]

    def issue(r, carry):
        for k in range(TOP_K):
            pltpu.make_async_copy(ys_ref.at[pl.ds(pos_ref[TOP_K * r + k], 1), :],
                                  rows.at[k, pl.ds(r, 1), :], sem).start()
        return carry

    def drain(r, carry):
        for k in range(TOP_K):
            pltpu.make_async_copy(ys_ref.at[pl.ds(0, 1), :],
                                  rows.at[0, pl.ds(0, 1), :], sem).wait()
        return carry

    lax.fori_loop(0, tt, issue, 0)
    lax.fori_loop(0, tt, drain, 0)
    gate = gate_ref[...]
    y = gate[:, 0:1] * rows[0]
    for k in range(1, TOP_K):
        y = y + gate[:, k:k + 1] * rows[k]
    o_ref[...] = x1_ref[...] + mod_ref[0, 5:6, :] * y


def _combine(pos_flat, ys, gates, x1, mod3, seq):
    t, d = x1.shape
    tt = min(ROW_TT, seq)
    per_b = seq // tt
    return pl.pallas_call(
        _combine_kernel,
        grid=(t // tt,),
        in_specs=[pl.BlockSpec((tt * TOP_K,), lambda i: (i,), memory_space=pltpu.SMEM),
                  pl.BlockSpec(memory_space=pl.ANY),
                  pl.BlockSpec((tt, EXPERT_LANES), lambda i: (i, 0)),
                  pl.BlockSpec((tt, d), lambda i: (i, 0)),
                  pl.BlockSpec((1, 6, d), lambda i: (i // per_b, 0, 0))],
        out_specs=pl.BlockSpec((tt, d), lambda i: (i, 0)),
        out_shape=jax.ShapeDtypeStruct((t, d), F32),
        scratch_shapes=[pltpu.VMEM((TOP_K, tt, d), F32), pltpu.SemaphoreType.DMA(())],
        compiler_params=_cparams(("arbitrary",)),
        name="combine",
    )(pos_flat, ys, gates, x1, mod3)


def _rope_tables(seq):
    rows = seq // GRID_W
    row = jnp.repeat(jnp.arange(rows, dtype=F32), GRID_W)
    col = jnp.tile(jnp.arange(GRID_W, dtype=F32), rows)
    axis_dim = HEAD_DIM // 2
    inv = ROPE_THETA ** (-jnp.arange(0, axis_dim, 2, dtype=F32) / axis_dim)
    ang = jnp.concatenate([row[:, None] * inv, col[:, None] * inv], axis=-1)
    cos_rep = jnp.repeat(jnp.cos(ang), 2, axis=-1)
    sin_rep = jnp.repeat(jnp.sin(ang), 2, axis=-1)
    even = (jnp.arange(HEAD_DIM) % 2 == 0)[None, :]
    return cos_rep, jnp.where(even, -sin_rep, 0.0), jnp.where(even, 0.0, sin_rep)


def _layer(x, c, rel_table, w_ada, b_ada, norm_attn, norm_ffn, w_in, w_out, a_q_norm,
           a_k_norm, b_q_norm, b_k_norm, lq1, lk1, lq2, lk2, b_subln, w_router, b_router,
           w_gu, b_gu, w_down, b_down):
    batch, seq, d = x.shape
    t = batch * seq
    x2 = x.reshape(t, d)

    c_pad = jnp.pad(c, ((0, 8 - batch % 8 if batch % 8 else 0), (0, 0)))
    mod = _ada_mod(c_pad, w_ada, b_ada.reshape(1, -1))
    mod3 = mod[:batch].reshape(batch, 6, d)

    cos_rep, sin_even, sin_odd = _rope_tables(seq)
    row = lambda p: p.reshape(1, -1)
    proj = _in_proj(x2, mod3, row(norm_attn), w_in.astype(BF16), cos_rep, sin_even, sin_odd,
                    row(a_q_norm), row(a_k_norm), row(jnp.tile(b_q_norm, 2)),
                    row(jnp.tile(b_k_norm, 2)), seq)

    tq = min(ATT_TQ, seq)
    band = _bias_band(rel_table, tq)
    off_a, safe_a, off_b, safe_b = _softmax_offsets(_qk_stats(proj, batch, seq), rel_table)
    out_a = _attn_a(off_a, safe_a, proj, batch, seq)
    out_b = _attn_b(rel_table, off_b, safe_b, proj, band, row(lq1), row(lk1), row(lq2), row(lk2),
                    row(b_subln), batch, seq)

    n_e = w_router.shape[1]
    w_router_pad = jnp.pad(w_router, ((0, 0), (0, EXPERT_LANES - n_e)))
    wr_hi = w_router_pad.astype(BF16)
    wr_lo = (w_router_pad - wr_hi.astype(F32)).astype(BF16)
    wr_split = jnp.concatenate([wr_hi, wr_hi, wr_lo], axis=0)
    b_router_pad = jnp.pad(b_router, (0, EXPERT_LANES - n_e), constant_values=NEG_BIG)
    x1, h2, logits = _post_attn(out_a, out_b, x2, mod3, w_out.astype(BF16), row(norm_ffn),
                                wr_split, row(b_router_pad), seq)

    pos, gates, counts = _route(logits)
    pos_flat = pos[:, :TOP_K].reshape(-1)
    cnt = counts[0, :n_e]
    n_rows = t * TOP_K + n_e * EXP_SUB
    xs = _dispatch(_clear_plan(cnt, n_rows), pos_flat, h2, n_rows)

    meta = _visit_plan(cnt, n_rows, (w_gu.shape[2] // 2) // EXP_TF)
    ys = _experts(meta, xs, w_gu, b_gu.reshape(n_e, 1, -1), w_down, b_down.reshape(n_e, 1, -1))

    out = _combine(pos_flat, ys, gates, x1, mod3, seq)
    return out.reshape(batch, seq, d)


def kernel(x, c, rel_bias_table, w_ada, b_ada, norm_attn, norm_ffn, w_in, w_out, a_q_norm,
           a_k_norm, b_q_norm, b_k_norm, lambda_q1, lambda_k1, lambda_q2, lambda_k2, b_subln,
           w_router, b_router, w_gu, b_gu, w_down, b_down):
    return _layer(x, c, rel_bias_table, w_ada[0], b_ada[0], norm_attn[0], norm_ffn[0],
                  w_in[0], w_out[0], a_q_norm[0], a_k_norm[0], b_q_norm[0], b_k_norm[0],
                  lambda_q1[0], lambda_k1[0], lambda_q2[0], lambda_k2[0], b_subln[0],
                  w_router[0], b_router[0], w_gu[0], b_gu[0], w_down[0], b_down[0])
```

```python
import functools
import math

import jax
import jax.numpy as jnp
from jax import lax
from jax.experimental import pallas as pl
from jax.experimental.pallas import tpu as pltpu

F32 = jnp.float32
BF16 = jnp.bfloat16
HIGHEST = lax.Precision.HIGHEST

D_MODEL = 2048
HEAD_DIM = 128
A_Q_HEADS = 8
A_KV_HEADS = 2
A_GROUP = A_Q_HEADS // A_KV_HEADS
ROPE_THETA = 10000.0
GRID_W = 64
B_HEADS = 8
B_QK_DIM = 64
REL_BUCKETS = 32
N_EXPERTS = 32
TOP_K = 4
D_FF = D_MODEL
SWIGLU_LIMIT = 7.0
SWIGLU_ALPHA = 1.702
EPS = 1e-6
LAMBDA_INIT = 0.8 - 0.6 * math.exp(-0.3 * 0)

A_Q_W = A_Q_HEADS * HEAD_DIM
A_KV_W = A_KV_HEADS * HEAD_DIM
B_QK_W = B_HEADS * 2 * B_QK_DIM
B_V_W = B_HEADS * HEAD_DIM
OFF_KA = A_Q_W
OFF_VA = OFF_KA + A_KV_W
OFF_QB = OFF_VA + A_KV_W
OFF_KB = OFF_QB + B_QK_W
OFF_VB = OFF_KB + B_QK_W
IN_W = OFF_VB + B_V_W

LOG2E = 1.4426950408889634
LANES = 128
EXPERT_LANES = LANES
NEG_BIG = -1e30

VMEM_LIMIT = 56 * 1024 * 1024

ADA_TN = 1536
PROJ_TM = 512
PROJ_SEG = 512
ATT_TQ = 256
ATT_A_TK = 512
POST_TM = 512
POST_HALF = 256
ROUTE_TB = 512
ROW_TT = 256
EXP_SUB = 256
EXP_MAX_SUBS = 9
ROW_DMA_PRIORITY = 1
EXP_TF = 512


def _cparams(sem, vmem=VMEM_LIMIT):
    return pltpu.CompilerParams(dimension_semantics=sem, vmem_limit_bytes=vmem)


def _ada_kernel(c_ref, w_ref, b_ref, o_ref):
    c = c_ref[...]
    ca = c * jax.nn.sigmoid(c)
    o_ref[...] = jnp.dot(ca, w_ref[...], preferred_element_type=F32,
                         precision=HIGHEST) + b_ref[...]


def _ada_mod(c_pad, w_ada, b_ada):
    rows, d = c_pad.shape
    n = w_ada.shape[1]
    return pl.pallas_call(
        _ada_kernel,
        grid=(n // ADA_TN,),
        in_specs=[pl.BlockSpec((rows, d), lambda j: (0, 0)),
                  pl.BlockSpec((d, ADA_TN), lambda j: (0, j)),
                  pl.BlockSpec((1, ADA_TN), lambda j: (0, j))],
        out_specs=pl.BlockSpec((rows, ADA_TN), lambda j: (0, j)),
        out_shape=jax.ShapeDtypeStruct((rows, n), F32),
        compiler_params=_cparams(("arbitrary",)),
        name="ada_mod",
    )(c_pad, w_ada, b_ada)


def _inproj_kernel(x_ref, mod_ref, ng_ref, w_ref, cos_ref, se_ref, so_ref,
                   gaq_ref, gak_ref, gbq_ref, gbk_ref, o_ref):
    x = x_ref[...]
    ms = jnp.mean(x * x, axis=-1, keepdims=True)
    y = x * lax.rsqrt(ms + EPS) * ng_ref[...]
    h = y * (1.0 + mod_ref[0, 1:2, :]) + mod_ref[0, 0:1, :]
    hb = h.astype(BF16)

    cos = cos_ref[...]
    sin_even = se_ref[...]
    sin_odd = so_ref[...]
    lane = lax.broadcasted_iota(jnp.int32, (1, LANES), 1)
    low_half = lane < B_QK_DIM

    def rope(t):
        return (t * cos + pltpu.roll(t, LANES - 1, 1) * sin_even
                + pltpu.roll(t, 1, 1) * sin_odd)

    def norm_head(t, g):
        m = jnp.mean(t * t, axis=-1, keepdims=True)
        return t * lax.rsqrt(m + EPS) * g

    def norm_halves(t, g):
        sq = t * t
        s_lo = jnp.sum(jnp.where(low_half, sq, 0.0), axis=-1, keepdims=True)
        s_hi = jnp.sum(jnp.where(low_half, 0.0, sq), axis=-1, keepdims=True)
        m = jnp.where(low_half, s_lo, s_hi) * (1.0 / B_QK_DIM)
        return t * lax.rsqrt(m + EPS) * g

    a_scale = (HEAD_DIM ** -0.5) * LOG2E
    b_scale = (B_QK_DIM ** -0.5) * LOG2E
    for seg in range(IN_W // PROJ_SEG):
        acc = jnp.dot(hb, w_ref[:, seg * PROJ_SEG:(seg + 1) * PROJ_SEG],
                      preferred_element_type=F32)
        for j in range(PROJ_SEG // LANES):
            col = seg * PROJ_SEG + j * LANES
            t = acc[:, j * LANES:(j + 1) * LANES]
            if col < OFF_KA:
                t = rope(norm_head(t, gaq_ref[...])) * a_scale
            elif col < OFF_VA:
                t = rope(norm_head(t, gak_ref[...]))
            elif col < OFF_QB:
                pass
            elif col < OFF_KB:
                t = norm_halves(t, gbq_ref[...]) * b_scale
            elif col < OFF_VB:
                t = norm_halves(t, gbk_ref[...])
            o_ref[:, col:col + LANES] = t.astype(BF16)


def _in_proj(x2, mod3, norm_g, w_in_bf, cos_rep, sin_even, sin_odd, gaq, gak, gbq, gbk, seq):
    t, d = x2.shape
    tm = min(PROJ_TM, seq)
    per_b = seq // tm
    vec = lambda: pl.BlockSpec((1, LANES), lambda i: (0, 0))
    tab = lambda: pl.BlockSpec((tm, LANES), lambda i: (i % per_b, 0))
    return pl.pallas_call(
        _inproj_kernel,
        grid=(t // tm,),
        in_specs=[pl.BlockSpec((tm, d), lambda i: (i, 0)),
                  pl.BlockSpec((1, 6, d), lambda i: (i // per_b, 0, 0)),
                  pl.BlockSpec((1, d), lambda i: (0, 0)),
                  pl.BlockSpec((d, IN_W), lambda i: (0, 0), pipeline_mode=pl.Buffered(1)),
                  tab(), tab(), tab(), vec(), vec(), vec(), vec()],
        out_specs=pl.BlockSpec((tm, IN_W), lambda i: (i, 0)),
        out_shape=jax.ShapeDtypeStruct((t, IN_W), BF16),
        compiler_params=_cparams(("arbitrary",)),
        name="in_proj",
    )(x2, mod3, norm_g, w_in_bf, cos_rep, sin_even, sin_odd, gaq, gak, gbq, gbk)


BAND_CHUNKS = 5


def _bias_band_kernel(tab_ref, o_ref, *, tq):
    h = pl.program_id(0)
    width = BAND_CHUNKS * tq
    qq = lax.broadcasted_iota(jnp.int32, (tq, width), 0)
    kk = lax.broadcasted_iota(jnp.int32, (tq, width), 1)
    d = kk - (BAND_CHUNKS // 2) * tq - qq
    n = jnp.abs(d)
    n2 = n * n
    large = jnp.full_like(n, 8)
    for j in range(1, 8):
        large = large + (n2 >= 64 * 2 ** j).astype(jnp.int32)
    bucket = jnp.where(n < 8, n, large) + jnp.where(d > 0, 16, 0)
    acc = jnp.zeros((tq, width), F32)
    for b in range(REL_BUCKETS):
        acc = jnp.where(bucket == b, tab_ref[b, h], acc)
    o_ref[0] = acc * LOG2E


def _bias_band(rel_table, tq):
    return pl.pallas_call(
        functools.partial(_bias_band_kernel, tq=tq),
        grid=(B_HEADS,),
        in_specs=[pl.BlockSpec(memory_space=pltpu.SMEM)],
        out_specs=pl.BlockSpec((1, tq, BAND_CHUNKS * tq), lambda h: (h, 0, 0)),
        out_shape=jax.ShapeDtypeStruct((B_HEADS, tq, BAND_CHUNKS * tq), F32),
        compiler_params=_cparams(("arbitrary",)),
        name="bias_band",
    )(rel_table)


STAT_ROWS = 32
NORM_SLACK = 1.02


def _qk_stats_kernel(p_ref, o_ref):
    r = lax.broadcasted_iota(jnp.int32, (LANES, LANES), 0)
    c = lax.broadcasted_iota(jnp.int32, (LANES, LANES), 1)
    same_half = ((r < B_QK_DIM) == (c < B_QK_DIM)).astype(BF16)

    @pl.when(pl.program_id(1) == 0)
    def _():
        o_ref[...] = jnp.zeros_like(o_ref)

    for slab in range(OFF_VB // LANES):
        t = p_ref[:, slab * LANES:(slab + 1) * LANES].astype(F32)
        half_norm2 = jnp.dot((t * t).astype(BF16), same_half, preferred_element_type=F32)
        top = jnp.max(half_norm2, axis=0, keepdims=True)
        o_ref[0, slab:slab + 1, :] = jnp.maximum(o_ref[0, slab:slab + 1, :], top)


def _qk_stats(proj, batch, seq):
    tm = min(PROJ_TM, seq)
    per_b = seq // tm
    return pl.pallas_call(
        _qk_stats_kernel,
        grid=(batch, per_b),
        in_specs=[pl.BlockSpec((tm, IN_W), lambda b, j: (b * per_b + j, 0))],
        out_specs=pl.BlockSpec((1, STAT_ROWS, LANES), lambda b, j: (b, 0, 0)),
        out_shape=jax.ShapeDtypeStruct((batch, STAT_ROWS, LANES), F32),
        compiler_params=_cparams(("arbitrary", "arbitrary")),
        name="qk_stats",
    )(proj)


def _softmax_offsets(stats, rel_table):
    lo = stats[:, :, 0] * NORM_SLACK
    hi = stats[:, :, B_QK_DIM] * NORM_SLACK
    batch = stats.shape[0]
    s_ka, s_qb, s_kb, s_vb = (OFF_KA // LANES, OFF_QB // LANES, OFF_KB // LANES, OFF_VB // LANES)
    qa = jnp.max((lo + hi)[:, :s_ka].reshape(batch, A_KV_HEADS, A_GROUP), axis=-1)
    ka = (lo + hi)[:, s_ka:s_ka + A_KV_HEADS]
    off_a = jnp.sqrt(qa * ka)
    safe_a = 2.0 * off_a <= SAFE_SPAN
    bound1 = jnp.sqrt(lo[:, s_qb:s_kb] * lo[:, s_kb:s_vb])
    bound2 = jnp.sqrt(hi[:, s_qb:s_kb] * hi[:, s_kb:s_vb])
    bias_hi = jnp.max(rel_table, axis=0) * LOG2E
    bias_lo = jnp.min(rel_table, axis=0) * LOG2E
    off_b = jnp.stack([bound1 + bias_hi, bound2 + bias_hi], axis=-1)
    safe_b = 2.0 * jnp.maximum(bound1, bound2) + (bias_hi - bias_lo) <= SAFE_SPAN
    return (off_a.reshape(-1), safe_a.reshape(-1).astype(jnp.int32),
            off_b.reshape(-1), safe_b.reshape(-1).astype(jnp.int32))


def _flash_step(s, shift, v_c, m_ref, l_ref, acc_ref, idx):
    tk = s.shape[1]
    m_prev = m_ref[idx]
    m_cur = jnp.max(s, axis=1, keepdims=True) + shift
    m_new = jnp.maximum(m_prev, m_cur)
    alpha = jnp.exp2(m_prev - m_new)
    off = m_new - shift
    p = jnp.exp2(s - jnp.tile(off, (1, tk // LANES)))
    l_ref[idx] = alpha * l_ref[idx] + jnp.sum(p, axis=1, keepdims=True)
    acc_ref[idx] = alpha * acc_ref[idx] + jnp.dot(
        p.astype(BF16), v_c, preferred_element_type=F32)
    m_ref[idx] = m_new


_NT = (((1,), (1,)), ((), ()))
SAFE_SPAN = 100.0


def _extend_values(v_ref, vext_ref):
    vext_ref[:, :HEAD_DIM] = v_ref[...]
    vext_ref[:, HEAD_DIM:] = jnp.ones((v_ref.shape[0], HEAD_DIM), BF16)


def _fixed_offset_pass(q_rows, off, k_ref, vext_ref, accx_ref, tk, bias_of):
    offb = off
    accx_ref[...] = jnp.zeros(accx_ref.shape, F32)

    def chunk(c, carry):
        r0 = pl.multiple_of(c * tk, tk)
        s = lax.dot_general(q_rows, k_ref[pl.ds(r0, tk), :], _NT, preferred_element_type=F32)
        e = s - jnp.tile(offb, (1, tk // LANES))
        if bias_of is not None:
            e = e + bias_of(c)
        p = jnp.exp2(e).astype(BF16)
        accx_ref[...] += jnp.dot(p, vext_ref[pl.ds(r0, tk), :], preferred_element_type=F32)
        return carry

    lax.fori_loop(0, k_ref.shape[0] // tk, chunk, 0, unroll=True)


def _attn_a_kernel(off_ref, safe_ref, q_ref, k_ref, v_ref, o_ref, m_ref, l_ref, acc_ref,
                   vext_ref, accx_ref, *, tk):
    tq = q_ref.shape[0]
    seq = k_ref.shape[0]
    group = pl.program_id(0) * pl.num_programs(1) + pl.program_id(1)

    @pl.when(pl.program_id(2) == 0)
    def _():
        _extend_values(v_ref, vext_ref)

    q_rows = jnp.concatenate(
        [q_ref[:, g * HEAD_DIM:(g + 1) * HEAD_DIM] for g in range(A_GROUP)], axis=0)
    safe = safe_ref[group] == 1

    @pl.when(safe)
    def _():
        bound = jnp.full((A_GROUP * tq, LANES), off_ref[group], F32)
        _fixed_offset_pass(q_rows, bound, k_ref, vext_ref, accx_ref, tk, None)
        for g in range(A_GROUP):
            blk = accx_ref[g * tq:(g + 1) * tq, :]
            o_ref[:, g * HEAD_DIM:(g + 1) * HEAD_DIM] = (
                blk[:, :HEAD_DIM] / blk[:, HEAD_DIM:]).astype(BF16)

    @pl.when(jnp.logical_not(safe))
    def _():
        m_ref[...] = jnp.full(m_ref.shape, -jnp.inf, F32)
        l_ref[...] = jnp.zeros(l_ref.shape, F32)
        acc_ref[...] = jnp.zeros(acc_ref.shape, F32)

        def chunk(c, carry):
            r0 = pl.multiple_of(c * tk, tk)
            k_c = k_ref[pl.ds(r0, tk), :]
            v_c = v_ref[pl.ds(r0, tk), :]
            for g in range(A_GROUP):
                q = q_ref[:, g * HEAD_DIM:(g + 1) * HEAD_DIM]
                s = lax.dot_general(q, k_c, _NT, preferred_element_type=F32)
                _flash_step(s, 0.0, v_c, m_ref, l_ref, acc_ref, g)
            return carry

        lax.fori_loop(0, seq // tk, chunk, 0)
        for g in range(A_GROUP):
            o_ref[:, g * HEAD_DIM:(g + 1) * HEAD_DIM] = (acc_ref[g] / l_ref[g]).astype(BF16)


def _attn_a(off, safe, proj, batch, seq):
    t = proj.shape[0]
    tq = min(ATT_TQ, seq)
    tk = min(ATT_A_TK, seq)
    nq = seq // tq
    gw = A_GROUP * HEAD_DIM
    grid_spec = pltpu.PrefetchScalarGridSpec(
        num_scalar_prefetch=2,
        grid=(batch, A_KV_HEADS, nq),
        in_specs=[pl.BlockSpec((tq, gw), lambda b, g, i, *_: (b * nq + i, g)),
                  pl.BlockSpec((seq, HEAD_DIM), lambda b, g, i, *_: (b, OFF_KA // HEAD_DIM + g)),
                  pl.BlockSpec((seq, HEAD_DIM), lambda b, g, i, *_: (b, OFF_VA // HEAD_DIM + g))],
        out_specs=pl.BlockSpec((tq, gw), lambda b, g, i, *_: (b * nq + i, g)),
        scratch_shapes=[pltpu.VMEM((A_GROUP, tq, LANES), F32),
                        pltpu.VMEM((A_GROUP, tq, LANES), F32),
                        pltpu.VMEM((A_GROUP, tq, HEAD_DIM), F32),
                        pltpu.VMEM((seq, 2 * HEAD_DIM), BF16),
                        pltpu.VMEM((A_GROUP * tq, 2 * HEAD_DIM), F32)],
    )
    return pl.pallas_call(
        functools.partial(_attn_a_kernel, tk=tk),
        grid_spec=grid_spec,
        out_shape=jax.ShapeDtypeStruct((t, A_Q_W), BF16),
        compiler_params=_cparams(("arbitrary", "arbitrary", "arbitrary")),
        name="attn_a",
    )(off, safe, proj, proj, proj)


def _attn_b_kernel(tab_ref, off_ref, safe_ref, q_ref, k_ref, v_ref, band_ref, lq1_ref, lk1_ref,
                   lq2_ref, lk2_ref, sg_ref, o_ref, m_ref, l_ref, acc_ref, vext_ref, accx_ref):
    tq = q_ref.shape[0]
    seq = k_ref.shape[0]
    n_chunks = seq // tq
    mid = BAND_CHUNKS // 2
    h = pl.program_id(1)
    i = pl.program_id(2)
    head = pl.program_id(0) * pl.num_programs(1) + h
    lane = lax.broadcasted_iota(jnp.int32, (1, LANES), 1)
    low_half = lane < B_QK_DIM

    @pl.when(i == 0)
    def _():
        _extend_values(v_ref, vext_ref)

    q = q_ref[...]
    zero = jnp.zeros_like(q)
    q1 = jnp.where(low_half, q, zero)
    q2 = jnp.where(low_half, zero, q)
    safe = safe_ref[head] == 1

    lam1 = jnp.exp(jnp.sum(lq1_ref[...] * lk1_ref[...], axis=-1, keepdims=True))
    lam2 = jnp.exp(jnp.sum(lq2_ref[...] * lk2_ref[...], axis=-1, keepdims=True))
    lam = lam1 - lam2 + LAMBDA_INIT

    def finish(o1, o2):
        o = o1 - lam * o2
        ms = jnp.mean(o * o, axis=-1, keepdims=True)
        o = o * lax.rsqrt(ms + EPS) * sg_ref[...] * (1.0 - LAMBDA_INIT)
        o_ref[...] = o.astype(BF16)

    @pl.when(safe)
    def _():
        def bias_of(c):
            j = jnp.clip(c - i + mid, 0, BAND_CHUNKS - 1)
            bias = band_ref[0, :, pl.ds(pl.multiple_of(j * tq, tq), tq)]
            return jnp.concatenate([bias, bias], axis=0)

        off = jnp.concatenate([jnp.full((tq, LANES), off_ref[2 * head], F32),
                               jnp.full((tq, LANES), off_ref[2 * head + 1], F32)], axis=0)
        _fixed_offset_pass(jnp.concatenate([q1, q2], axis=0), off, k_ref,
                           vext_ref, accx_ref, tq, bias_of)
        a1 = accx_ref[0:tq, :]
        a2 = accx_ref[tq:2 * tq, :]
        finish(a1[:, :HEAD_DIM] / a1[:, HEAD_DIM:], a2[:, :HEAD_DIM] / a2[:, HEAD_DIM:])

    @pl.when(jnp.logical_not(safe))
    def _():
        m_ref[...] = jnp.full(m_ref.shape, -jnp.inf, F32)
        l_ref[...] = jnp.zeros(l_ref.shape, F32)
        acc_ref[...] = jnp.zeros(acc_ref.shape, F32)
        far_left = tab_ref[REL_BUCKETS // 2 - 1, h] * LOG2E
        far_right = tab_ref[REL_BUCKETS - 1, h] * LOG2E

        def step(c, shift, bias):
            r0 = pl.multiple_of(c * tq, tq)
            k_c = k_ref[pl.ds(r0, tq), :]
            v_c = v_ref[pl.ds(r0, tq), :]
            s1 = lax.dot_general(q1, k_c, _NT, preferred_element_type=F32)
            s2 = lax.dot_general(q2, k_c, _NT, preferred_element_type=F32)
            if bias is not None:
                s1 = s1 + bias
                s2 = s2 + bias
            _flash_step(s1, shift, v_c, m_ref, l_ref, acc_ref, 0)
            _flash_step(s2, shift, v_c, m_ref, l_ref, acc_ref, 1)

        def left(c, carry):
            step(c, far_left, None)
            return carry

        def right(c, carry):
            step(c, far_right, None)
            return carry

        lax.fori_loop(0, jnp.maximum(i - 1, 0), left, 0)
        for jj in range(3):
            c = i - 1 + jj

            @pl.when((c >= 0) & (c < n_chunks))
            def _():
                step(c, 0.0, band_ref[0, :, (mid - 1 + jj) * tq:(mid + jj) * tq])

        lax.fori_loop(jnp.minimum(i + 2, n_chunks), n_chunks, right, 0)
        finish(acc_ref[0] / l_ref[0], acc_ref[1] / l_ref[1])


def _attn_b(rel_table, off, safe, proj, band, lq1, lk1, lq2, lk2, subln, batch, seq):
    t = proj.shape[0]
    tq = band.shape[1]
    nq = seq // tq
    small = lambda w: pl.BlockSpec((1, w), lambda b, h, i, *_: (0, 0))
    grid_spec = pltpu.PrefetchScalarGridSpec(
        num_scalar_prefetch=3,
        grid=(batch, B_HEADS, nq),
        in_specs=[pl.BlockSpec((tq, HEAD_DIM), lambda b, h, i, *_: (b * nq + i, OFF_QB // HEAD_DIM + h)),
                  pl.BlockSpec((seq, HEAD_DIM), lambda b, h, i, *_: (b, OFF_KB // HEAD_DIM + h)),
                  pl.BlockSpec((seq, HEAD_DIM), lambda b, h, i, *_: (b, OFF_VB // HEAD_DIM + h)),
                  pl.BlockSpec((1, tq, BAND_CHUNKS * tq), lambda b, h, i, *_: (h, 0, 0)),
                  small(B_QK_DIM), small(B_QK_DIM), small(B_QK_DIM), small(B_QK_DIM),
                  small(HEAD_DIM)],
        out_specs=pl.BlockSpec((tq, HEAD_DIM), lambda b, h, i, *_: (b * nq + i, h)),
        scratch_shapes=[pltpu.VMEM((2, tq, LANES), F32),
                        pltpu.VMEM((2, tq, LANES), F32),
                        pltpu.VMEM((2, tq, HEAD_DIM), F32),
                        pltpu.VMEM((seq, 2 * HEAD_DIM), BF16),
                        pltpu.VMEM((2 * tq, 2 * HEAD_DIM), F32)],
    )
    return pl.pallas_call(
        _attn_b_kernel,
        grid_spec=grid_spec,
        out_shape=jax.ShapeDtypeStruct((t, B_V_W), BF16),
        compiler_params=_cparams(("arbitrary", "arbitrary", "arbitrary")),
        name="attn_b",
    )(rel_table, off, safe, proj, proj, proj, band, lq1, lk1, lq2, lk2, subln)


def _post_attn_kernel(oa_ref, ob_ref, x_ref, mod_ref, woa_ref, wob_ref, ng_ref, wr_ref,
                      br_ref, x1_ref, hp_ref, lg_ref):
    tm = x_ref.shape[0]
    d = x_ref.shape[1]
    for r in range(0, tm, POST_HALF):
        rows = slice(r, r + POST_HALF)
        mix = jnp.dot(oa_ref[rows, :], woa_ref[...], preferred_element_type=F32)
        mix = mix + jnp.dot(ob_ref[rows, :], wob_ref[...], preferred_element_type=F32)
        x1 = x_ref[rows, :] + mod_ref[0, 2:3, :] * mix
        x1_ref[rows, :] = x1
        ms = jnp.mean(x1 * x1, axis=-1, keepdims=True)
        y = x1 * lax.rsqrt(ms + EPS) * ng_ref[...]
        h2 = y * (1.0 + mod_ref[0, 4:5, :]) + mod_ref[0, 3:4, :]
        hp_ref[rows, :] = _pack_bf16_pairs(h2)
        hi = h2.astype(BF16)
        lo = (h2 - hi.astype(F32)).astype(BF16)
        lhs = jnp.concatenate([hi, lo, hi], axis=1)
        lg_ref[rows, :] = jnp.dot(lhs, wr_ref[...], preferred_element_type=F32) + br_ref[...]


def _post_attn(out_a, out_b, x2, mod3, w_out_bf, norm_g, w_router_pad, b_router_pad, seq):
    t, d = x2.shape
    tm = min(POST_TM, seq)
    per_b = seq // tm
    half = w_out_bf.shape[0] // 2
    return pl.pallas_call(
        _post_attn_kernel,
        grid=(t // tm,),
        in_specs=[pl.BlockSpec((tm, half), lambda i: (i, 0)),
                  pl.BlockSpec((tm, half), lambda i: (i, 0)),
                  pl.BlockSpec((tm, d), lambda i: (i, 0)),
                  pl.BlockSpec((1, 6, d), lambda i: (i // per_b, 0, 0)),
                  pl.BlockSpec((half, d), lambda i: (0, 0), pipeline_mode=pl.Buffered(1)),
                  pl.BlockSpec((half, d), lambda i: (1, 0), pipeline_mode=pl.Buffered(1)),
                  pl.BlockSpec((1, d), lambda i: (0, 0)),
                  pl.BlockSpec((3 * d, EXPERT_LANES), lambda i: (0, 0)),
                  pl.BlockSpec((1, EXPERT_LANES), lambda i: (0, 0))],
        out_specs=[pl.BlockSpec((tm, d), lambda i: (i, 0)),
                   pl.BlockSpec((tm, d // 2), lambda i: (i, 0)),
                   pl.BlockSpec((tm, EXPERT_LANES), lambda i: (i, 0))],
        out_shape=[jax.ShapeDtypeStruct((t, d), F32),
                   jax.ShapeDtypeStruct((t, d // 2), jnp.int32),
                   jax.ShapeDtypeStruct((t, EXPERT_LANES), F32)],
        compiler_params=_cparams(("arbitrary",)),
        name="post_attn",
    )(out_a, out_b, x2, mod3, w_out_bf, w_out_bf, norm_g, w_router_pad, b_router_pad)


def _route_kernel(lg_ref, pos_ref, gate_ref, cnt_ref, counts, start, carry):
    phase = pl.program_id(0)
    j = pl.program_id(1)
    tb = lg_ref.shape[0]
    lane_i = lax.broadcasted_iota(jnp.int32, (tb, EXPERT_LANES), 1)
    lane_f = lane_i.astype(F32)

    logit = lg_ref[...]
    vals, hots = [], []
    for _ in range(TOP_K):
        mk = jnp.max(logit, axis=1, keepdims=True)
        idx = jnp.min(jnp.where(logit == mk, lane_f, float(EXPERT_LANES)), axis=1, keepdims=True)
        hot = lane_f == idx
        logit = jnp.where(hot, -jnp.inf, logit)
        vals.append(mk)
        hots.append(hot)
    sel = jnp.zeros((tb, EXPERT_LANES), F32)
    for hot in hots:
        sel = sel + hot.astype(F32)
    col_sum = jnp.sum(sel, axis=0, keepdims=True)

    @pl.when((phase == 0) & (j == 0))
    def _():
        counts[...] = jnp.zeros_like(counts)

    @pl.when(phase == 0)
    def _():
        counts[...] += col_sum

    @pl.when((phase == 1) & (j == 0))
    def _():
        r = lax.broadcasted_iota(jnp.int32, (EXPERT_LANES, EXPERT_LANES), 0)
        c = lax.broadcasted_iota(jnp.int32, (EXPERT_LANES, EXPERT_LANES), 1)
        before = (r < c).astype(F32)
        padded = jnp.floor((counts[...] + (EXP_SUB - 1)) * (1.0 / EXP_SUB)) * EXP_SUB
        start[...] = jnp.dot(padded, before, preferred_element_type=F32, precision=HIGHEST)
        carry[...] = jnp.zeros_like(carry)
        cnt_ref[...] = counts[...].astype(jnp.int32)

    @pl.when(phase == 1)
    def _():
        r = lax.broadcasted_iota(jnp.int32, (tb, tb), 0)
        c = lax.broadcasted_iota(jnp.int32, (tb, tb), 1)
        earlier = (c < r).astype(BF16)
        prefix = jnp.dot(earlier, sel.astype(BF16), preferred_element_type=F32)
        base = prefix + carry[...] + start[...]
        exps = [jnp.exp(v - vals[0]) for v in vals]
        denom = exps[0] + exps[1] + exps[2] + exps[3]
        pos_out = jnp.zeros((tb, EXPERT_LANES), F32)
        gate_out = jnp.zeros((tb, EXPERT_LANES), F32)
        for k in range(TOP_K):
            pos_k = jnp.sum(jnp.where(hots[k], base, 0.0), axis=1, keepdims=True)
            pos_out = jnp.where(lane_i == k, pos_k, pos_out)
            gate_out = jnp.where(lane_i == k, exps[k] / denom, gate_out)
        pos_ref[...] = pos_out.astype(jnp.int32)
        gate_ref[...] = gate_out
        carry[...] += col_sum


def _route(logits):
    t = logits.shape[0]
    tb = min(ROUTE_TB, t)
    return pl.pallas_call(
        _route_kernel,
        grid=(2, t // tb),
        in_specs=[pl.BlockSpec((tb, EXPERT_LANES), lambda p, j: (j, 0))],
        out_specs=[pl.BlockSpec((tb, EXPERT_LANES), lambda p, j: (j * p, 0)),
                   pl.BlockSpec((tb, EXPERT_LANES), lambda p, j: (j * p, 0)),
                   pl.BlockSpec((1, EXPERT_LANES), lambda p, j: (0, 0))],
        out_shape=[jax.ShapeDtypeStruct((t, EXPERT_LANES), jnp.int32),
                   jax.ShapeDtypeStruct((t, EXPERT_LANES), F32),
                   jax.ShapeDtypeStruct((1, EXPERT_LANES), jnp.int32)],
        scratch_shapes=[pltpu.VMEM((1, EXPERT_LANES), F32),
                        pltpu.VMEM((1, EXPERT_LANES), F32),
                        pltpu.VMEM((1, EXPERT_LANES), F32)],
        compiler_params=_cparams(("arbitrary", "arbitrary")),
        name="route",
    )(logits)


def _row_copy(src_ref, src_row, dst_ref, dst_row, sem):
    return pltpu.make_async_copy(src_ref.at[pl.ds(src_row, 1), :],
                                 dst_ref.at[pl.ds(dst_row, 1), :], sem)


def _pack_bf16_pairs(x):
    half = x.shape[1] // 2
    bits = lax.bitcast_convert_type(x, jnp.int32)

    def rounded(b):
        lsb = lax.shift_right_logical(b, 16) & 1
        return b + 0x7FFF + lsb

    lo = lax.shift_right_logical(rounded(bits[:, :half]), 16)
    hi = rounded(bits[:, half:]) & jnp.int32(-65536)
    return lo | hi


def _unpack_bf16_pairs(w):
    lo = lax.bitcast_convert_type(lax.shift_left(w, 16), F32)
    hi = lax.bitcast_convert_type(w & jnp.int32(-65536), F32)
    return jnp.concatenate([lo, hi], axis=1)


def _dispatch_kernel(clr_ref, dst_ref, h_ref, xs_ref, zeros, csem, sem):
    tt = h_ref.shape[0]
    sub = zeros.shape[0]
    n_clear = clr_ref.shape[0]

    @pl.when(pl.program_id(0) == 0)
    def _():
        zeros[...] = jnp.zeros_like(zeros)

        def block_copy(blk):
            rows = pl.ds(pl.multiple_of(blk * sub, sub), sub)
            return pltpu.make_async_copy(zeros, xs_ref.at[rows, :], csem)

        def clear(b, carry):
            @pl.when(clr_ref[b] >= 0)
            def _():
                block_copy(clr_ref[b]).start()

            return carry

        def clear_done(b, carry):
            @pl.when(clr_ref[b] >= 0)
            def _():
                block_copy(0).wait()

            return carry

        lax.fori_loop(0, n_clear, clear, 0)
        lax.fori_loop(0, n_clear, clear_done, 0)

    def issue(r, carry):
        for k in range(TOP_K):
            _row_copy(h_ref, r, xs_ref, dst_ref[TOP_K * r + k], sem).start()
        return carry

    def drain(r, carry):
        for k in range(TOP_K):
            _row_copy(h_ref, 0, xs_ref, 0, sem).wait()
        return carry

    lax.fori_loop(0, tt, issue, 0)
    lax.fori_loop(0, tt, drain, 0, unroll=8)


def _dispatch(clear_blocks, pos_flat, hp, n_rows):
    t, dp = hp.shape
    tt = min(ROW_TT, t)
    grid_spec = pltpu.PrefetchScalarGridSpec(
        num_scalar_prefetch=1,
        grid=(t // tt,),
        in_specs=[pl.BlockSpec((tt * TOP_K,), lambda i, clr: (i,), memory_space=pltpu.SMEM),
                  pl.BlockSpec((tt, dp), lambda i, clr: (i, 0))],
        out_specs=pl.BlockSpec(memory_space=pl.ANY),
        scratch_shapes=[pltpu.VMEM((EXP_SUB, dp), jnp.int32), pltpu.SemaphoreType.DMA(()),
                        pltpu.SemaphoreType.DMA(())],
    )
    return pl.pallas_call(
        _dispatch_kernel,
        grid_spec=grid_spec,
        out_shape=jax.ShapeDtypeStruct((n_rows, dp), jnp.int32),
        compiler_params=_cparams(("arbitrary",)),
        name="dispatch",
    )(clear_blocks, pos_flat, hp)


def _clear_plan(counts, n_rows):
    n_e = counts.shape[0]
    n_blocks = n_rows // EXP_SUB
    pcb = (counts + EXP_SUB - 1) // EXP_SUB
    pend = jnp.cumsum(pcb)
    seg_last = jnp.where(pcb > 0, pend - 1, -1)
    tail = pend[-1] + jnp.arange(n_e, dtype=jnp.int32)
    tail = jnp.where(tail < n_blocks, tail, -1)
    return jnp.concatenate([seg_last, tail]).astype(jnp.int32)


VISIT_IDLE, VISIT_COMPUTE, VISIT_CLEAR = 0, 1, 2


def _experts_kernel(exp_ref, row0_ref, nsub_ref, kind_ref, feff_ref,
                    xs_ref, wg_ref, wu_ref, wd_ref, bg_ref, bu_ref, bd_ref, ys_ref,
                    xbuf, acc, wg_s, wu_s, wd_s, xsem, osem, *, sub):
    v = pl.program_id(0)
    f = pl.program_id(1)
    last_f = pl.num_programs(1) - 1
    row0 = row0_ref[v]
    nsub = nsub_ref[v]
    kind = kind_ref[v]

    def x_copy(j, slot):
        rows = pl.ds(pl.multiple_of(row0 + j * sub, sub), sub)
        return pltpu.make_async_copy(xs_ref.at[rows, :], xbuf.at[slot], xsem.at[slot])

    def y_copy(j, src_row):
        rows = pl.ds(pl.multiple_of(row0 + j * sub, sub), sub)
        return pltpu.make_async_copy(acc.at[pl.ds(src_row, sub), :], ys_ref.at[rows, :], osem)

    def drain(j, carry):
        y_copy(0, 0).wait()
        return carry

    def x0_copy(visit):
        rows = pl.ds(pl.multiple_of(row0_ref[visit], sub), sub)
        return pltpu.make_async_copy(xs_ref.at[rows, :], xbuf.at[0], xsem.at[0])

    def run_visit(first, last):
        def compute(j, slot):
            xb = _unpack_bf16_pairs(xbuf[slot]).astype(BF16)
            g = jnp.dot(xb, wg_s[...], preferred_element_type=F32) + bg_ref[0]
            u = jnp.dot(xb, wu_s[...], preferred_element_type=F32) + bu_ref[0]
            g = jnp.minimum(g, SWIGLU_LIMIT)
            u = jnp.clip(u, -SWIGLU_LIMIT, SWIGLU_LIMIT)
            glu = g * jax.nn.sigmoid(SWIGLU_ALPHA * g)
            a = ((u + 1.0) * glu).astype(BF16)
            y = jnp.dot(a, wd_s[...], preferred_element_type=F32)
            r0 = pl.multiple_of(j * sub, sub)
            if first:
                acc[pl.ds(r0, sub), :] = y + bd_ref[0]
            else:
                acc[pl.ds(r0, sub), :] += y
            if last:
                y_copy(j, r0).start(priority=ROW_DMA_PRIORITY)

        if first:
            @pl.when(v == 0)
            def _():
                x0_copy(0).start(priority=ROW_DMA_PRIORITY)

        more = nsub > 1

        @pl.when(more)
        def _():
            x_copy(1, 1).start(priority=ROW_DMA_PRIORITY)

        wg_s[...] = wg_ref[0].astype(BF16)
        wu_s[...] = wu_ref[0].astype(BF16)
        wd_s[...] = wd_ref[0].astype(BF16)
        if first:
            x0_copy(v).wait()
        compute(0, 0)
        if last:
            nxt = jnp.minimum(v + 1, pl.num_programs(0) - 1)

            @pl.when(kind_ref[nxt] == VISIT_COMPUTE)
            def _():
                x0_copy(nxt).start(priority=ROW_DMA_PRIORITY)

        @pl.when(more)
        def _():
            def sub_block(j, carry):
                slot = 1 + (j - 1) % 2
                x_copy(j, slot).wait()
                x_copy(jnp.minimum(j + 1, nsub - 1), 3 - slot).start(priority=ROW_DMA_PRIORITY)
                compute(j, slot)
                return carry

            lax.fori_loop(1, nsub, sub_block, 0)
            x_copy(0, 1 + (nsub - 1) % 2).wait()

        if last:
            lax.fori_loop(0, nsub, drain, 0)

    active = (kind == VISIT_COMPUTE) & (nsub > 0)
    pl.when(active & (f == 0))(lambda: run_visit(True, False))
    pl.when(active & (f > 0) & (f < last_f))(lambda: run_visit(False, False))
    pl.when(active & (f == last_f))(lambda: run_visit(False, True))

    @pl.when((kind == VISIT_CLEAR) & (f == 0) & (nsub > 0))
    def _():
        acc[pl.ds(0, sub), :] = jnp.zeros((sub, acc.shape[1]), F32)

        def clear(j, carry):
            y_copy(j, 0).start()
            return carry

        lax.fori_loop(0, nsub, clear, 0)
        lax.fori_loop(0, nsub, drain, 0)


def _experts(meta, xs, w_gu, b_gu3, w_down, b_down3):
    n_rows, dp = xs.shape
    n_e, d, two_f = w_gu.shape
    ff = two_f // 2
    tf = EXP_TF
    nf = ff // tf
    n_visits = meta[0].shape[0]
    wspec = lambda shape, imap: pl.BlockSpec(shape, imap)
    grid_spec = pltpu.PrefetchScalarGridSpec(
        num_scalar_prefetch=5,
        grid=(n_visits, nf),
        in_specs=[
            pl.BlockSpec(memory_space=pl.ANY),
            wspec((1, d, tf), lambda v, f, ex, r0, ns, kd, fe: (ex[v], 0, fe[v * nf + f])),
            wspec((1, d, tf), lambda v, f, ex, r0, ns, kd, fe: (ex[v], 0, nf + fe[v * nf + f])),
            wspec((1, tf, d), lambda v, f, ex, r0, ns, kd, fe: (ex[v], fe[v * nf + f], 0)),
            wspec((1, 1, tf), lambda v, f, ex, r0, ns, kd, fe: (ex[v], 0, fe[v * nf + f])),
            wspec((1, 1, tf), lambda v, f, ex, r0, ns, kd, fe: (ex[v], 0, nf + fe[v * nf + f])),
            wspec((1, 1, d), lambda v, f, ex, r0, ns, kd, fe: (ex[v], 0, 0)),
        ],
        out_specs=pl.BlockSpec(memory_space=pl.ANY),
        scratch_shapes=[pltpu.VMEM((3, EXP_SUB, dp), jnp.int32),
                        pltpu.VMEM((EXP_MAX_SUBS * EXP_SUB, d), F32),
                        pltpu.VMEM((d, tf), BF16), pltpu.VMEM((d, tf), BF16),
                        pltpu.VMEM((tf, d), BF16),
                        pltpu.SemaphoreType.DMA((3,)),
                        pltpu.SemaphoreType.DMA(())],
    )
    return pl.pallas_call(
        functools.partial(_experts_kernel, sub=EXP_SUB),
        grid_spec=grid_spec,
        out_shape=jax.ShapeDtypeStruct((n_rows, d), F32),
        compiler_params=_cparams(("arbitrary", "arbitrary")),
        name="experts",
    )(*meta, xs, w_gu, w_gu, w_down, b_gu3, b_gu3, b_down3)


def _visit_plan(counts, n_rows, nf):
    n_e = counts.shape[0]
    sub = EXP_SUB
    tmx = EXP_MAX_SUBS * sub
    pc = (counts + sub - 1) // sub * sub
    pend = jnp.cumsum(pc)
    pstart = pend - pc
    nvis = (pc + tmx - 1) // tmx
    vend = jnp.cumsum(nvis)
    vstart = vend - nvis
    total = vend[-1]
    n_visits = n_rows // tmx + n_e + 1
    v = jnp.arange(n_visits, dtype=jnp.int32)
    compute = v < total
    v_clamped = jnp.minimum(v, total - 1)
    e_v = jnp.minimum(jnp.sum(vend[None, :] <= v_clamped[:, None], axis=1), n_e - 1).astype(jnp.int32)
    part = v - vstart[e_v]
    row0 = jnp.where(compute, pstart[e_v] + part * tmx, 0)
    nsub = jnp.where(compute, jnp.minimum(pc[e_v] - part * tmx, tmx) // sub, 0)
    clear = v == total
    row0 = jnp.where(clear, jnp.minimum(pend[-1], n_rows - sub), row0).astype(jnp.int32)
    nsub = jnp.where(clear, (n_rows - pend[-1]) // sub, nsub).astype(jnp.int32)
    kind = jnp.where(compute, VISIT_COMPUTE, jnp.where(clear, VISIT_CLEAR, VISIT_IDLE))
    f = jnp.arange(nf, dtype=jnp.int32)
    feff = jnp.where(compute[:, None], f[None, :], nf - 1).reshape(-1).astype(jnp.int32)
    return e_v, row0, nsub, kind.astype(jnp.int32), feff


def _combine_kernel(pos_ref, ys_ref, gate_ref, x1_ref, mod_ref, o_ref, rows, sem):
    tt = x1_ref.shape[0]

    def issue(r, carry):
        for k in range(TOP_K):
            pltpu.make_async_copy(ys_ref.at[pl.ds(pos_ref[TOP_K * r + k], 1), :],
                                  rows.at[k, pl.ds(r, 1), :], sem).start()
        return carry

    def drain(r, carry):
        for k in range(TOP_K):
            pltpu.make_async_copy(ys_ref.at[pl.ds(0, 1), :],
                                  rows.at[0, pl.ds(0, 1), :], sem).wait()
        return carry

    lax.fori_loop(0, tt, issue, 0)
    lax.fori_loop(0, tt, drain, 0, unroll=8)
    gate = gate_ref[...]
    y = gate[:, 0:1] * rows[0]
    for k in range(1, TOP_K):
        y = y + gate[:, k:k + 1] * rows[k]
    o_ref[...] = x1_ref[...] + mod_ref[0, 5:6, :] * y


def _combine(pos_flat, ys, gates, x1, mod3, seq):
    t, d = x1.shape
    tt = min(ROW_TT, seq)
    per_b = seq // tt
    return pl.pallas_call(
        _combine_kernel,
        grid=(t // tt,),
        in_specs=[pl.BlockSpec((tt * TOP_K,), lambda i: (i,), memory_space=pltpu.SMEM),
                  pl.BlockSpec(memory_space=pl.ANY),
                  pl.BlockSpec((tt, EXPERT_LANES), lambda i: (i, 0)),
                  pl.BlockSpec((tt, d), lambda i: (i, 0)),
                  pl.BlockSpec((1, 6, d), lambda i: (i // per_b, 0, 0))],
        out_specs=pl.BlockSpec((tt, d), lambda i: (i, 0)),
        out_shape=jax.ShapeDtypeStruct((t, d), F32),
        scratch_shapes=[pltpu.VMEM((TOP_K, tt, d), F32), pltpu.SemaphoreType.DMA(())],
        compiler_params=_cparams(("arbitrary",)),
        name="combine",
    )(pos_flat, ys, gates, x1, mod3)


def _rope_tables(seq):
    rows = seq // GRID_W
    row = jnp.repeat(jnp.arange(rows, dtype=F32), GRID_W)
    col = jnp.tile(jnp.arange(GRID_W, dtype=F32), rows)
    axis_dim = HEAD_DIM // 2
    inv = ROPE_THETA ** (-jnp.arange(0, axis_dim, 2, dtype=F32) / axis_dim)
    ang = jnp.concatenate([row[:, None] * inv, col[:, None] * inv], axis=-1)
    cos_rep = jnp.repeat(jnp.cos(ang), 2, axis=-1)
    sin_rep = jnp.repeat(jnp.sin(ang), 2, axis=-1)
    even = (jnp.arange(HEAD_DIM) % 2 == 0)[None, :]
    return cos_rep, jnp.where(even, -sin_rep, 0.0), jnp.where(even, 0.0, sin_rep)


def _layer(x, c, rel_table, w_ada, b_ada, norm_attn, norm_ffn, w_in, w_out, a_q_norm,
           a_k_norm, b_q_norm, b_k_norm, lq1, lk1, lq2, lk2, b_subln, w_router, b_router,
           w_gu, b_gu, w_down, b_down):
    batch, seq, d = x.shape
    t = batch * seq
    x2 = x.reshape(t, d)

    c_pad = jnp.pad(c, ((0, 8 - batch % 8 if batch % 8 else 0), (0, 0)))
    mod = _ada_mod(c_pad, w_ada, b_ada.reshape(1, -1))
    mod3 = mod[:batch].reshape(batch, 6, d)

    cos_rep, sin_even, sin_odd = _rope_tables(seq)
    row = lambda p: p.reshape(1, -1)
    proj = _in_proj(x2, mod3, row(norm_attn), w_in.astype(BF16), cos_rep, sin_even, sin_odd,
                    row(a_q_norm), row(a_k_norm), row(jnp.tile(b_q_norm, 2)),
                    row(jnp.tile(b_k_norm, 2)), seq)

    tq = min(ATT_TQ, seq)
    band = _bias_band(rel_table, tq)
    off_a, safe_a, off_b, safe_b = _softmax_offsets(_qk_stats(proj, batch, seq), rel_table)
    out_a = _attn_a(off_a, safe_a, proj, batch, seq)
    out_b = _attn_b(rel_table, off_b, safe_b, proj, band, row(lq1), row(lk1), row(lq2), row(lk2),
                    row(b_subln), batch, seq)

    n_e = w_router.shape[1]
    w_router_pad = jnp.pad(w_router, ((0, 0), (0, EXPERT_LANES - n_e)))
    wr_hi = w_router_pad.astype(BF16)
    wr_lo = (w_router_pad - wr_hi.astype(F32)).astype(BF16)
    wr_split = jnp.concatenate([wr_hi, wr_hi, wr_lo], axis=0)
    b_router_pad = jnp.pad(b_router, (0, EXPERT_LANES - n_e), constant_values=NEG_BIG)
    x1, h2, logits = _post_attn(out_a, out_b, x2, mod3, w_out.astype(BF16), row(norm_ffn),
                                wr_split, row(b_router_pad), seq)

    pos, gates, counts = _route(logits)
    pos_flat = pos[:, :TOP_K].reshape(-1)
    cnt = counts[0, :n_e]
    n_rows = t * TOP_K + n_e * EXP_SUB
    xs = _dispatch(_clear_plan(cnt, n_rows), pos_flat, h2, n_rows)

    meta = _visit_plan(cnt, n_rows, (w_gu.shape[2] // 2) // EXP_TF)
    ys = _experts(meta, xs, w_gu, b_gu.reshape(n_e, 1, -1), w_down, b_down.reshape(n_e, 1, -1))

    out = _combine(pos_flat, ys, gates, x1, mod3, seq)
    return out.reshape(batch, seq, d)


def kernel(x, c, rel_bias_table, w_ada, b_ada, norm_attn, norm_ffn, w_in, w_out, a_q_norm,
           a_k_norm, b_q_norm, b_k_norm, lambda_q1, lambda_k1, lambda_q2, lambda_k2, b_subln,
           w_router, b_router, w_gu, b_gu, w_down, b_down):
    return _layer(x, c, rel_bias_table, w_ada[0], b_ada[0], norm_attn[0], norm_ffn[0],
                  w_in[0], w_out[0], a_q_norm[0], a_k_norm[0], b_q_norm[0], b_k_norm[0],
                  lambda_q1[0], lambda_k1[0], lambda_q2[0], lambda_k2[0], b_subln[0],
                  w_router[0], b_router[0], w_gu[0], b_gu[0], w_down[0], b_down[0])
```

```python
import functools
import math

import jax
import jax.numpy as jnp
from jax import lax
from jax.experimental import pallas as pl
from jax.experimental.pallas import tpu as pltpu

F32 = jnp.float32
BF16 = jnp.bfloat16
HIGHEST = lax.Precision.HIGHEST

D_MODEL = 2048
HEAD_DIM = 128
A_Q_HEADS = 8
A_KV_HEADS = 2
A_GROUP = A_Q_HEADS // A_KV_HEADS
ROPE_THETA = 10000.0
GRID_W = 64
B_HEADS = 8
B_QK_DIM = 64
REL_BUCKETS = 32
N_EXPERTS = 32
TOP_K = 4
D_FF = D_MODEL
SWIGLU_LIMIT = 7.0
SWIGLU_ALPHA = 1.702
EPS = 1e-6
LAMBDA_INIT = 0.8 - 0.6 * math.exp(-0.3 * 0)

A_Q_W = A_Q_HEADS * HEAD_DIM
A_KV_W = A_KV_HEADS * HEAD_DIM
B_QK_W = B_HEADS * 2 * B_QK_DIM
B_V_W = B_HEADS * HEAD_DIM
OFF_KA = A_Q_W
OFF_VA = OFF_KA + A_KV_W
OFF_QB = OFF_VA + A_KV_W
OFF_KB = OFF_QB + B_QK_W
OFF_VB = OFF_KB + B_QK_W
IN_W = OFF_VB + B_V_W

LOG2E = 1.4426950408889634
LANES = 128
EXPERT_LANES = LANES
NEG_BIG = -1e30

VMEM_LIMIT = 56 * 1024 * 1024

ADA_TN = 1536
PROJ_TM = 512
PROJ_SEG = 512
ATT_TQ = 256
ATT_B_TQ = 512
ATT_A_TK = 512
POST_TM = 512
POST_HALF = 256
ROUTE_TB = 512
ROW_TT = 256
EXP_SUB = 256
EXP_UNIT = 512
EXP_MAX_UNITS = 4
EXPERTS_VMEM_LIMIT = 60 * 1024 * 1024
ROW_DMA_PRIORITY = 1
EXP_TF = 512


def _cparams(sem, vmem=VMEM_LIMIT):
    return pltpu.CompilerParams(dimension_semantics=sem, vmem_limit_bytes=vmem)


def _ada_kernel(c_ref, w_ref, b_ref, o_ref):
    c = c_ref[...]
    ca = c * jax.nn.sigmoid(c)
    o_ref[...] = jnp.dot(ca, w_ref[...], preferred_element_type=F32,
                         precision=HIGHEST) + b_ref[...]


def _ada_mod(c_pad, w_ada, b_ada):
    rows, d = c_pad.shape
    n = w_ada.shape[1]
    return pl.pallas_call(
        _ada_kernel,
        grid=(n // ADA_TN,),
        in_specs=[pl.BlockSpec((rows, d), lambda j: (0, 0)),
                  pl.BlockSpec((d, ADA_TN), lambda j: (0, j)),
                  pl.BlockSpec((1, ADA_TN), lambda j: (0, j))],
        out_specs=pl.BlockSpec((rows, ADA_TN), lambda j: (0, j)),
        out_shape=jax.ShapeDtypeStruct((rows, n), F32),
        compiler_params=_cparams(("arbitrary",)),
        name="ada_mod",
    )(c_pad, w_ada, b_ada)


def _inproj_kernel(x_ref, mod_ref, ng_ref, w_ref, cos_ref, se_ref, so_ref,
                   gaq_ref, gak_ref, gbq_ref, gbk_ref, o_ref):
    x = x_ref[...]
    ms = jnp.mean(x * x, axis=-1, keepdims=True)
    y = x * lax.rsqrt(ms + EPS) * ng_ref[...]
    h = y * (1.0 + mod_ref[0, 1:2, :]) + mod_ref[0, 0:1, :]
    hb = h.astype(BF16)

    cos = cos_ref[...]
    sin_even = se_ref[...]
    sin_odd = so_ref[...]
    lane = lax.broadcasted_iota(jnp.int32, (1, LANES), 1)
    low_half = lane < B_QK_DIM

    def rope(t):
        return (t * cos + pltpu.roll(t, LANES - 1, 1) * sin_even
                + pltpu.roll(t, 1, 1) * sin_odd)

    def norm_head(t, g):
        m = jnp.mean(t * t, axis=-1, keepdims=True)
        return t * lax.rsqrt(m + EPS) * g

    def norm_halves(t, g):
        sq = t * t
        s_lo = jnp.sum(jnp.where(low_half, sq, 0.0), axis=-1, keepdims=True)
        s_hi = jnp.sum(jnp.where(low_half, 0.0, sq), axis=-1, keepdims=True)
        m = jnp.where(low_half, s_lo, s_hi) * (1.0 / B_QK_DIM)
        return t * lax.rsqrt(m + EPS) * g

    a_scale = (HEAD_DIM ** -0.5) * LOG2E
    b_scale = (B_QK_DIM ** -0.5) * LOG2E
    for seg in range(IN_W // PROJ_SEG):
        acc = jnp.dot(hb, w_ref[:, seg * PROJ_SEG:(seg + 1) * PROJ_SEG],
                      preferred_element_type=F32)
        for j in range(PROJ_SEG // LANES):
            col = seg * PROJ_SEG + j * LANES
            t = acc[:, j * LANES:(j + 1) * LANES]
            if col < OFF_KA:
                t = rope(norm_head(t, gaq_ref[...])) * a_scale
            elif col < OFF_VA:
                t = rope(norm_head(t, gak_ref[...]))
            elif col < OFF_QB:
                pass
            elif col < OFF_KB:
                t = norm_halves(t, gbq_ref[...]) * b_scale
            elif col < OFF_VB:
                t = norm_halves(t, gbk_ref[...])
            o_ref[:, col:col + LANES] = t.astype(BF16)


def _in_proj(x2, mod3, norm_g, w_in_bf, cos_rep, sin_even, sin_odd, gaq, gak, gbq, gbk, seq):
    t, d = x2.shape
    tm = min(PROJ_TM, seq)
    per_b = seq // tm
    vec = lambda: pl.BlockSpec((1, LANES), lambda i: (0, 0))
    tab = lambda: pl.BlockSpec((tm, LANES), lambda i: (i % per_b, 0))
    return pl.pallas_call(
        _inproj_kernel,
        grid=(t // tm,),
        in_specs=[pl.BlockSpec((tm, d), lambda i: (i, 0)),
                  pl.BlockSpec((1, 6, d), lambda i: (i // per_b, 0, 0)),
                  pl.BlockSpec((1, d), lambda i: (0, 0)),
                  pl.BlockSpec((d, IN_W), lambda i: (0, 0), pipeline_mode=pl.Buffered(1)),
                  tab(), tab(), tab(), vec(), vec(), vec(), vec()],
        out_specs=pl.BlockSpec((tm, IN_W), lambda i: (i, 0)),
        out_shape=jax.ShapeDtypeStruct((t, IN_W), BF16),
        compiler_params=_cparams(("arbitrary",)),
        name="in_proj",
    )(x2, mod3, norm_g, w_in_bf, cos_rep, sin_even, sin_odd, gaq, gak, gbq, gbk)


BAND_CHUNKS = 5


def _bias_band_kernel(tab_ref, o_ref, *, tq):
    h = pl.program_id(0)
    width = BAND_CHUNKS * tq
    qq = lax.broadcasted_iota(jnp.int32, (tq, width), 0)
    kk = lax.broadcasted_iota(jnp.int32, (tq, width), 1)
    d = kk - (BAND_CHUNKS // 2) * tq - qq
    n = jnp.abs(d)
    n2 = n * n
    large = jnp.full_like(n, 8)
    for j in range(1, 8):
        large = large + (n2 >= 64 * 2 ** j).astype(jnp.int32)
    bucket = jnp.where(n < 8, n, large) + jnp.where(d > 0, 16, 0)
    acc = jnp.zeros((tq, width), F32)
    for b in range(REL_BUCKETS):
        acc = jnp.where(bucket == b, tab_ref[b, h], acc)
    o_ref[0] = acc * LOG2E


def _bias_band(rel_table, tq):
    return pl.pallas_call(
        functools.partial(_bias_band_kernel, tq=tq),
        grid=(B_HEADS,),
        in_specs=[pl.BlockSpec(memory_space=pltpu.SMEM)],
        out_specs=pl.BlockSpec((1, tq, BAND_CHUNKS * tq), lambda h: (h, 0, 0)),
        out_shape=jax.ShapeDtypeStruct((B_HEADS, tq, BAND_CHUNKS * tq), F32),
        compiler_params=_cparams(("arbitrary",)),
        name="bias_band",
    )(rel_table)


STAT_ROWS = 32
NORM_SLACK = 1.02


def _qk_stats_kernel(p_ref, o_ref):
    r = lax.broadcasted_iota(jnp.int32, (LANES, LANES), 0)
    c = lax.broadcasted_iota(jnp.int32, (LANES, LANES), 1)
    same_half = ((r < B_QK_DIM) == (c < B_QK_DIM)).astype(BF16)

    @pl.when(pl.program_id(1) == 0)
    def _():
        o_ref[...] = jnp.zeros_like(o_ref)

    for slab in range(OFF_VB // LANES):
        t = p_ref[:, slab * LANES:(slab + 1) * LANES].astype(F32)
        half_norm2 = jnp.dot((t * t).astype(BF16), same_half, preferred_element_type=F32)
        top = jnp.max(half_norm2, axis=0, keepdims=True)
        o_ref[0, slab:slab + 1, :] = jnp.maximum(o_ref[0, slab:slab + 1, :], top)


def _qk_stats(proj, batch, seq):
    tm = min(PROJ_TM, seq)
    per_b = seq // tm
    return pl.pallas_call(
        _qk_stats_kernel,
        grid=(batch, per_b),
        in_specs=[pl.BlockSpec((tm, IN_W), lambda b, j: (b * per_b + j, 0))],
        out_specs=pl.BlockSpec((1, STAT_ROWS, LANES), lambda b, j: (b, 0, 0)),
        out_shape=jax.ShapeDtypeStruct((batch, STAT_ROWS, LANES), F32),
        compiler_params=_cparams(("arbitrary", "arbitrary")),
        name="qk_stats",
    )(proj)


def _softmax_offsets(stats, rel_table):
    lo = stats[:, :, 0] * NORM_SLACK
    hi = stats[:, :, B_QK_DIM] * NORM_SLACK
    batch = stats.shape[0]
    s_ka, s_qb, s_kb, s_vb = (OFF_KA // LANES, OFF_QB // LANES, OFF_KB // LANES, OFF_VB // LANES)
    qa = jnp.max((lo + hi)[:, :s_ka].reshape(batch, A_KV_HEADS, A_GROUP), axis=-1)
    ka = (lo + hi)[:, s_ka:s_ka + A_KV_HEADS]
    off_a = jnp.sqrt(qa * ka)
    safe_a = 2.0 * off_a <= SAFE_SPAN
    bound1 = jnp.sqrt(lo[:, s_qb:s_kb] * lo[:, s_kb:s_vb])
    bound2 = jnp.sqrt(hi[:, s_qb:s_kb] * hi[:, s_kb:s_vb])
    bias_hi = jnp.max(rel_table, axis=0) * LOG2E
    bias_lo = jnp.min(rel_table, axis=0) * LOG2E
    off_b = jnp.stack([bound1 + bias_hi, bound2 + bias_hi], axis=-1)
    safe_b = 2.0 * jnp.maximum(bound1, bound2) + (bias_hi - bias_lo) <= SAFE_SPAN
    return (off_a.reshape(-1), safe_a.reshape(-1).astype(jnp.int32),
            off_b.reshape(-1), safe_b.reshape(-1).astype(jnp.int32))


def _flash_step(s, shift, v_c, m_ref, l_ref, acc_ref, idx):
    tk = s.shape[1]
    m_prev = m_ref[idx]
    m_cur = jnp.max(s, axis=1, keepdims=True) + shift
    m_new = jnp.maximum(m_prev, m_cur)
    alpha = jnp.exp2(m_prev - m_new)
    off = m_new - shift
    p = jnp.exp2(s - jnp.tile(off, (1, tk // LANES)))
    l_ref[idx] = alpha * l_ref[idx] + jnp.sum(p, axis=1, keepdims=True)
    acc_ref[idx] = alpha * acc_ref[idx] + jnp.dot(
        p.astype(BF16), v_c, preferred_element_type=F32)
    m_ref[idx] = m_new


_NT = (((1,), (1,)), ((), ()))
SAFE_SPAN = 100.0


def _extend_values(v_ref, vext_ref):
    vext_ref[:, :HEAD_DIM] = v_ref[...]
    vext_ref[:, HEAD_DIM:] = jnp.ones((v_ref.shape[0], HEAD_DIM), BF16)


def _fixed_offset_pass(q_rows, off, k_ref, vext_ref, accx_ref, tk, bias_of):
    offb = off
    accx_ref[...] = jnp.zeros(accx_ref.shape, F32)

    def chunk(c, carry):
        r0 = pl.multiple_of(c * tk, tk)
        s = lax.dot_general(q_rows, k_ref[pl.ds(r0, tk), :], _NT, preferred_element_type=F32)
        e = s - jnp.tile(offb, (1, tk // LANES))
        if bias_of is not None:
            e = e + bias_of(c)
        p = jnp.exp2(e).astype(BF16)
        accx_ref[...] += jnp.dot(p, vext_ref[pl.ds(r0, tk), :], preferred_element_type=F32)
        return carry

    lax.fori_loop(0, k_ref.shape[0] // tk, chunk, 0, unroll=True)


def _attn_a_kernel(off_ref, safe_ref, q_ref, k_ref, v_ref, o_ref, m_ref, l_ref, acc_ref,
                   vext_ref, accx_ref, *, tk):
    tq = q_ref.shape[0]
    seq = k_ref.shape[0]
    group = pl.program_id(0) * pl.num_programs(1) + pl.program_id(1)

    @pl.when(pl.program_id(2) == 0)
    def _():
        _extend_values(v_ref, vext_ref)

    q_rows = jnp.concatenate(
        [q_ref[:, g * HEAD_DIM:(g + 1) * HEAD_DIM] for g in range(A_GROUP)], axis=0)
    safe = safe_ref[group] == 1

    @pl.when(safe)
    def _():
        bound = jnp.full((A_GROUP * tq, LANES), off_ref[group], F32)
        _fixed_offset_pass(q_rows, bound, k_ref, vext_ref, accx_ref, tk, None)
        for g in range(A_GROUP):
            blk = accx_ref[g * tq:(g + 1) * tq, :]
            o_ref[:, g * HEAD_DIM:(g + 1) * HEAD_DIM] = (
                blk[:, :HEAD_DIM] / blk[:, HEAD_DIM:]).astype(BF16)

    @pl.when(jnp.logical_not(safe))
    def _():
        m_ref[...] = jnp.full(m_ref.shape, -jnp.inf, F32)
        l_ref[...] = jnp.zeros(l_ref.shape, F32)
        acc_ref[...] = jnp.zeros(acc_ref.shape, F32)

        def chunk(c, carry):
            r0 = pl.multiple_of(c * tk, tk)
            k_c = k_ref[pl.ds(r0, tk), :]
            v_c = v_ref[pl.ds(r0, tk), :]
            for g in range(A_GROUP):
                q = q_ref[:, g * HEAD_DIM:(g + 1) * HEAD_DIM]
                s = lax.dot_general(q, k_c, _NT, preferred_element_type=F32)
                _flash_step(s, 0.0, v_c, m_ref, l_ref, acc_ref, g)
            return carry

        lax.fori_loop(0, seq // tk, chunk, 0)
        for g in range(A_GROUP):
            o_ref[:, g * HEAD_DIM:(g + 1) * HEAD_DIM] = (acc_ref[g] / l_ref[g]).astype(BF16)


def _attn_a(off, safe, proj, batch, seq):
    t = proj.shape[0]
    tq = min(ATT_TQ, seq)
    tk = min(ATT_A_TK, seq)
    nq = seq // tq
    gw = A_GROUP * HEAD_DIM
    grid_spec = pltpu.PrefetchScalarGridSpec(
        num_scalar_prefetch=2,
        grid=(batch, A_KV_HEADS, nq),
        in_specs=[pl.BlockSpec((tq, gw), lambda b, g, i, *_: (b * nq + i, g)),
                  pl.BlockSpec((seq, HEAD_DIM), lambda b, g, i, *_: (b, OFF_KA // HEAD_DIM + g)),
                  pl.BlockSpec((seq, HEAD_DIM), lambda b, g, i, *_: (b, OFF_VA // HEAD_DIM + g))],
        out_specs=pl.BlockSpec((tq, gw), lambda b, g, i, *_: (b * nq + i, g)),
        scratch_shapes=[pltpu.VMEM((A_GROUP, tq, LANES), F32),
                        pltpu.VMEM((A_GROUP, tq, LANES), F32),
                        pltpu.VMEM((A_GROUP, tq, HEAD_DIM), F32),
                        pltpu.VMEM((seq, 2 * HEAD_DIM), BF16),
                        pltpu.VMEM((A_GROUP * tq, 2 * HEAD_DIM), F32)],
    )
    return pl.pallas_call(
        functools.partial(_attn_a_kernel, tk=tk),
        grid_spec=grid_spec,
        out_shape=jax.ShapeDtypeStruct((t, A_Q_W), BF16),
        compiler_params=_cparams(("arbitrary", "arbitrary", "arbitrary")),
        name="attn_a",
    )(off, safe, proj, proj, proj)


def _attn_b_kernel(tab_ref, off_ref, safe_ref, q_ref, k_ref, v_ref, band_ref, lq1_ref, lk1_ref,
                   lq2_ref, lk2_ref, sg_ref, o_ref, m_ref, l_ref, acc_ref, vext_ref, accx_ref):
    tq = q_ref.shape[0]
    seq = k_ref.shape[0]
    n_chunks = seq // tq
    mid = BAND_CHUNKS // 2
    h = pl.program_id(1)
    i = pl.program_id(2)
    head = pl.program_id(0) * pl.num_programs(1) + h
    lane = lax.broadcasted_iota(jnp.int32, (1, LANES), 1)
    low_half = lane < B_QK_DIM

    @pl.when(i == 0)
    def _():
        _extend_values(v_ref, vext_ref)

    q = q_ref[...]
    zero = jnp.zeros_like(q)
    q1 = jnp.where(low_half, q, zero)
    q2 = jnp.where(low_half, zero, q)
    safe = safe_ref[head] == 1

    lam1 = jnp.exp(jnp.sum(lq1_ref[...] * lk1_ref[...], axis=-1, keepdims=True))
    lam2 = jnp.exp(jnp.sum(lq2_ref[...] * lk2_ref[...], axis=-1, keepdims=True))
    lam = lam1 - lam2 + LAMBDA_INIT

    def finish(o1, o2):
        o = o1 - lam * o2
        ms = jnp.mean(o * o, axis=-1, keepdims=True)
        o = o * lax.rsqrt(ms + EPS) * sg_ref[...] * (1.0 - LAMBDA_INIT)
        o_ref[...] = o.astype(BF16)

    @pl.when(safe)
    def _():
        def bias_of(c):
            j = jnp.clip(c - i + mid, 0, BAND_CHUNKS - 1)
            bias = band_ref[0, :, pl.ds(pl.multiple_of(j * tq, tq), tq)]
            return jnp.concatenate([bias, bias], axis=0)

        off = jnp.concatenate([jnp.full((tq, LANES), off_ref[2 * head], F32),
                               jnp.full((tq, LANES), off_ref[2 * head + 1], F32)], axis=0)
        _fixed_offset_pass(jnp.concatenate([q1, q2], axis=0), off, k_ref,
                           vext_ref, accx_ref, tq, bias_of)
        a1 = accx_ref[0:tq, :]
        a2 = accx_ref[tq:2 * tq, :]
        finish(a1[:, :HEAD_DIM] / a1[:, HEAD_DIM:], a2[:, :HEAD_DIM] / a2[:, HEAD_DIM:])

    @pl.when(jnp.logical_not(safe))
    def _():
        m_ref[...] = jnp.full(m_ref.shape, -jnp.inf, F32)
        l_ref[...] = jnp.zeros(l_ref.shape, F32)
        acc_ref[...] = jnp.zeros(acc_ref.shape, F32)
        far_left = tab_ref[REL_BUCKETS // 2 - 1, h] * LOG2E
        far_right = tab_ref[REL_BUCKETS - 1, h] * LOG2E

        def step(c, shift, bias):
            r0 = pl.multiple_of(c * tq, tq)
            k_c = k_ref[pl.ds(r0, tq), :]
            v_c = v_ref[pl.ds(r0, tq), :]
            s1 = lax.dot_general(q1, k_c, _NT, preferred_element_type=F32)
            s2 = lax.dot_general(q2, k_c, _NT, preferred_element_type=F32)
            if bias is not None:
                s1 = s1 + bias
                s2 = s2 + bias
            _flash_step(s1, shift, v_c, m_ref, l_ref, acc_ref, 0)
            _flash_step(s2, shift, v_c, m_ref, l_ref, acc_ref, 1)

        def left(c, carry):
            step(c, far_left, None)
            return carry

        def right(c, carry):
            step(c, far_right, None)
            return carry

        lax.fori_loop(0, jnp.maximum(i - 1, 0), left, 0)
        for jj in range(3):
            c = i - 1 + jj

            @pl.when((c >= 0) & (c < n_chunks))
            def _():
                step(c, 0.0, band_ref[0, :, (mid - 1 + jj) * tq:(mid + jj) * tq])

        lax.fori_loop(jnp.minimum(i + 2, n_chunks), n_chunks, right, 0)
        finish(acc_ref[0] / l_ref[0], acc_ref[1] / l_ref[1])


def _attn_b(rel_table, off, safe, proj, band, lq1, lk1, lq2, lk2, subln, batch, seq):
    t = proj.shape[0]
    tq = band.shape[1]
    nq = seq // tq
    small = lambda w: pl.BlockSpec((1, w), lambda b, h, i, *_: (0, 0))
    grid_spec = pltpu.PrefetchScalarGridSpec(
        num_scalar_prefetch=3,
        grid=(batch, B_HEADS, nq),
        in_specs=[pl.BlockSpec((tq, HEAD_DIM), lambda b, h, i, *_: (b * nq + i, OFF_QB // HEAD_DIM + h)),
                  pl.BlockSpec((seq, HEAD_DIM), lambda b, h, i, *_: (b, OFF_KB // HEAD_DIM + h)),
                  pl.BlockSpec((seq, HEAD_DIM), lambda b, h, i, *_: (b, OFF_VB // HEAD_DIM + h)),
                  pl.BlockSpec((1, tq, BAND_CHUNKS * tq), lambda b, h, i, *_: (h, 0, 0)),
                  small(B_QK_DIM), small(B_QK_DIM), small(B_QK_DIM), small(B_QK_DIM),
                  small(HEAD_DIM)],
        out_specs=pl.BlockSpec((tq, HEAD_DIM), lambda b, h, i, *_: (b * nq + i, h)),
        scratch_shapes=[pltpu.VMEM((2, tq, LANES), F32),
                        pltpu.VMEM((2, tq, LANES), F32),
                        pltpu.VMEM((2, tq, HEAD_DIM), F32),
                        pltpu.VMEM((seq, 2 * HEAD_DIM), BF16),
                        pltpu.VMEM((2 * tq, 2 * HEAD_DIM), F32)],
    )
    return pl.pallas_call(
        _attn_b_kernel,
        grid_spec=grid_spec,
        out_shape=jax.ShapeDtypeStruct((t, B_V_W), BF16),
        compiler_params=_cparams(("arbitrary", "arbitrary", "arbitrary")),
        name="attn_b",
    )(rel_table, off, safe, proj, proj, proj, band, lq1, lk1, lq2, lk2, subln)


def _post_attn_kernel(oa_ref, ob_ref, x_ref, mod_ref, woa_ref, wob_ref, ng_ref, wr_ref,
                      br_ref, x1_ref, hp_ref, lg_ref):
    tm = x_ref.shape[0]
    d = x_ref.shape[1]
    for r in range(0, tm, POST_HALF):
        rows = slice(r, r + POST_HALF)
        mix = jnp.dot(oa_ref[rows, :], woa_ref[...], preferred_element_type=F32)
        mix = mix + jnp.dot(ob_ref[rows, :], wob_ref[...], preferred_element_type=F32)
        x1 = x_ref[rows, :] + mod_ref[0, 2:3, :] * mix
        x1_ref[rows, :] = x1
        ms = jnp.mean(x1 * x1, axis=-1, keepdims=True)
        y = x1 * lax.rsqrt(ms + EPS) * ng_ref[...]
        h2 = y * (1.0 + mod_ref[0, 4:5, :]) + mod_ref[0, 3:4, :]
        hp_ref[rows, :] = _pack_bf16_pairs(h2)
        hi = h2.astype(BF16)
        lo = (h2 - hi.astype(F32)).astype(BF16)
        lhs = jnp.concatenate([hi, lo, hi], axis=1)
        lg_ref[rows, :] = jnp.dot(lhs, wr_ref[...], preferred_element_type=F32) + br_ref[...]


def _post_attn(out_a, out_b, x2, mod3, w_out_bf, norm_g, w_router_pad, b_router_pad, seq):
    t, d = x2.shape
    tm = min(POST_TM, seq)
    per_b = seq // tm
    half = w_out_bf.shape[0] // 2
    return pl.pallas_call(
        _post_attn_kernel,
        grid=(t // tm,),
        in_specs=[pl.BlockSpec((tm, half), lambda i: (i, 0)),
                  pl.BlockSpec((tm, half), lambda i: (i, 0)),
                  pl.BlockSpec((tm, d), lambda i: (i, 0)),
                  pl.BlockSpec((1, 6, d), lambda i: (i // per_b, 0, 0)),
                  pl.BlockSpec((half, d), lambda i: (0, 0), pipeline_mode=pl.Buffered(1)),
                  pl.BlockSpec((half, d), lambda i: (1, 0), pipeline_mode=pl.Buffered(1)),
                  pl.BlockSpec((1, d), lambda i: (0, 0)),
                  pl.BlockSpec((3 * d, EXPERT_LANES), lambda i: (0, 0)),
                  pl.BlockSpec((1, EXPERT_LANES), lambda i: (0, 0))],
        out_specs=[pl.BlockSpec((tm, d), lambda i: (i, 0)),
                   pl.BlockSpec((tm, d // 2), lambda i: (i, 0)),
                   pl.BlockSpec((tm, EXPERT_LANES), lambda i: (i, 0))],
        out_shape=[jax.ShapeDtypeStruct((t, d), F32),
                   jax.ShapeDtypeStruct((t, d // 2), jnp.int32),
                   jax.ShapeDtypeStruct((t, EXPERT_LANES), F32)],
        compiler_params=_cparams(("arbitrary",)),
        name="post_attn",
    )(out_a, out_b, x2, mod3, w_out_bf, w_out_bf, norm_g, w_router_pad, b_router_pad)


def _route_kernel(lg_ref, pos_ref, gate_ref, cnt_ref, counts, start, carry):
    phase = pl.program_id(0)
    j = pl.program_id(1)
    tb = lg_ref.shape[0]
    lane_i = lax.broadcasted_iota(jnp.int32, (tb, EXPERT_LANES), 1)
    lane_f = lane_i.astype(F32)

    logit = lg_ref[...]
    vals, hots = [], []
    for _ in range(TOP_K):
        mk = jnp.max(logit, axis=1, keepdims=True)
        idx = jnp.min(jnp.where(logit == mk, lane_f, float(EXPERT_LANES)), axis=1, keepdims=True)
        hot = lane_f == idx
        logit = jnp.where(hot, -jnp.inf, logit)
        vals.append(mk)
        hots.append(hot)
    sel = jnp.zeros((tb, EXPERT_LANES), F32)
    for hot in hots:
        sel = sel + hot.astype(F32)
    col_sum = jnp.sum(sel, axis=0, keepdims=True)

    @pl.when((phase == 0) & (j == 0))
    def _():
        counts[...] = jnp.zeros_like(counts)

    @pl.when(phase == 0)
    def _():
        counts[...] += col_sum

    @pl.when((phase == 1) & (j == 0))
    def _():
        r = lax.broadcasted_iota(jnp.int32, (EXPERT_LANES, EXPERT_LANES), 0)
        c = lax.broadcasted_iota(jnp.int32, (EXPERT_LANES, EXPERT_LANES), 1)
        before = (r < c).astype(F32)
        padded = jnp.floor((counts[...] + (EXP_SUB - 1)) * (1.0 / EXP_SUB)) * EXP_SUB
        start[...] = jnp.dot(padded, before, preferred_element_type=F32, precision=HIGHEST)
        carry[...] = jnp.zeros_like(carry)
        cnt_ref[...] = counts[...].astype(jnp.int32)

    @pl.when(phase == 1)
    def _():
        r = lax.broadcasted_iota(jnp.int32, (tb, tb), 0)
        c = lax.broadcasted_iota(jnp.int32, (tb, tb), 1)
        earlier = (c < r).astype(BF16)
        prefix = jnp.dot(earlier, sel.astype(BF16), preferred_element_type=F32)
        base = prefix + carry[...] + start[...]
        exps = [jnp.exp(v - vals[0]) for v in vals]
        denom = exps[0] + exps[1] + exps[2] + exps[3]
        pos_out = jnp.zeros((tb, EXPERT_LANES), F32)
        gate_out = jnp.zeros((tb, EXPERT_LANES), F32)
        for k in range(TOP_K):
            pos_k = jnp.sum(jnp.where(hots[k], base, 0.0), axis=1, keepdims=True)
            pos_out = jnp.where(lane_i == k, pos_k, pos_out)
            gate_out = jnp.where(lane_i == k, exps[k] / denom, gate_out)
        pos_ref[...] = pos_out.astype(jnp.int32)
        gate_ref[...] = gate_out
        carry[...] += col_sum


def _route(logits):
    t = logits.shape[0]
    tb = min(ROUTE_TB, t)
    return pl.pallas_call(
        _route_kernel,
        grid=(2, t // tb),
        in_specs=[pl.BlockSpec((tb, EXPERT_LANES), lambda p, j: (j, 0))],
        out_specs=[pl.BlockSpec((tb, EXPERT_LANES), lambda p, j: (j * p, 0)),
                   pl.BlockSpec((tb, EXPERT_LANES), lambda p, j: (j * p, 0)),
                   pl.BlockSpec((1, EXPERT_LANES), lambda p, j: (0, 0))],
        out_shape=[jax.ShapeDtypeStruct((t, EXPERT_LANES), jnp.int32),
                   jax.ShapeDtypeStruct((t, EXPERT_LANES), F32),
                   jax.ShapeDtypeStruct((1, EXPERT_LANES), jnp.int32)],
        scratch_shapes=[pltpu.VMEM((1, EXPERT_LANES), F32),
                        pltpu.VMEM((1, EXPERT_LANES), F32),
                        pltpu.VMEM((1, EXPERT_LANES), F32)],
        compiler_params=_cparams(("arbitrary", "arbitrary")),
        name="route",
    )(logits)


def _row_copy(src_ref, src_row, dst_ref, dst_row, sem):
    return pltpu.make_async_copy(src_ref.at[pl.ds(src_row, 1), :],
                                 dst_ref.at[pl.ds(dst_row, 1), :], sem)


def _pack_bf16_pairs(x):
    half = x.shape[1] // 2
    bits = lax.bitcast_convert_type(x, jnp.int32)

    def rounded(b):
        lsb = lax.shift_right_logical(b, 16) & 1
        return b + 0x7FFF + lsb

    lo = lax.shift_right_logical(rounded(bits[:, :half]), 16)
    hi = rounded(bits[:, half:]) & jnp.int32(-65536)
    return lo | hi


def _unpack_bf16_pairs(w):
    lo = lax.bitcast_convert_type(lax.shift_left(w, 16), F32)
    hi = lax.bitcast_convert_type(w & jnp.int32(-65536), F32)
    return jnp.concatenate([lo, hi], axis=1)


def _dispatch_kernel(clr_ref, dst_ref, h_ref, xs_ref, zeros, csem, sem):
    tt = h_ref.shape[0]
    sub = zeros.shape[0]
    n_clear = clr_ref.shape[0]

    @pl.when(pl.program_id(0) == 0)
    def _():
        zeros[...] = jnp.zeros_like(zeros)

        def block_copy(blk):
            rows = pl.ds(pl.multiple_of(blk * sub, sub), sub)
            return pltpu.make_async_copy(zeros, xs_ref.at[rows, :], csem)

        def clear(b, carry):
            @pl.when(clr_ref[b] >= 0)
            def _():
                block_copy(clr_ref[b]).start()

            return carry

        def clear_done(b, carry):
            @pl.when(clr_ref[b] >= 0)
            def _():
                block_copy(0).wait()

            return carry

        lax.fori_loop(0, n_clear, clear, 0)
        lax.fori_loop(0, n_clear, clear_done, 0)

    def issue(r, carry):
        for k in range(TOP_K):
            _row_copy(h_ref, r, xs_ref, dst_ref[TOP_K * r + k], sem).start()
        return carry

    def drain(r, carry):
        for k in range(TOP_K):
            _row_copy(h_ref, 0, xs_ref, 0, sem).wait()
        return carry

    lax.fori_loop(0, tt, issue, 0)
    lax.fori_loop(0, tt, drain, 0, unroll=8)


def _dispatch(clear_blocks, pos_flat, hp, n_rows):
    t, dp = hp.shape
    tt = min(ROW_TT, t)
    grid_spec = pltpu.PrefetchScalarGridSpec(
        num_scalar_prefetch=1,
        grid=(t // tt,),
        in_specs=[pl.BlockSpec((tt * TOP_K,), lambda i, clr: (i,), memory_space=pltpu.SMEM),
                  pl.BlockSpec((tt, dp), lambda i, clr: (i, 0))],
        out_specs=pl.BlockSpec(memory_space=pl.ANY),
        scratch_shapes=[pltpu.VMEM((EXP_SUB, dp), jnp.int32), pltpu.SemaphoreType.DMA(()),
                        pltpu.SemaphoreType.DMA(())],
    )
    return pl.pallas_call(
        _dispatch_kernel,
        grid_spec=grid_spec,
        out_shape=jax.ShapeDtypeStruct((n_rows, dp), jnp.int32),
        compiler_params=_cparams(("arbitrary",)),
        name="dispatch",
    )(clear_blocks, pos_flat, hp)


def _clear_plan(counts, n_rows):
    n_e = counts.shape[0]
    n_blocks = n_rows // EXP_SUB
    pcb = (counts + EXP_SUB - 1) // EXP_SUB
    pend = jnp.cumsum(pcb)
    seg_last = jnp.where(pcb > 0, pend - 1, -1)
    tail = pend[-1] + jnp.arange(n_e + 1, dtype=jnp.int32)
    tail = jnp.where(tail < n_blocks, tail, -1)
    return jnp.concatenate([seg_last, tail]).astype(jnp.int32)


VISIT_IDLE, VISIT_COMPUTE, VISIT_CLEAR = 0, 1, 2


def _experts_kernel(exp_ref, row0_ref, nsub_ref, kind_ref, feff_ref,
                    xs_ref, wg_ref, wu_ref, wd_ref, bg_ref, bu_ref, bd_ref, ys_ref,
                    xbuf, acc, xsem, osem, *, sub, unit):
    v = pl.program_id(0)
    f = pl.program_id(1)
    last_f = pl.num_programs(1) - 1
    row0 = row0_ref[v]
    nsub = nsub_ref[v]
    kind = kind_ref[v]

    def x_copy(j, slot):
        rows = pl.ds(pl.multiple_of(row0 + j * unit, sub), unit)
        return pltpu.make_async_copy(xs_ref.at[rows, :], xbuf.at[slot], xsem.at[slot])

    def y_copy(j, src_row):
        rows = pl.ds(pl.multiple_of(row0 + j * unit, sub), unit)
        return pltpu.make_async_copy(acc.at[pl.ds(src_row, unit), :], ys_ref.at[rows, :], osem)

    def drain(j, carry):
        y_copy(0, 0).wait()
        return carry

    def x0_copy(visit):
        rows = pl.ds(pl.multiple_of(row0_ref[visit], sub), unit)
        return pltpu.make_async_copy(xs_ref.at[rows, :], xbuf.at[0], xsem.at[0])

    def run_visit(first, last):
        def compute(j, slot):
            xb = _unpack_bf16_pairs(xbuf[slot]).astype(BF16)
            g = jnp.dot(xb, wg_ref[0].astype(BF16), preferred_element_type=F32) + bg_ref[0]
            u = jnp.dot(xb, wu_ref[0].astype(BF16), preferred_element_type=F32) + bu_ref[0]
            g = jnp.minimum(g, SWIGLU_LIMIT)
            u = jnp.clip(u, -SWIGLU_LIMIT, SWIGLU_LIMIT)
            glu = g * jax.nn.sigmoid(SWIGLU_ALPHA * g)
            a = ((u + 1.0) * glu).astype(BF16)
            y = jnp.dot(a, wd_ref[0].astype(BF16), preferred_element_type=F32)
            r0 = pl.multiple_of(j * unit, unit)
            if first:
                acc[pl.ds(r0, unit), :] = y + bd_ref[0]
            else:
                acc[pl.ds(r0, unit), :] += y
            if last:
                y_copy(j, r0).start(priority=ROW_DMA_PRIORITY)

        if first:
            @pl.when(v == 0)
            def _():
                x0_copy(0).start(priority=ROW_DMA_PRIORITY)

        more = nsub > 1

        @pl.when(more)
        def _():
            x_copy(1, 1).start(priority=ROW_DMA_PRIORITY)

        if first:
            x0_copy(v).wait()
        compute(0, 0)
        if last:
            nxt = jnp.minimum(v + 1, pl.num_programs(0) - 1)

            @pl.when(kind_ref[nxt] == VISIT_COMPUTE)
            def _():
                x0_copy(nxt).start(priority=ROW_DMA_PRIORITY)

        @pl.when(more)
        def _():
            def later_unit(j, carry):
                slot = 1 + (j - 1) % 2
                x_copy(j, slot).wait()
                x_copy(jnp.minimum(j + 1, nsub - 1), 3 - slot).start(priority=ROW_DMA_PRIORITY)
                compute(j, slot)
                return carry

            lax.fori_loop(1, nsub, later_unit, 0)
            x_copy(0, 1 + (nsub - 1) % 2).wait()

        if last:
            lax.fori_loop(0, nsub, drain, 0)

    active = (kind == VISIT_COMPUTE) & (nsub > 0)
    pl.when(active & (f == 0))(lambda: run_visit(True, False))
    pl.when(active & (f > 0) & (f < last_f))(lambda: run_visit(False, False))
    pl.when(active & (f == last_f))(lambda: run_visit(False, True))

    @pl.when((kind == VISIT_CLEAR) & (f == 0) & (nsub > 0))
    def _():
        acc[pl.ds(0, sub), :] = jnp.zeros((sub, acc.shape[1]), F32)

        def clear_copy(j):
            rows = pl.ds(pl.multiple_of(row0 + j * sub, sub), sub)
            return pltpu.make_async_copy(acc.at[pl.ds(0, sub), :], ys_ref.at[rows, :], osem)

        def clear(j, carry):
            clear_copy(j).start()
            return carry

        def clear_done(j, carry):
            clear_copy(0).wait()
            return carry

        lax.fori_loop(0, nsub, clear, 0)
        lax.fori_loop(0, nsub, clear_done, 0)


def _experts(meta, xs, w_gu, b_gu3, w_down, b_down3):
    n_rows, dp = xs.shape
    n_e, d, two_f = w_gu.shape
    ff = two_f // 2
    tf = EXP_TF
    nf = ff // tf
    n_visits = meta[0].shape[0]
    wspec = lambda shape, imap: pl.BlockSpec(shape, imap)
    grid_spec = pltpu.PrefetchScalarGridSpec(
        num_scalar_prefetch=5,
        grid=(n_visits, nf),
        in_specs=[
            pl.BlockSpec(memory_space=pl.ANY),
            wspec((1, d, tf), lambda v, f, ex, r0, ns, kd, fe: (ex[v], 0, fe[v * nf + f])),
            wspec((1, d, tf), lambda v, f, ex, r0, ns, kd, fe: (ex[v], 0, nf + fe[v * nf + f])),
            wspec((1, tf, d), lambda v, f, ex, r0, ns, kd, fe: (ex[v], fe[v * nf + f], 0)),
            wspec((1, 1, tf), lambda v, f, ex, r0, ns, kd, fe: (ex[v], 0, fe[v * nf + f])),
            wspec((1, 1, tf), lambda v, f, ex, r0, ns, kd, fe: (ex[v], 0, nf + fe[v * nf + f])),
            wspec((1, 1, d), lambda v, f, ex, r0, ns, kd, fe: (ex[v], 0, 0)),
        ],
        out_specs=pl.BlockSpec(memory_space=pl.ANY),
        scratch_shapes=[pltpu.VMEM((3, EXP_UNIT, dp), jnp.int32),
                        pltpu.VMEM((EXP_MAX_UNITS * EXP_UNIT, d), F32),
                        pltpu.SemaphoreType.DMA((3,)),
                        pltpu.SemaphoreType.DMA(())],
    )
    return pl.pallas_call(
        functools.partial(_experts_kernel, sub=EXP_SUB, unit=EXP_UNIT),
        grid_spec=grid_spec,
        out_shape=jax.ShapeDtypeStruct((n_rows, d), F32),
        compiler_params=_cparams(("arbitrary", "arbitrary"), vmem=EXPERTS_VMEM_LIMIT),
        name="experts",
    )(*meta, xs, w_gu, w_gu, w_down, b_gu3, b_gu3, b_down3)


def _visit_plan(counts, n_rows, nf):
    n_e = counts.shape[0]
    sub = EXP_SUB
    tmx = EXP_MAX_UNITS * EXP_UNIT
    pc = (counts + sub - 1) // sub * sub
    pend = jnp.cumsum(pc)
    pstart = pend - pc
    nvis = (pc + tmx - 1) // tmx
    vend = jnp.cumsum(nvis)
    vstart = vend - nvis
    total = vend[-1]
    n_visits = n_rows // tmx + n_e + 1
    v = jnp.arange(n_visits, dtype=jnp.int32)
    compute = v < total
    v_clamped = jnp.minimum(v, total - 1)
    e_v = jnp.minimum(jnp.sum(vend[None, :] <= v_clamped[:, None], axis=1), n_e - 1).astype(jnp.int32)
    part = v - vstart[e_v]
    row0 = jnp.where(compute, pstart[e_v] + part * tmx, 0)
    nsub = jnp.where(compute, (jnp.minimum(pc[e_v] - part * tmx, tmx) + EXP_UNIT - 1) // EXP_UNIT, 0)
    clear = v == total
    row0 = jnp.where(clear, jnp.minimum(pend[-1], n_rows - sub), row0).astype(jnp.int32)
    nsub = jnp.where(clear, (n_rows - pend[-1]) // sub, nsub).astype(jnp.int32)
    kind = jnp.where(compute, VISIT_COMPUTE, jnp.where(clear, VISIT_CLEAR, VISIT_IDLE))
    f = jnp.arange(nf, dtype=jnp.int32)
    feff = jnp.where(compute[:, None], f[None, :], nf - 1).reshape(-1).astype(jnp.int32)
    return e_v, row0, nsub, kind.astype(jnp.int32), feff


def _combine_kernel(pos_ref, ys_ref, gate_ref, x1_ref, mod_ref, o_ref, rows, sem):
    tt = x1_ref.shape[0]

    def issue(r, carry):
        for k in range(TOP_K):
            pltpu.make_async_copy(ys_ref.at[pl.ds(pos_ref[TOP_K * r + k], 1), :],
                                  rows.at[k, pl.ds(r, 1), :], sem).start()
        return carry

    def drain(r, carry):
        for k in range(TOP_K):
            pltpu.make_async_copy(ys_ref.at[pl.ds(0, 1), :],
                                  rows.at[0, pl.ds(0, 1), :], sem).wait()
        return carry

    lax.fori_loop(0, tt, issue, 0)
    lax.fori_loop(0, tt, drain, 0, unroll=8)
    gate = gate_ref[...]
    y = gate[:, 0:1] * rows[0]
    for k in range(1, TOP_K):
        y = y + gate[:, k:k + 1] * rows[k]
    o_ref[...] = x1_ref[...] + mod_ref[0, 5:6, :] * y


def _combine(pos_flat, ys, gates, x1, mod3, seq):
    t, d = x1.shape
    tt = min(ROW_TT, seq)
    per_b = seq // tt
    return pl.pallas_call(
        _combine_kernel,
        grid=(t // tt,),
        in_specs=[pl.BlockSpec((tt * TOP_K,), lambda i: (i,), memory_space=pltpu.SMEM),
                  pl.BlockSpec(memory_space=pl.ANY),
                  pl.BlockSpec((tt, EXPERT_LANES), lambda i: (i, 0)),
                  pl.BlockSpec((tt, d), lambda i: (i, 0)),
                  pl.BlockSpec((1, 6, d), lambda i: (i // per_b, 0, 0))],
        out_specs=pl.BlockSpec((tt, d), lambda i: (i, 0)),
        out_shape=jax.ShapeDtypeStruct((t, d), F32),
        scratch_shapes=[pltpu.VMEM((TOP_K, tt, d), F32), pltpu.SemaphoreType.DMA(())],
        compiler_params=_cparams(("arbitrary",)),
        name="combine",
    )(pos_flat, ys, gates, x1, mod3)


def _rope_tables(seq):
    rows = seq // GRID_W
    row = jnp.repeat(jnp.arange(rows, dtype=F32), GRID_W)
    col = jnp.tile(jnp.arange(GRID_W, dtype=F32), rows)
    axis_dim = HEAD_DIM // 2
    inv = ROPE_THETA ** (-jnp.arange(0, axis_dim, 2, dtype=F32) / axis_dim)
    ang = jnp.concatenate([row[:, None] * inv, col[:, None] * inv], axis=-1)
    cos_rep = jnp.repeat(jnp.cos(ang), 2, axis=-1)
    sin_rep = jnp.repeat(jnp.sin(ang), 2, axis=-1)
    even = (jnp.arange(HEAD_DIM) % 2 == 0)[None, :]
    return cos_rep, jnp.where(even, -sin_rep, 0.0), jnp.where(even, 0.0, sin_rep)


def _layer(x, c, rel_table, w_ada, b_ada, norm_attn, norm_ffn, w_in, w_out, a_q_norm,
           a_k_norm, b_q_norm, b_k_norm, lq1, lk1, lq2, lk2, b_subln, w_router, b_router,
           w_gu, b_gu, w_down, b_down):
    batch, seq, d = x.shape
    t = batch * seq
    x2 = x.reshape(t, d)

    c_pad = jnp.pad(c, ((0, 8 - batch % 8 if batch % 8 else 0), (0, 0)))
    mod = _ada_mod(c_pad, w_ada, b_ada.reshape(1, -1))
    mod3 = mod[:batch].reshape(batch, 6, d)

    cos_rep, sin_even, sin_odd = _rope_tables(seq)
    row = lambda p: p.reshape(1, -1)
    proj = _in_proj(x2, mod3, row(norm_attn), w_in.astype(BF16), cos_rep, sin_even, sin_odd,
                    row(a_q_norm), row(a_k_norm), row(jnp.tile(b_q_norm, 2)),
                    row(jnp.tile(b_k_norm, 2)), seq)

    band = _bias_band(rel_table, min(ATT_B_TQ, seq))
    off_a, safe_a, off_b, safe_b = _softmax_offsets(_qk_stats(proj, batch, seq), rel_table)
    out_a = _attn_a(off_a, safe_a, proj, batch, seq)
    out_b = _attn_b(rel_table, off_b, safe_b, proj, band, row(lq1), row(lk1), row(lq2), row(lk2),
                    row(b_subln), batch, seq)

    n_e = w_router.shape[1]
    w_router_pad = jnp.pad(w_router, ((0, 0), (0, EXPERT_LANES - n_e)))
    wr_hi = w_router_pad.astype(BF16)
    wr_lo = (w_router_pad - wr_hi.astype(F32)).astype(BF16)
    wr_split = jnp.concatenate([wr_hi, wr_hi, wr_lo], axis=0)
    b_router_pad = jnp.pad(b_router, (0, EXPERT_LANES - n_e), constant_values=NEG_BIG)
    x1, h2, logits = _post_attn(out_a, out_b, x2, mod3, w_out.astype(BF16), row(norm_ffn),
                                wr_split, row(b_router_pad), seq)

    pos, gates, counts = _route(logits)
    pos_flat = pos[:, :TOP_K].reshape(-1)
    cnt = counts[0, :n_e]
    n_rows = t * TOP_K + (n_e + 1) * EXP_SUB
    xs = _dispatch(_clear_plan(cnt, n_rows), pos_flat, h2, n_rows)

    meta = _visit_plan(cnt, n_rows, (w_gu.shape[2] // 2) // EXP_TF)
    ys = _experts(meta, xs, w_gu, b_gu.reshape(n_e, 1, -1), w_down, b_down.reshape(n_e, 1, -1))

    out = _combine(pos_flat, ys, gates, x1, mod3, seq)
    return out.reshape(batch, seq, d)


def kernel(x, c, rel_bias_table, w_ada, b_ada, norm_attn, norm_ffn, w_in, w_out, a_q_norm,
           a_k_norm, b_q_norm, b_k_norm, lambda_q1, lambda_k1, lambda_q2, lambda_k2, b_subln,
           w_router, b_router, w_gu, b_gu, w_down, b_down):
    return _layer(x, c, rel_bias_table, w_ada[0], b_ada[0], norm_attn[0], norm_ffn[0],
                  w_in[0], w_out[0], a_q_norm[0], a_k_norm[0], b_q_norm[0], b_k_norm[0],
                  lambda_q1[0], lambda_k1[0], lambda_q2[0], lambda_k2[0], b_subln[0],
                  w_router[0], b_router[0], w_gu[0], b_gu[0], w_down[0], b_down[0])
```

```python
import functools
import math

import jax
import jax.numpy as jnp
from jax import lax
from jax.experimental import pallas as pl
from jax.experimental.pallas import tpu as pltpu

F32 = jnp.float32
BF16 = jnp.bfloat16
HIGHEST = lax.Precision.HIGHEST

D_MODEL = 2048
HEAD_DIM = 128
A_Q_HEADS = 8
A_KV_HEADS = 2
A_GROUP = A_Q_HEADS // A_KV_HEADS
ROPE_THETA = 10000.0
GRID_W = 64
B_HEADS = 8
B_QK_DIM = 64
REL_BUCKETS = 32
N_EXPERTS = 32
TOP_K = 4
D_FF = D_MODEL
SWIGLU_LIMIT = 7.0
SWIGLU_ALPHA = 1.702
EPS = 1e-6
LAMBDA_INIT = 0.8 - 0.6 * math.exp(-0.3 * 0)

A_Q_W = A_Q_HEADS * HEAD_DIM
A_KV_W = A_KV_HEADS * HEAD_DIM
B_QK_W = B_HEADS * 2 * B_QK_DIM
B_V_W = B_HEADS * HEAD_DIM
OFF_KA = A_Q_W
OFF_VA = OFF_KA + A_KV_W
OFF_QB = OFF_VA + A_KV_W
OFF_KB = OFF_QB + B_QK_W
OFF_VB = OFF_KB + B_QK_W
IN_W = OFF_VB + B_V_W

LOG2E = 1.4426950408889634
LANES = 128
EXPERT_LANES = LANES
NEG_BIG = -1e30

VMEM_LIMIT = 56 * 1024 * 1024

ADA_TN = 1536
PROJ_TM = 512
PROJ_SEG = 512
ATT_TQ = 256
ATT_B_TQ = 512
ATT_A_TK = 512
POST_TM = 512
POST_HALF = 256
ROUTE_TB = 512
ROW_TT = 256
EXP_SUB = 256
EXP_UNIT = 512
EXP_MAX_UNITS = 5
EXPERTS_VMEM_LIMIT = 60 * 1024 * 1024
ROW_DMA_PRIORITY = 1
EXP_TF = 512


def _cparams(sem, vmem=VMEM_LIMIT):
    return pltpu.CompilerParams(dimension_semantics=sem, vmem_limit_bytes=vmem)


def _ada_kernel(c_ref, w_ref, b_ref, o_ref):
    c = c_ref[...]
    ca = c * jax.nn.sigmoid(c)
    o_ref[...] = jnp.dot(ca, w_ref[...], preferred_element_type=F32,
                         precision=HIGHEST) + b_ref[...]


def _ada_mod(c_pad, w_ada, b_ada):
    rows, d = c_pad.shape
    n = w_ada.shape[1]
    return pl.pallas_call(
        _ada_kernel,
        grid=(n // ADA_TN,),
        in_specs=[pl.BlockSpec((rows, d), lambda j: (0, 0)),
                  pl.BlockSpec((d, ADA_TN), lambda j: (0, j)),
                  pl.BlockSpec((1, ADA_TN), lambda j: (0, j))],
        out_specs=pl.BlockSpec((rows, ADA_TN), lambda j: (0, j)),
        out_shape=jax.ShapeDtypeStruct((rows, n), F32),
        compiler_params=_cparams(("arbitrary",)),
        name="ada_mod",
    )(c_pad, w_ada, b_ada)


def _inproj_kernel(x_ref, mod_ref, ng_ref, w_ref, cos_ref, se_ref, so_ref,
                   gaq_ref, gak_ref, gbq_ref, gbk_ref, o_ref):
    x = x_ref[...]
    ms = jnp.mean(x * x, axis=-1, keepdims=True)
    y = x * lax.rsqrt(ms + EPS) * ng_ref[...]
    h = y * (1.0 + mod_ref[0, 1:2, :]) + mod_ref[0, 0:1, :]
    hb = h.astype(BF16)

    cos = cos_ref[...]
    sin_even = se_ref[...]
    sin_odd = so_ref[...]
    lane = lax.broadcasted_iota(jnp.int32, (1, LANES), 1)
    low_half = lane < B_QK_DIM

    def rope(t):
        return (t * cos + pltpu.roll(t, LANES - 1, 1) * sin_even
                + pltpu.roll(t, 1, 1) * sin_odd)

    def norm_head(t, g):
        m = jnp.mean(t * t, axis=-1, keepdims=True)
        return t * lax.rsqrt(m + EPS) * g

    def norm_halves(t, g):
        sq = t * t
        s_lo = jnp.sum(jnp.where(low_half, sq, 0.0), axis=-1, keepdims=True)
        s_hi = jnp.sum(jnp.where(low_half, 0.0, sq), axis=-1, keepdims=True)
        m = jnp.where(low_half, s_lo, s_hi) * (1.0 / B_QK_DIM)
        return t * lax.rsqrt(m + EPS) * g

    a_scale = (HEAD_DIM ** -0.5) * LOG2E
    b_scale = (B_QK_DIM ** -0.5) * LOG2E
    for seg in range(IN_W // PROJ_SEG):
        acc = jnp.dot(hb, w_ref[:, seg * PROJ_SEG:(seg + 1) * PROJ_SEG],
                      preferred_element_type=F32)
        for j in range(PROJ_SEG // LANES):
            col = seg * PROJ_SEG + j * LANES
            t = acc[:, j * LANES:(j + 1) * LANES]
            if col < OFF_KA:
                t = rope(norm_head(t, gaq_ref[...])) * a_scale
            elif col < OFF_VA:
                t = rope(norm_head(t, gak_ref[...]))
            elif col < OFF_QB:
                pass
            elif col < OFF_KB:
                t = norm_halves(t, gbq_ref[...]) * b_scale
            elif col < OFF_VB:
                t = norm_halves(t, gbk_ref[...])
            o_ref[:, col:col + LANES] = t.astype(BF16)


def _in_proj(x2, mod3, norm_g, w_in_bf, cos_rep, sin_even, sin_odd, gaq, gak, gbq, gbk, seq):
    t, d = x2.shape
    tm = min(PROJ_TM, seq)
    per_b = seq // tm
    vec = lambda: pl.BlockSpec((1, LANES), lambda i: (0, 0))
    tab = lambda: pl.BlockSpec((tm, LANES), lambda i: (i % per_b, 0))
    return pl.pallas_call(
        _inproj_kernel,
        grid=(t // tm,),
        in_specs=[pl.BlockSpec((tm, d), lambda i: (i, 0)),
                  pl.BlockSpec((1, 6, d), lambda i: (i // per_b, 0, 0)),
                  pl.BlockSpec((1, d), lambda i: (0, 0)),
                  pl.BlockSpec((d, IN_W), lambda i: (0, 0), pipeline_mode=pl.Buffered(1)),
                  tab(), tab(), tab(), vec(), vec(), vec(), vec()],
        out_specs=pl.BlockSpec((tm, IN_W), lambda i: (i, 0)),
        out_shape=jax.ShapeDtypeStruct((t, IN_W), BF16),
        compiler_params=_cparams(("arbitrary",)),
        name="in_proj",
    )(x2, mod3, norm_g, w_in_bf, cos_rep, sin_even, sin_odd, gaq, gak, gbq, gbk)


BAND_CHUNKS = 5


def _bias_band_kernel(tab_ref, o_ref, *, tq):
    h = pl.program_id(0)
    width = BAND_CHUNKS * tq
    base = min(LANES, tq)
    strip = width + tq - base
    qq = lax.broadcasted_iota(jnp.int32, (base, strip), 0)
    kk = lax.broadcasted_iota(jnp.int32, (base, strip), 1)
    d = kk - (tq - base) - (BAND_CHUNKS // 2) * tq - qq
    n = jnp.abs(d)
    n2 = n * n
    large = jnp.full_like(n, 8)
    for j in range(1, 8):
        large = large + (n2 >= 64 * 2 ** j).astype(jnp.int32)
    bucket = jnp.where(n < 8, n, large) + jnp.where(d > 0, 16, 0)
    acc = jnp.zeros((base, strip), F32)
    for b in range(REL_BUCKETS):
        acc = jnp.where(bucket == b, tab_ref[b, h], acc)
    acc = acc * LOG2E
    for k in range(tq // base):
        start = tq - base - base * k
        o_ref[0, k * base:(k + 1) * base, :] = acc[:, start:start + width]


def _bias_band(rel_table, tq):
    return pl.pallas_call(
        functools.partial(_bias_band_kernel, tq=tq),
        grid=(B_HEADS,),
        in_specs=[pl.BlockSpec(memory_space=pltpu.SMEM)],
        out_specs=pl.BlockSpec((1, tq, BAND_CHUNKS * tq), lambda h: (h, 0, 0)),
        out_shape=jax.ShapeDtypeStruct((B_HEADS, tq, BAND_CHUNKS * tq), F32),
        compiler_params=_cparams(("arbitrary",)),
        name="bias_band",
    )(rel_table)


STAT_ROWS = 32
NORM_SLACK = 1.02


def _qk_stats_kernel(p_ref, o_ref):
    r = lax.broadcasted_iota(jnp.int32, (LANES, LANES), 0)
    c = lax.broadcasted_iota(jnp.int32, (LANES, LANES), 1)
    same_half = ((r < B_QK_DIM) == (c < B_QK_DIM)).astype(BF16)

    @pl.when(pl.program_id(1) == 0)
    def _():
        o_ref[...] = jnp.zeros_like(o_ref)

    for slab in range(OFF_VB // LANES):
        t = p_ref[:, slab * LANES:(slab + 1) * LANES].astype(F32)
        half_norm2 = jnp.dot((t * t).astype(BF16), same_half, preferred_element_type=F32)
        top = jnp.max(half_norm2, axis=0, keepdims=True)
        o_ref[0, slab:slab + 1, :] = jnp.maximum(o_ref[0, slab:slab + 1, :], top)


def _qk_stats(proj, batch, seq):
    tm = min(PROJ_TM, seq)
    per_b = seq // tm
    return pl.pallas_call(
        _qk_stats_kernel,
        grid=(batch, per_b),
        in_specs=[pl.BlockSpec((tm, IN_W), lambda b, j: (b * per_b + j, 0))],
        out_specs=pl.BlockSpec((1, STAT_ROWS, LANES), lambda b, j: (b, 0, 0)),
        out_shape=jax.ShapeDtypeStruct((batch, STAT_ROWS, LANES), F32),
        compiler_params=_cparams(("arbitrary", "arbitrary")),
        name="qk_stats",
    )(proj)


def _softmax_offsets(stats, rel_table):
    lo = stats[:, :, 0] * NORM_SLACK
    hi = stats[:, :, B_QK_DIM] * NORM_SLACK
    batch = stats.shape[0]
    s_ka, s_qb, s_kb, s_vb = (OFF_KA // LANES, OFF_QB // LANES, OFF_KB // LANES, OFF_VB // LANES)
    qa = jnp.max((lo + hi)[:, :s_ka].reshape(batch, A_KV_HEADS, A_GROUP), axis=-1)
    ka = (lo + hi)[:, s_ka:s_ka + A_KV_HEADS]
    off_a = jnp.sqrt(qa * ka)
    safe_a = 2.0 * off_a <= SAFE_SPAN
    bound1 = jnp.sqrt(lo[:, s_qb:s_kb] * lo[:, s_kb:s_vb])
    bound2 = jnp.sqrt(hi[:, s_qb:s_kb] * hi[:, s_kb:s_vb])
    bias_hi = jnp.max(rel_table, axis=0) * LOG2E
    bias_lo = jnp.min(rel_table, axis=0) * LOG2E
    off_b = jnp.stack([bound1 + bias_hi, bound2 + bias_hi], axis=-1)
    safe_b = 2.0 * jnp.maximum(bound1, bound2) + (bias_hi - bias_lo) <= SAFE_SPAN
    return (off_a.reshape(-1), safe_a.reshape(-1).astype(jnp.int32),
            off_b.reshape(-1), safe_b.reshape(-1).astype(jnp.int32))


def _flash_step(s, shift, v_c, m_ref, l_ref, acc_ref, idx):
    tk = s.shape[1]
    m_prev = m_ref[idx]
    m_cur = jnp.max(s, axis=1, keepdims=True) + shift
    m_new = jnp.maximum(m_prev, m_cur)
    alpha = jnp.exp2(m_prev - m_new)
    off = m_new - shift
    p = jnp.exp2(s - jnp.tile(off, (1, tk // LANES)))
    l_ref[idx] = alpha * l_ref[idx] + jnp.sum(p, axis=1, keepdims=True)
    acc_ref[idx] = alpha * acc_ref[idx] + jnp.dot(
        p.astype(BF16), v_c, preferred_element_type=F32)
    m_ref[idx] = m_new


_NT = (((1,), (1,)), ((), ()))
SAFE_SPAN = 100.0


def _extend_values(v_ref, vext_ref):
    vext_ref[:, :HEAD_DIM] = v_ref[...]
    vext_ref[:, HEAD_DIM:] = jnp.ones((v_ref.shape[0], HEAD_DIM), BF16)


def _fixed_offset_pass(q_rows, off, k_ref, vext_ref, accx_ref, tk, bias_of):
    offb = off
    accx_ref[...] = jnp.zeros(accx_ref.shape, F32)

    def chunk(c, carry):
        r0 = pl.multiple_of(c * tk, tk)
        s = lax.dot_general(q_rows, k_ref[pl.ds(r0, tk), :], _NT, preferred_element_type=F32)
        e = s - jnp.tile(offb, (1, tk // LANES))
        if bias_of is not None:
            e = e + bias_of(c)
        p = jnp.exp2(e).astype(BF16)
        accx_ref[...] += jnp.dot(p, vext_ref[pl.ds(r0, tk), :], preferred_element_type=F32)
        return carry

    lax.fori_loop(0, k_ref.shape[0] // tk, chunk, 0, unroll=True)


def _attn_a_kernel(off_ref, safe_ref, q_ref, k_ref, v_ref, o_ref, m_ref, l_ref, acc_ref,
                   vext_ref, accx_ref, *, tk):
    tq = q_ref.shape[0]
    seq = k_ref.shape[0]
    group = pl.program_id(0) * pl.num_programs(1) + pl.program_id(1)

    @pl.when(pl.program_id(2) == 0)
    def _():
        _extend_values(v_ref, vext_ref)

    q_rows = jnp.concatenate(
        [q_ref[:, g * HEAD_DIM:(g + 1) * HEAD_DIM] for g in range(A_GROUP)], axis=0)
    safe = safe_ref[group] == 1

    @pl.when(safe)
    def _():
        bound = jnp.full((A_GROUP * tq, LANES), off_ref[group], F32)
        _fixed_offset_pass(q_rows, bound, k_ref, vext_ref, accx_ref, tk, None)
        for g in range(A_GROUP):
            blk = accx_ref[g * tq:(g + 1) * tq, :]
            o_ref[:, g * HEAD_DIM:(g + 1) * HEAD_DIM] = (
                blk[:, :HEAD_DIM] / blk[:, HEAD_DIM:]).astype(BF16)

    @pl.when(jnp.logical_not(safe))
    def _():
        m_ref[...] = jnp.full(m_ref.shape, -jnp.inf, F32)
        l_ref[...] = jnp.zeros(l_ref.shape, F32)
        acc_ref[...] = jnp.zeros(acc_ref.shape, F32)

        def chunk(c, carry):
            r0 = pl.multiple_of(c * tk, tk)
            k_c = k_ref[pl.ds(r0, tk), :]
            v_c = v_ref[pl.ds(r0, tk), :]
            for g in range(A_GROUP):
                q = q_ref[:, g * HEAD_DIM:(g + 1) * HEAD_DIM]
                s = lax.dot_general(q, k_c, _NT, preferred_element_type=F32)
                _flash_step(s, 0.0, v_c, m_ref, l_ref, acc_ref, g)
            return carry

        lax.fori_loop(0, seq // tk, chunk, 0)
        for g in range(A_GROUP):
            o_ref[:, g * HEAD_DIM:(g + 1) * HEAD_DIM] = (acc_ref[g] / l_ref[g]).astype(BF16)


def _attn_a(off, safe, proj, batch, seq):
    t = proj.shape[0]
    tq = min(ATT_TQ, seq)
    tk = min(ATT_A_TK, seq)
    nq = seq // tq
    gw = A_GROUP * HEAD_DIM
    grid_spec = pltpu.PrefetchScalarGridSpec(
        num_scalar_prefetch=2,
        grid=(batch, A_KV_HEADS, nq),
        in_specs=[pl.BlockSpec((tq, gw), lambda b, g, i, *_: (b * nq + i, g)),
                  pl.BlockSpec((seq, HEAD_DIM), lambda b, g, i, *_: (b, OFF_KA // HEAD_DIM + g)),
                  pl.BlockSpec((seq, HEAD_DIM), lambda b, g, i, *_: (b, OFF_VA // HEAD_DIM + g))],
        out_specs=pl.BlockSpec((tq, gw), lambda b, g, i, *_: (b * nq + i, g)),
        scratch_shapes=[pltpu.VMEM((A_GROUP, tq, LANES), F32),
                        pltpu.VMEM((A_GROUP, tq, LANES), F32),
                        pltpu.VMEM((A_GROUP, tq, HEAD_DIM), F32),
                        pltpu.VMEM((seq, 2 * HEAD_DIM), BF16),
                        pltpu.VMEM((A_GROUP * tq, 2 * HEAD_DIM), F32)],
    )
    return pl.pallas_call(
        functools.partial(_attn_a_kernel, tk=tk),
        grid_spec=grid_spec,
        out_shape=jax.ShapeDtypeStruct((t, A_Q_W), BF16),
        compiler_params=_cparams(("arbitrary", "arbitrary", "arbitrary")),
        name="attn_a",
    )(off, safe, proj, proj, proj)


def _attn_b_kernel(tab_ref, off_ref, safe_ref, q_ref, k_ref, v_ref, band_ref, lq1_ref, lk1_ref,
                   lq2_ref, lk2_ref, sg_ref, o_ref, m_ref, l_ref, acc_ref, vext_ref, accx_ref):
    tq = q_ref.shape[0]
    seq = k_ref.shape[0]
    n_chunks = seq // tq
    mid = BAND_CHUNKS // 2
    h = pl.program_id(1)
    i = pl.program_id(2)
    head = pl.program_id(0) * pl.num_programs(1) + h
    lane = lax.broadcasted_iota(jnp.int32, (1, LANES), 1)
    low_half = lane < B_QK_DIM

    @pl.when(i == 0)
    def _():
        _extend_values(v_ref, vext_ref)

    q = q_ref[...]
    zero = jnp.zeros_like(q)
    q1 = jnp.where(low_half, q, zero)
    q2 = jnp.where(low_half, zero, q)
    safe = safe_ref[head] == 1

    lam1 = jnp.exp(jnp.sum(lq1_ref[...] * lk1_ref[...], axis=-1, keepdims=True))
    lam2 = jnp.exp(jnp.sum(lq2_ref[...] * lk2_ref[...], axis=-1, keepdims=True))
    lam = lam1 - lam2 + LAMBDA_INIT

    def finish(o1, o2):
        o = o1 - lam * o2
        ms = jnp.mean(o * o, axis=-1, keepdims=True)
        o = o * lax.rsqrt(ms + EPS) * sg_ref[...] * (1.0 - LAMBDA_INIT)
        o_ref[...] = o.astype(BF16)

    @pl.when(safe)
    def _():
        def bias_of(c):
            j = jnp.clip(c - i + mid, 0, BAND_CHUNKS - 1)
            bias = band_ref[0, :, pl.ds(pl.multiple_of(j * tq, tq), tq)]
            return jnp.concatenate([bias, bias], axis=0)

        off = jnp.concatenate([jnp.full((tq, LANES), off_ref[2 * head], F32),
                               jnp.full((tq, LANES), off_ref[2 * head + 1], F32)], axis=0)
        _fixed_offset_pass(jnp.concatenate([q1, q2], axis=0), off, k_ref,
                           vext_ref, accx_ref, tq, bias_of)
        a1 = accx_ref[0:tq, :]
        a2 = accx_ref[tq:2 * tq, :]
        finish(a1[:, :HEAD_DIM] / a1[:, HEAD_DIM:], a2[:, :HEAD_DIM] / a2[:, HEAD_DIM:])

    @pl.when(jnp.logical_not(safe))
    def _():
        m_ref[...] = jnp.full(m_ref.shape, -jnp.inf, F32)
        l_ref[...] = jnp.zeros(l_ref.shape, F32)
        acc_ref[...] = jnp.zeros(acc_ref.shape, F32)
        far_left = tab_ref[REL_BUCKETS // 2 - 1, h] * LOG2E
        far_right = tab_ref[REL_BUCKETS - 1, h] * LOG2E

        def step(c, shift, bias):
            r0 = pl.multiple_of(c * tq, tq)
            k_c = k_ref[pl.ds(r0, tq), :]
            v_c = v_ref[pl.ds(r0, tq), :]
            s1 = lax.dot_general(q1, k_c, _NT, preferred_element_type=F32)
            s2 = lax.dot_general(q2, k_c, _NT, preferred_element_type=F32)
            if bias is not None:
                s1 = s1 + bias
                s2 = s2 + bias
            _flash_step(s1, shift, v_c, m_ref, l_ref, acc_ref, 0)
            _flash_step(s2, shift, v_c, m_ref, l_ref, acc_ref, 1)

        def left(c, carry):
            step(c, far_left, None)
            return carry

        def right(c, carry):
            step(c, far_right, None)
            return carry

        lax.fori_loop(0, jnp.maximum(i - 1, 0), left, 0)
        for jj in range(3):
            c = i - 1 + jj

            @pl.when((c >= 0) & (c < n_chunks))
            def _():
                step(c, 0.0, band_ref[0, :, (mid - 1 + jj) * tq:(mid + jj) * tq])

        lax.fori_loop(jnp.minimum(i + 2, n_chunks), n_chunks, right, 0)
        finish(acc_ref[0] / l_ref[0], acc_ref[1] / l_ref[1])


def _attn_b(rel_table, off, safe, proj, band, lq1, lk1, lq2, lk2, subln, batch, seq):
    t = proj.shape[0]
    tq = band.shape[1]
    nq = seq // tq
    small = lambda w: pl.BlockSpec((1, w), lambda b, h, i, *_: (0, 0))
    grid_spec = pltpu.PrefetchScalarGridSpec(
        num_scalar_prefetch=3,
        grid=(batch, B_HEADS, nq),
        in_specs=[pl.BlockSpec((tq, HEAD_DIM), lambda b, h, i, *_: (b * nq + i, OFF_QB // HEAD_DIM + h)),
                  pl.BlockSpec((seq, HEAD_DIM), lambda b, h, i, *_: (b, OFF_KB // HEAD_DIM + h)),
                  pl.BlockSpec((seq, HEAD_DIM), lambda b, h, i, *_: (b, OFF_VB // HEAD_DIM + h)),
                  pl.BlockSpec((1, tq, BAND_CHUNKS * tq), lambda b, h, i, *_: (h, 0, 0)),
                  small(B_QK_DIM), small(B_QK_DIM), small(B_QK_DIM), small(B_QK_DIM),
                  small(HEAD_DIM)],
        out_specs=pl.BlockSpec((tq, HEAD_DIM), lambda b, h, i, *_: (b * nq + i, h)),
        scratch_shapes=[pltpu.VMEM((2, tq, LANES), F32),
                        pltpu.VMEM((2, tq, LANES), F32),
                        pltpu.VMEM((2, tq, HEAD_DIM), F32),
                        pltpu.VMEM((seq, 2 * HEAD_DIM), BF16),
                        pltpu.VMEM((2 * tq, 2 * HEAD_DIM), F32)],
    )
    return pl.pallas_call(
        _attn_b_kernel,
        grid_spec=grid_spec,
        out_shape=jax.ShapeDtypeStruct((t, B_V_W), BF16),
        compiler_params=_cparams(("arbitrary", "arbitrary", "arbitrary")),
        name="attn_b",
    )(rel_table, off, safe, proj, proj, proj, band, lq1, lk1, lq2, lk2, subln)


def _post_attn_kernel(oa_ref, ob_ref, x_ref, mod_ref, woa_ref, wob_ref, ng_ref, wr_ref,
                      br_ref, x1_ref, hp_ref, lg_ref):
    tm = x_ref.shape[0]
    d = x_ref.shape[1]
    for r in range(0, tm, POST_HALF):
        rows = slice(r, r + POST_HALF)
        mix = jnp.dot(oa_ref[rows, :], woa_ref[...], preferred_element_type=F32)
        mix = mix + jnp.dot(ob_ref[rows, :], wob_ref[...], preferred_element_type=F32)
        x1 = x_ref[rows, :] + mod_ref[0, 2:3, :] * mix
        x1_ref[rows, :] = x1
        ms = jnp.mean(x1 * x1, axis=-1, keepdims=True)
        y = x1 * lax.rsqrt(ms + EPS) * ng_ref[...]
        h2 = y * (1.0 + mod_ref[0, 4:5, :]) + mod_ref[0, 3:4, :]
        hp_ref[rows, :] = _pack_bf16_pairs(h2)
        hi = h2.astype(BF16)
        lo = (h2 - hi.astype(F32)).astype(BF16)
        lhs = jnp.concatenate([hi, lo, hi], axis=1)
        lg_ref[rows, :] = jnp.dot(lhs, wr_ref[...], preferred_element_type=F32) + br_ref[...]


def _post_attn(out_a, out_b, x2, mod3, w_out_bf, norm_g, w_router_pad, b_router_pad, seq):
    t, d = x2.shape
    tm = min(POST_TM, seq)
    per_b = seq // tm
    half = w_out_bf.shape[0] // 2
    return pl.pallas_call(
        _post_attn_kernel,
        grid=(t // tm,),
        in_specs=[pl.BlockSpec((tm, half), lambda i: (i, 0)),
                  pl.BlockSpec((tm, half), lambda i: (i, 0)),
                  pl.BlockSpec((tm, d), lambda i: (i, 0)),
                  pl.BlockSpec((1, 6, d), lambda i: (i // per_b, 0, 0)),
                  pl.BlockSpec((half, d), lambda i: (0, 0), pipeline_mode=pl.Buffered(1)),
                  pl.BlockSpec((half, d), lambda i: (1, 0), pipeline_mode=pl.Buffered(1)),
                  pl.BlockSpec((1, d), lambda i: (0, 0)),
                  pl.BlockSpec((3 * d, EXPERT_LANES), lambda i: (0, 0)),
                  pl.BlockSpec((1, EXPERT_LANES), lambda i: (0, 0))],
        out_specs=[pl.BlockSpec((tm, d), lambda i: (i, 0)),
                   pl.BlockSpec((tm, d // 2), lambda i: (i, 0)),
                   pl.BlockSpec((tm, EXPERT_LANES), lambda i: (i, 0))],
        out_shape=[jax.ShapeDtypeStruct((t, d), F32),
                   jax.ShapeDtypeStruct((t, d // 2), jnp.int32),
                   jax.ShapeDtypeStruct((t, EXPERT_LANES), F32)],
        compiler_params=_cparams(("arbitrary",)),
        name="post_attn",
    )(out_a, out_b, x2, mod3, w_out_bf, w_out_bf, norm_g, w_router_pad, b_router_pad)


def _route_kernel(lg_ref, pos_ref, gate_ref, cnt_ref, counts, start, carry):
    phase = pl.program_id(0)
    j = pl.program_id(1)
    tb = lg_ref.shape[0]
    lane_i = lax.broadcasted_iota(jnp.int32, (tb, EXPERT_LANES), 1)
    lane_f = lane_i.astype(F32)

    logit = lg_ref[...]
    vals, hots = [], []
    for _ in range(TOP_K):
        mk = jnp.max(logit, axis=1, keepdims=True)
        idx = jnp.min(jnp.where(logit == mk, lane_f, float(EXPERT_LANES)), axis=1, keepdims=True)
        hot = lane_f == idx
        logit = jnp.where(hot, -jnp.inf, logit)
        vals.append(mk)
        hots.append(hot)
    sel = jnp.zeros((tb, EXPERT_LANES), F32)
    for hot in hots:
        sel = sel + hot.astype(F32)
    col_sum = jnp.sum(sel, axis=0, keepdims=True)

    @pl.when((phase == 0) & (j == 0))
    def _():
        counts[...] = jnp.zeros_like(counts)

    @pl.when(phase == 0)
    def _():
        counts[...] += col_sum

    @pl.when((phase == 1) & (j == 0))
    def _():
        r = lax.broadcasted_iota(jnp.int32, (EXPERT_LANES, EXPERT_LANES), 0)
        c = lax.broadcasted_iota(jnp.int32, (EXPERT_LANES, EXPERT_LANES), 1)
        before = (r < c).astype(F32)
        padded = jnp.floor((counts[...] + (EXP_SUB - 1)) * (1.0 / EXP_SUB)) * EXP_SUB
        start[...] = jnp.dot(padded, before, preferred_element_type=F32, precision=HIGHEST)
        carry[...] = jnp.zeros_like(carry)
        cnt_ref[...] = counts[...].astype(jnp.int32)

    @pl.when(phase == 1)
    def _():
        r = lax.broadcasted_iota(jnp.int32, (tb, tb), 0)
        c = lax.broadcasted_iota(jnp.int32, (tb, tb), 1)
        earlier = (c < r).astype(BF16)
        prefix = jnp.dot(earlier, sel.astype(BF16), preferred_element_type=F32)
        base = prefix + carry[...] + start[...]
        exps = [jnp.exp(v - vals[0]) for v in vals]
        denom = exps[0] + exps[1] + exps[2] + exps[3]
        pos_out = jnp.zeros((tb, EXPERT_LANES), F32)
        gate_out = jnp.zeros((tb, EXPERT_LANES), F32)
        for k in range(TOP_K):
            pos_k = jnp.sum(jnp.where(hots[k], base, 0.0), axis=1, keepdims=True)
            pos_out = jnp.where(lane_i == k, pos_k, pos_out)
            gate_out = jnp.where(lane_i == k, exps[k] / denom, gate_out)
        pos_ref[...] = pos_out.astype(jnp.int32)
        gate_ref[...] = gate_out
        carry[...] += col_sum


def _route(logits):
    t = logits.shape[0]
    tb = min(ROUTE_TB, t)
    return pl.pallas_call(
        _route_kernel,
        grid=(2, t // tb),
        in_specs=[pl.BlockSpec((tb, EXPERT_LANES), lambda p, j: (j, 0))],
        out_specs=[pl.BlockSpec((tb, EXPERT_LANES), lambda p, j: (j * p, 0)),
                   pl.BlockSpec((tb, EXPERT_LANES), lambda p, j: (j * p, 0)),
                   pl.BlockSpec((1, EXPERT_LANES), lambda p, j: (0, 0))],
        out_shape=[jax.ShapeDtypeStruct((t, EXPERT_LANES), jnp.int32),
                   jax.ShapeDtypeStruct((t, EXPERT_LANES), F32),
                   jax.ShapeDtypeStruct((1, EXPERT_LANES), jnp.int32)],
        scratch_shapes=[pltpu.VMEM((1, EXPERT_LANES), F32),
                        pltpu.VMEM((1, EXPERT_LANES), F32),
                        pltpu.VMEM((1, EXPERT_LANES), F32)],
        compiler_params=_cparams(("arbitrary", "arbitrary")),
        name="route",
    )(logits)


def _row_copy(src_ref, src_row, dst_ref, dst_row, sem):
    return pltpu.make_async_copy(src_ref.at[pl.ds(src_row, 1), :],
                                 dst_ref.at[pl.ds(dst_row, 1), :], sem)


def _pack_bf16_pairs(x):
    half = x.shape[1] // 2
    bits = lax.bitcast_convert_type(x, jnp.int32)

    def rounded(b):
        lsb = lax.shift_right_logical(b, 16) & 1
        return b + 0x7FFF + lsb

    lo = lax.shift_right_logical(rounded(bits[:, :half]), 16)
    hi = rounded(bits[:, half:]) & jnp.int32(-65536)
    return lo | hi


def _unpack_bf16_pairs(w):
    lo = lax.bitcast_convert_type(lax.shift_left(w, 16), F32)
    hi = lax.bitcast_convert_type(w & jnp.int32(-65536), F32)
    return jnp.concatenate([lo, hi], axis=1)


def _dispatch_kernel(clr_ref, dst_ref, h_ref, xs_ref, zeros, csem, sem):
    tt = h_ref.shape[0]
    sub = zeros.shape[0]
    n_clear = clr_ref.shape[0]

    @pl.when(pl.program_id(0) == 0)
    def _():
        zeros[...] = jnp.zeros_like(zeros)

        def block_copy(blk):
            rows = pl.ds(pl.multiple_of(blk * sub, sub), sub)
            return pltpu.make_async_copy(zeros, xs_ref.at[rows, :], csem)

        def clear(b, carry):
            @pl.when(clr_ref[b] >= 0)
            def _():
                block_copy(clr_ref[b]).start()

            return carry

        def clear_done(b, carry):
            @pl.when(clr_ref[b] >= 0)
            def _():
                block_copy(0).wait()

            return carry

        lax.fori_loop(0, n_clear, clear, 0)
        lax.fori_loop(0, n_clear, clear_done, 0)

    def issue(r, carry):
        for k in range(TOP_K):
            _row_copy(h_ref, r, xs_ref, dst_ref[TOP_K * r + k], sem).start()
        return carry

    def drain(r, carry):
        for k in range(TOP_K):
            _row_copy(h_ref, 0, xs_ref, 0, sem).wait()
        return carry

    lax.fori_loop(0, tt, issue, 0)
    lax.fori_loop(0, tt, drain, 0, unroll=8)


def _dispatch(clear_blocks, pos_flat, hp, n_rows):
    t, dp = hp.shape
    tt = min(ROW_TT, t)
    grid_spec = pltpu.PrefetchScalarGridSpec(
        num_scalar_prefetch=1,
        grid=(t // tt,),
        in_specs=[pl.BlockSpec((tt * TOP_K,), lambda i, clr: (i,), memory_space=pltpu.SMEM),
                  pl.BlockSpec((tt, dp), lambda i, clr: (i, 0))],
        out_specs=pl.BlockSpec(memory_space=pl.ANY),
        scratch_shapes=[pltpu.VMEM((EXP_SUB, dp), jnp.int32), pltpu.SemaphoreType.DMA(()),
                        pltpu.SemaphoreType.DMA(())],
    )
    return pl.pallas_call(
        _dispatch_kernel,
        grid_spec=grid_spec,
        out_shape=jax.ShapeDtypeStruct((n_rows, dp), jnp.int32),
        compiler_params=_cparams(("arbitrary",)),
        name="dispatch",
    )(clear_blocks, pos_flat, hp)


def _clear_plan(counts, n_rows):
    n_e = counts.shape[0]
    n_blocks = n_rows // EXP_SUB
    pcb = (counts + EXP_SUB - 1) // EXP_SUB
    pend = jnp.cumsum(pcb)
    seg_last = jnp.where(pcb > 0, pend - 1, -1)
    tail = pend[-1] + jnp.arange(n_e + 1, dtype=jnp.int32)
    tail = jnp.where(tail < n_blocks, tail, -1)
    return jnp.concatenate([seg_last, tail]).astype(jnp.int32)


VISIT_IDLE, VISIT_COMPUTE, VISIT_CLEAR = 0, 1, 2


def _experts_kernel(exp_ref, row0_ref, nsub_ref, kind_ref, feff_ref,
                    xs_ref, wg_ref, wu_ref, wd_ref, bg_ref, bu_ref, bd_ref, ys_ref,
                    xbuf, acc, xsem, osem, *, sub, unit):
    v = pl.program_id(0)
    f = pl.program_id(1)
    last_f = pl.num_programs(1) - 1
    row0 = row0_ref[v]
    nsub = nsub_ref[v]
    kind = kind_ref[v]

    def x_copy(j, slot):
        rows = pl.ds(pl.multiple_of(row0 + j * unit, sub), unit)
        return pltpu.make_async_copy(xs_ref.at[rows, :], xbuf.at[slot], xsem.at[slot])

    def y_copy(j, src_row):
        rows = pl.ds(pl.multiple_of(row0 + j * unit, sub), unit)
        return pltpu.make_async_copy(acc.at[pl.ds(src_row, unit), :], ys_ref.at[rows, :], osem)

    def drain(j, carry):
        y_copy(0, 0).wait()
        return carry

    def x0_copy(visit):
        rows = pl.ds(pl.multiple_of(row0_ref[visit], sub), unit)
        return pltpu.make_async_copy(xs_ref.at[rows, :], xbuf.at[0], xsem.at[0])

    def run_visit(first, last):
        def compute(j, slot):
            xb = _unpack_bf16_pairs(xbuf[slot]).astype(BF16)
            g = jnp.dot(xb, wg_ref[0].astype(BF16), preferred_element_type=F32) + bg_ref[0]
            u = jnp.dot(xb, wu_ref[0].astype(BF16), preferred_element_type=F32) + bu_ref[0]
            g = jnp.minimum(g, SWIGLU_LIMIT)
            u = jnp.clip(u, -SWIGLU_LIMIT, SWIGLU_LIMIT)
            glu = g * jax.nn.sigmoid(SWIGLU_ALPHA * g)
            a = ((u + 1.0) * glu).astype(BF16)
            y = jnp.dot(a, wd_ref[0].astype(BF16), preferred_element_type=F32)
            r0 = pl.multiple_of(j * unit, unit)
            if first:
                acc[pl.ds(r0, unit), :] = y + bd_ref[0]
            else:
                acc[pl.ds(r0, unit), :] += y
            if last:
                y_copy(j, r0).start(priority=ROW_DMA_PRIORITY)

        if first:
            @pl.when(v == 0)
            def _():
                x0_copy(0).start(priority=ROW_DMA_PRIORITY)

        more = nsub > 1

        @pl.when(more)
        def _():
            x_copy(1, 1).start(priority=ROW_DMA_PRIORITY)

        if first:
            x0_copy(v).wait()
        compute(0, 0)
        if last:
            nxt = jnp.minimum(v + 1, pl.num_programs(0) - 1)

            @pl.when(kind_ref[nxt] == VISIT_COMPUTE)
            def _():
                x0_copy(nxt).start(priority=ROW_DMA_PRIORITY)

        @pl.when(more)
        def _():
            def later_unit(j, carry):
                slot = 1 + (j - 1) % 2
                x_copy(j, slot).wait()
                x_copy(jnp.minimum(j + 1, nsub - 1), 3 - slot).start(priority=ROW_DMA_PRIORITY)
                compute(j, slot)
                return carry

            lax.fori_loop(1, nsub, later_unit, 0)
            x_copy(0, 1 + (nsub - 1) % 2).wait()

        if last:
            lax.fori_loop(0, nsub, drain, 0)

    active = (kind == VISIT_COMPUTE) & (nsub > 0)
    pl.when(active & (f == 0))(lambda: run_visit(True, False))
    pl.when(active & (f > 0) & (f < last_f))(lambda: run_visit(False, False))
    pl.when(active & (f == last_f))(lambda: run_visit(False, True))

    @pl.when((kind == VISIT_CLEAR) & (f == 0) & (nsub > 0))
    def _():
        acc[pl.ds(0, sub), :] = jnp.zeros((sub, acc.shape[1]), F32)

        def clear_copy(j):
            rows = pl.ds(pl.multiple_of(row0 + j * sub, sub), sub)
            return pltpu.make_async_copy(acc.at[pl.ds(0, sub), :], ys_ref.at[rows, :], osem)

        def clear(j, carry):
            clear_copy(j).start()
            return carry

        def clear_done(j, carry):
            clear_copy(0).wait()
            return carry

        lax.fori_loop(0, nsub, clear, 0)
        lax.fori_loop(0, nsub, clear_done, 0)


def _experts(meta, xs, w_gu, b_gu3, w_down, b_down3):
    n_rows, dp = xs.shape
    n_e, d, two_f = w_gu.shape
    ff = two_f // 2
    tf = EXP_TF
    nf = ff // tf
    n_visits = meta[0].shape[0]
    wspec = lambda shape, imap: pl.BlockSpec(shape, imap)
    grid_spec = pltpu.PrefetchScalarGridSpec(
        num_scalar_prefetch=5,
        grid=(n_visits, nf),
        in_specs=[
            pl.BlockSpec(memory_space=pl.ANY),
            wspec((1, d, tf), lambda v, f, ex, r0, ns, kd, fe: (ex[v], 0, fe[v * nf + f])),
            wspec((1, d, tf), lambda v, f, ex, r0, ns, kd, fe: (ex[v], 0, nf + fe[v * nf + f])),
            wspec((1, tf, d), lambda v, f, ex, r0, ns, kd, fe: (ex[v], fe[v * nf + f], 0)),
            wspec((1, 1, tf), lambda v, f, ex, r0, ns, kd, fe: (ex[v], 0, fe[v * nf + f])),
            wspec((1, 1, tf), lambda v, f, ex, r0, ns, kd, fe: (ex[v], 0, nf + fe[v * nf + f])),
            wspec((1, 1, d), lambda v, f, ex, r0, ns, kd, fe: (ex[v], 0, 0)),
        ],
        out_specs=pl.BlockSpec(memory_space=pl.ANY),
        scratch_shapes=[pltpu.VMEM((3, EXP_UNIT, dp), jnp.int32),
                        pltpu.VMEM((EXP_MAX_UNITS * EXP_UNIT, d), F32),
                        pltpu.SemaphoreType.DMA((3,)),
                        pltpu.SemaphoreType.DMA(())],
    )
    return pl.pallas_call(
        functools.partial(_experts_kernel, sub=EXP_SUB, unit=EXP_UNIT),
        grid_spec=grid_spec,
        out_shape=jax.ShapeDtypeStruct((n_rows, d), F32),
        compiler_params=_cparams(("arbitrary", "arbitrary"), vmem=EXPERTS_VMEM_LIMIT),
        name="experts",
    )(*meta, xs, w_gu, w_gu, w_down, b_gu3, b_gu3, b_down3)


def _visit_plan(counts, n_rows, nf):
    n_e = counts.shape[0]
    sub = EXP_SUB
    tmx = EXP_MAX_UNITS * EXP_UNIT
    pc = (counts + sub - 1) // sub * sub
    pend = jnp.cumsum(pc)
    pstart = pend - pc
    nvis = (pc + tmx - 1) // tmx
    vend = jnp.cumsum(nvis)
    vstart = vend - nvis
    total = vend[-1]
    n_visits = n_rows // tmx + n_e + 1
    v = jnp.arange(n_visits, dtype=jnp.int32)
    compute = v < total
    v_clamped = jnp.minimum(v, total - 1)
    e_v = jnp.minimum(jnp.sum(vend[None, :] <= v_clamped[:, None], axis=1), n_e - 1).astype(jnp.int32)
    part = v - vstart[e_v]
    row0 = jnp.where(compute, pstart[e_v] + part * tmx, 0)
    nsub = jnp.where(compute, (jnp.minimum(pc[e_v] - part * tmx, tmx) + EXP_UNIT - 1) // EXP_UNIT, 0)
    clear = v == total
    row0 = jnp.where(clear, jnp.minimum(pend[-1], n_rows - sub), row0).astype(jnp.int32)
    nsub = jnp.where(clear, (n_rows - pend[-1]) // sub, nsub).astype(jnp.int32)
    kind = jnp.where(compute, VISIT_COMPUTE, jnp.where(clear, VISIT_CLEAR, VISIT_IDLE))
    f = jnp.arange(nf, dtype=jnp.int32)
    feff = jnp.where(compute[:, None], f[None, :], nf - 1).reshape(-1).astype(jnp.int32)
    return e_v, row0, nsub, kind.astype(jnp.int32), feff


def _combine_kernel(pos_ref, ys_ref, gate_ref, x1_ref, mod_ref, o_ref, rows, sem):
    tt = x1_ref.shape[0]

    def issue(r, carry):
        for k in range(TOP_K):
            pltpu.make_async_copy(ys_ref.at[pl.ds(pos_ref[TOP_K * r + k], 1), :],
                                  rows.at[k, pl.ds(r, 1), :], sem).start()
        return carry

    def drain(r, carry):
        for k in range(TOP_K):
            pltpu.make_async_copy(ys_ref.at[pl.ds(0, 1), :],
                                  rows.at[0, pl.ds(0, 1), :], sem).wait()
        return carry

    lax.fori_loop(0, tt, issue, 0)
    lax.fori_loop(0, tt, drain, 0, unroll=8)
    gate = gate_ref[...]
    y = gate[:, 0:1] * rows[0]
    for k in range(1, TOP_K):
        y = y + gate[:, k:k + 1] * rows[k]
    o_ref[...] = x1_ref[...] + mod_ref[0, 5:6, :] * y


def _combine(pos_flat, ys, gates, x1, mod3, seq):
    t, d = x1.shape
    tt = min(ROW_TT, seq)
    per_b = seq // tt
    return pl.pallas_call(
        _combine_kernel,
        grid=(t // tt,),
        in_specs=[pl.BlockSpec((tt * TOP_K,), lambda i: (i,), memory_space=pltpu.SMEM),
                  pl.BlockSpec(memory_space=pl.ANY),
                  pl.BlockSpec((tt, EXPERT_LANES), lambda i: (i, 0)),
                  pl.BlockSpec((tt, d), lambda i: (i, 0)),
                  pl.BlockSpec((1, 6, d), lambda i: (i // per_b, 0, 0))],
        out_specs=pl.BlockSpec((tt, d), lambda i: (i, 0)),
        out_shape=jax.ShapeDtypeStruct((t, d), F32),
        scratch_shapes=[pltpu.VMEM((TOP_K, tt, d), F32), pltpu.SemaphoreType.DMA(())],
        compiler_params=_cparams(("arbitrary",)),
        name="combine",
    )(pos_flat, ys, gates, x1, mod3)


def _rope_tables(seq):
    rows = seq // GRID_W
    row = jnp.repeat(jnp.arange(rows, dtype=F32), GRID_W)
    col = jnp.tile(jnp.arange(GRID_W, dtype=F32), rows)
    axis_dim = HEAD_DIM // 2
    inv = ROPE_THETA ** (-jnp.arange(0, axis_dim, 2, dtype=F32) / axis_dim)
    ang = jnp.concatenate([row[:, None] * inv, col[:, None] * inv], axis=-1)
    cos_rep = jnp.repeat(jnp.cos(ang), 2, axis=-1)
    sin_rep = jnp.repeat(jnp.sin(ang), 2, axis=-1)
    even = (jnp.arange(HEAD_DIM) % 2 == 0)[None, :]
    return cos_rep, jnp.where(even, -sin_rep, 0.0), jnp.where(even, 0.0, sin_rep)


def _layer(x, c, rel_table, w_ada, b_ada, norm_attn, norm_ffn, w_in, w_out, a_q_norm,
           a_k_norm, b_q_norm, b_k_norm, lq1, lk1, lq2, lk2, b_subln, w_router, b_router,
           w_gu, b_gu, w_down, b_down):
    batch, seq, d = x.shape
    t = batch * seq
    x2 = x.reshape(t, d)

    c_pad = jnp.pad(c, ((0, 8 - batch % 8 if batch % 8 else 0), (0, 0)))
    mod = _ada_mod(c_pad, w_ada, b_ada.reshape(1, -1))
    mod3 = mod[:batch].reshape(batch, 6, d)

    cos_rep, sin_even, sin_odd = _rope_tables(seq)
    row = lambda p: p.reshape(1, -1)
    proj = _in_proj(x2, mod3, row(norm_attn), w_in.astype(BF16), cos_rep, sin_even, sin_odd,
                    row(a_q_norm), row(a_k_norm), row(jnp.tile(b_q_norm, 2)),
                    row(jnp.tile(b_k_norm, 2)), seq)

    band = _bias_band(rel_table, min(ATT_B_TQ, seq))
    off_a, safe_a, off_b, safe_b = _softmax_offsets(_qk_stats(proj, batch, seq), rel_table)
    out_a = _attn_a(off_a, safe_a, proj, batch, seq)
    out_b = _attn_b(rel_table, off_b, safe_b, proj, band, row(lq1), row(lk1), row(lq2), row(lk2),
                    row(b_subln), batch, seq)

    n_e = w_router.shape[1]
    w_router_pad = jnp.pad(w_router, ((0, 0), (0, EXPERT_LANES - n_e)))
    wr_hi = w_router_pad.astype(BF16)
    wr_lo = (w_router_pad - wr_hi.astype(F32)).astype(BF16)
    wr_split = jnp.concatenate([wr_hi, wr_hi, wr_lo], axis=0)
    b_router_pad = jnp.pad(b_router, (0, EXPERT_LANES - n_e), constant_values=NEG_BIG)
    x1, h2, logits = _post_attn(out_a, out_b, x2, mod3, w_out.astype(BF16), row(norm_ffn),
                                wr_split, row(b_router_pad), seq)

    pos, gates, counts = _route(logits)
    pos_flat = pos[:, :TOP_K].reshape(-1)
    cnt = counts[0, :n_e]
    n_rows = t * TOP_K + (n_e + 1) * EXP_SUB
    xs = _dispatch(_clear_plan(cnt, n_rows), pos_flat, h2, n_rows)

    meta = _visit_plan(cnt, n_rows, (w_gu.shape[2] // 2) // EXP_TF)
    ys = _experts(meta, xs, w_gu, b_gu.reshape(n_e, 1, -1), w_down, b_down.reshape(n_e, 1, -1))

    out = _combine(pos_flat, ys, gates, x1, mod3, seq)
    return out.reshape(batch, seq, d)


def kernel(x, c, rel_bias_table, w_ada, b_ada, norm_attn, norm_ffn, w_in, w_out, a_q_norm,
           a_k_norm, b_q_norm, b_k_norm, lambda_q1, lambda_k1, lambda_q2, lambda_k2, b_subln,
           w_router, b_router, w_gu, b_gu, w_down, b_down):
    return _layer(x, c, rel_bias_table, w_ada[0], b_ada[0], norm_attn[0], norm_ffn[0],
                  w_in[0], w_out[0], a_q_norm[0], a_k_norm[0], b_q_norm[0], b_k_norm[0],
                  lambda_q1[0], lambda_k1[0], lambda_q2[0], lambda_k2[0], b_subln[0],
                  w_router[0], b_router[0], w_gu[0], b_gu[0], w_down[0], b_down[0])
```

```python
import functools
import math

import jax
import jax.numpy as jnp
from jax import lax
from jax.experimental import pallas as pl
from jax.experimental.pallas import tpu as pltpu

F32 = jnp.float32
BF16 = jnp.bfloat16
HIGHEST = lax.Precision.HIGHEST

D_MODEL = 2048
HEAD_DIM = 128
A_Q_HEADS = 8
A_KV_HEADS = 2
A_GROUP = A_Q_HEADS // A_KV_HEADS
ROPE_THETA = 10000.0
GRID_W = 64
B_HEADS = 8
B_QK_DIM = 64
REL_BUCKETS = 32
N_EXPERTS = 32
TOP_K = 4
D_FF = D_MODEL
SWIGLU_LIMIT = 7.0
SWIGLU_ALPHA = 1.702
EPS = 1e-6
LAMBDA_INIT = 0.8 - 0.6 * math.exp(-0.3 * 0)

A_Q_W = A_Q_HEADS * HEAD_DIM
A_KV_W = A_KV_HEADS * HEAD_DIM
B_QK_W = B_HEADS * 2 * B_QK_DIM
B_V_W = B_HEADS * HEAD_DIM
OFF_KA = A_Q_W
OFF_VA = OFF_KA + A_KV_W
OFF_QB = OFF_VA + A_KV_W
OFF_KB = OFF_QB + B_QK_W
OFF_VB = OFF_KB + B_QK_W
IN_W = OFF_VB + B_V_W

LOG2E = 1.4426950408889634
LANES = 128
EXPERT_LANES = LANES
NEG_BIG = -1e30

VMEM_LIMIT = 56 * 1024 * 1024

ADA_TN = 1536
PROJ_TM = 512
PROJ_SEG = 512
ATT_TQ = 512
ATT_B_TQ = 512
ATT_A_TK = 512
POST_TM = 512
POST_HALF = 256
ROUTE_TB = 512
ROW_TT = 512
EXP_SUB = 256
EXP_UNIT = 512
EXP_MAX_UNITS = 5
EXPERTS_VMEM_LIMIT = 60 * 1024 * 1024
ROW_DMA_PRIORITY = 1
EXP_TF = 512


def _cparams(sem, vmem=VMEM_LIMIT):
    return pltpu.CompilerParams(dimension_semantics=sem, vmem_limit_bytes=vmem)


def _ada_kernel(c_ref, w_ref, b_ref, o_ref):
    c = c_ref[...]
    ca = c * jax.nn.sigmoid(c)
    o_ref[...] = jnp.dot(ca, w_ref[...], preferred_element_type=F32,
                         precision=HIGHEST) + b_ref[...]


def _ada_mod(c_pad, w_ada, b_ada):
    rows, d = c_pad.shape
    n = w_ada.shape[1]
    return pl.pallas_call(
        _ada_kernel,
        grid=(n // ADA_TN,),
        in_specs=[pl.BlockSpec((rows, d), lambda j: (0, 0)),
                  pl.BlockSpec((d, ADA_TN), lambda j: (0, j)),
                  pl.BlockSpec((1, ADA_TN), lambda j: (0, j))],
        out_specs=pl.BlockSpec((rows, ADA_TN), lambda j: (0, j)),
        out_shape=jax.ShapeDtypeStruct((rows, n), F32),
        compiler_params=_cparams(("arbitrary",)),
        name="ada_mod",
    )(c_pad, w_ada, b_ada)


def _inproj_kernel(x_ref, mod_ref, ng_ref, w_ref, cos_ref, se_ref, so_ref,
                   gaq_ref, gak_ref, gbq_ref, gbk_ref, o_ref):
    x = x_ref[...]
    ms = jnp.mean(x * x, axis=-1, keepdims=True)
    y = x * lax.rsqrt(ms + EPS) * ng_ref[...]
    h = y * (1.0 + mod_ref[0, 1:2, :]) + mod_ref[0, 0:1, :]
    hb = h.astype(BF16)

    cos = cos_ref[...]
    sin_even = se_ref[...]
    sin_odd = so_ref[...]
    lane = lax.broadcasted_iota(jnp.int32, (1, LANES), 1)
    low_half = lane < B_QK_DIM

    def rope(t):
        return (t * cos + pltpu.roll(t, LANES - 1, 1) * sin_even
                + pltpu.roll(t, 1, 1) * sin_odd)

    def norm_head(t, g):
        m = jnp.mean(t * t, axis=-1, keepdims=True)
        return t * lax.rsqrt(m + EPS) * g

    def norm_halves(t, g):
        sq = t * t
        s_lo = jnp.sum(jnp.where(low_half, sq, 0.0), axis=-1, keepdims=True)
        s_hi = jnp.sum(jnp.where(low_half, 0.0, sq), axis=-1, keepdims=True)
        m = jnp.where(low_half, s_lo, s_hi) * (1.0 / B_QK_DIM)
        return t * lax.rsqrt(m + EPS) * g

    a_scale = (HEAD_DIM ** -0.5) * LOG2E
    b_scale = (B_QK_DIM ** -0.5) * LOG2E
    for seg in range(IN_W // PROJ_SEG):
        acc = jnp.dot(hb, w_ref[:, seg * PROJ_SEG:(seg + 1) * PROJ_SEG],
                      preferred_element_type=F32)
        for j in range(PROJ_SEG // LANES):
            col = seg * PROJ_SEG + j * LANES
            t = acc[:, j * LANES:(j + 1) * LANES]
            if col < OFF_KA:
                t = rope(norm_head(t, gaq_ref[...])) * a_scale
            elif col < OFF_VA:
                t = rope(norm_head(t, gak_ref[...]))
            elif col < OFF_QB:
                pass
            elif col < OFF_KB:
                t = norm_halves(t, gbq_ref[...]) * b_scale
            elif col < OFF_VB:
                t = norm_halves(t, gbk_ref[...])
            o_ref[:, col:col + LANES] = t.astype(BF16)


def _in_proj(x2, mod3, norm_g, w_in_bf, cos_rep, sin_even, sin_odd, gaq, gak, gbq, gbk, seq):
    t, d = x2.shape
    tm = min(PROJ_TM, seq)
    per_b = seq // tm
    vec = lambda: pl.BlockSpec((1, LANES), lambda i: (0, 0))
    tab = lambda: pl.BlockSpec((tm, LANES), lambda i: (i % per_b, 0))
    return pl.pallas_call(
        _inproj_kernel,
        grid=(t // tm,),
        in_specs=[pl.BlockSpec((tm, d), lambda i: (i, 0)),
                  pl.BlockSpec((1, 6, d), lambda i: (i // per_b, 0, 0)),
                  pl.BlockSpec((1, d), lambda i: (0, 0)),
                  pl.BlockSpec((d, IN_W), lambda i: (0, 0), pipeline_mode=pl.Buffered(1)),
                  tab(), tab(), tab(), vec(), vec(), vec(), vec()],
        out_specs=pl.BlockSpec((tm, IN_W), lambda i: (i, 0)),
        out_shape=jax.ShapeDtypeStruct((t, IN_W), BF16),
        compiler_params=_cparams(("arbitrary",)),
        name="in_proj",
    )(x2, mod3, norm_g, w_in_bf, cos_rep, sin_even, sin_odd, gaq, gak, gbq, gbk)


BAND_CHUNKS = 5


def _bias_band_kernel(tab_ref, o_ref, *, tq):
    h = pl.program_id(0)
    width = BAND_CHUNKS * tq
    base = min(LANES, tq)
    strip = width + tq - base
    qq = lax.broadcasted_iota(jnp.int32, (base, strip), 0)
    kk = lax.broadcasted_iota(jnp.int32, (base, strip), 1)
    d = kk - (tq - base) - (BAND_CHUNKS // 2) * tq - qq
    n = jnp.abs(d)
    n2 = n * n
    large = jnp.full_like(n, 8)
    for j in range(1, 8):
        large = large + (n2 >= 64 * 2 ** j).astype(jnp.int32)
    bucket = jnp.where(n < 8, n, large) + jnp.where(d > 0, 16, 0)
    acc = jnp.zeros((base, strip), F32)
    for b in range(REL_BUCKETS):
        acc = jnp.where(bucket == b, tab_ref[b, h], acc)
    acc = acc * LOG2E
    for k in range(tq // base):
        start = tq - base - base * k
        o_ref[0, k * base:(k + 1) * base, :] = acc[:, start:start + width]


def _bias_band(rel_table, tq):
    return pl.pallas_call(
        functools.partial(_bias_band_kernel, tq=tq),
        grid=(B_HEADS,),
        in_specs=[pl.BlockSpec(memory_space=pltpu.SMEM)],
        out_specs=pl.BlockSpec((1, tq, BAND_CHUNKS * tq), lambda h: (h, 0, 0)),
        out_shape=jax.ShapeDtypeStruct((B_HEADS, tq, BAND_CHUNKS * tq), F32),
        compiler_params=_cparams(("arbitrary",)),
        name="bias_band",
    )(rel_table)


STAT_ROWS = 32
NORM_SLACK = 1.02


def _qk_stats_kernel(p_ref, o_ref):
    r = lax.broadcasted_iota(jnp.int32, (LANES, LANES), 0)
    c = lax.broadcasted_iota(jnp.int32, (LANES, LANES), 1)
    same_half = ((r < B_QK_DIM) == (c < B_QK_DIM)).astype(BF16)

    @pl.when(pl.program_id(1) == 0)
    def _():
        o_ref[...] = jnp.zeros_like(o_ref)

    for slab in range(OFF_VB // LANES):
        t = p_ref[:, slab * LANES:(slab + 1) * LANES].astype(F32)
        half_norm2 = jnp.dot((t * t).astype(BF16), same_half, preferred_element_type=F32)
        top = jnp.max(half_norm2, axis=0, keepdims=True)
        o_ref[0, slab:slab + 1, :] = jnp.maximum(o_ref[0, slab:slab + 1, :], top)


def _qk_stats(proj, batch, seq):
    tm = min(PROJ_TM, seq)
    per_b = seq // tm
    return pl.pallas_call(
        _qk_stats_kernel,
        grid=(batch, per_b),
        in_specs=[pl.BlockSpec((tm, IN_W), lambda b, j: (b * per_b + j, 0))],
        out_specs=pl.BlockSpec((1, STAT_ROWS, LANES), lambda b, j: (b, 0, 0)),
        out_shape=jax.ShapeDtypeStruct((batch, STAT_ROWS, LANES), F32),
        compiler_params=_cparams(("arbitrary", "arbitrary")),
        name="qk_stats",
    )(proj)


def _softmax_offsets(stats, rel_table):
    lo = stats[:, :, 0] * NORM_SLACK
    hi = stats[:, :, B_QK_DIM] * NORM_SLACK
    batch = stats.shape[0]
    s_ka, s_qb, s_kb, s_vb = (OFF_KA // LANES, OFF_QB // LANES, OFF_KB // LANES, OFF_VB // LANES)
    qa = jnp.max((lo + hi)[:, :s_ka].reshape(batch, A_KV_HEADS, A_GROUP), axis=-1)
    ka = (lo + hi)[:, s_ka:s_ka + A_KV_HEADS]
    off_a = jnp.sqrt(qa * ka)
    safe_a = 2.0 * off_a <= SAFE_SPAN
    bound1 = jnp.sqrt(lo[:, s_qb:s_kb] * lo[:, s_kb:s_vb])
    bound2 = jnp.sqrt(hi[:, s_qb:s_kb] * hi[:, s_kb:s_vb])
    bias_hi = jnp.max(rel_table, axis=0) * LOG2E
    bias_lo = jnp.min(rel_table, axis=0) * LOG2E
    off_b = jnp.stack([bound1 + bias_hi, bound2 + bias_hi], axis=-1)
    safe_b = 2.0 * jnp.maximum(bound1, bound2) + (bias_hi - bias_lo) <= SAFE_SPAN
    return (off_a.reshape(-1), safe_a.reshape(-1).astype(jnp.int32),
            off_b.reshape(-1), safe_b.reshape(-1).astype(jnp.int32))


def _flash_step(s, shift, v_c, m_ref, l_ref, acc_ref, idx):
    tk = s.shape[1]
    m_prev = m_ref[idx]
    m_cur = jnp.max(s, axis=1, keepdims=True) + shift
    m_new = jnp.maximum(m_prev, m_cur)
    alpha = jnp.exp2(m_prev - m_new)
    off = m_new - shift
    p = jnp.exp2(s - jnp.tile(off, (1, tk // LANES)))
    l_ref[idx] = alpha * l_ref[idx] + jnp.sum(p, axis=1, keepdims=True)
    acc_ref[idx] = alpha * acc_ref[idx] + jnp.dot(
        p.astype(BF16), v_c, preferred_element_type=F32)
    m_ref[idx] = m_new


_NT = (((1,), (1,)), ((), ()))
SAFE_SPAN = 100.0


def _extend_values(v_ref, vext_ref):
    vext_ref[:, :HEAD_DIM] = v_ref[...]
    vext_ref[:, HEAD_DIM:] = jnp.ones((v_ref.shape[0], HEAD_DIM), BF16)


def _fixed_offset_pass(q_rows, off, k_ref, vext_ref, accx_ref, tk, bias_of):
    offb = off
    accx_ref[...] = jnp.zeros(accx_ref.shape, F32)

    def chunk(c, carry):
        r0 = pl.multiple_of(c * tk, tk)
        s = lax.dot_general(q_rows, k_ref[pl.ds(r0, tk), :], _NT, preferred_element_type=F32)
        e = s - jnp.tile(offb, (1, tk // LANES))
        if bias_of is not None:
            e = e + bias_of(c)
        p = jnp.exp2(e).astype(BF16)
        accx_ref[...] += jnp.dot(p, vext_ref[pl.ds(r0, tk), :], preferred_element_type=F32)
        return carry

    lax.fori_loop(0, k_ref.shape[0] // tk, chunk, 0, unroll=True)


def _attn_a_kernel(off_ref, safe_ref, q_ref, k_ref, v_ref, o_ref, m_ref, l_ref, acc_ref,
                   vext_ref, accx_ref, *, tk):
    tq = q_ref.shape[0]
    seq = k_ref.shape[0]
    group = pl.program_id(0) * pl.num_programs(1) + pl.program_id(1)

    @pl.when(pl.program_id(2) == 0)
    def _():
        _extend_values(v_ref, vext_ref)

    q_rows = jnp.concatenate(
        [q_ref[:, g * HEAD_DIM:(g + 1) * HEAD_DIM] for g in range(A_GROUP)], axis=0)
    safe = safe_ref[group] == 1

    @pl.when(safe)
    def _():
        bound = jnp.full((A_GROUP * tq, LANES), off_ref[group], F32)
        _fixed_offset_pass(q_rows, bound, k_ref, vext_ref, accx_ref, tk, None)
        for g in range(A_GROUP):
            blk = accx_ref[g * tq:(g + 1) * tq, :]
            o_ref[:, g * HEAD_DIM:(g + 1) * HEAD_DIM] = (
                blk[:, :HEAD_DIM] / blk[:, HEAD_DIM:]).astype(BF16)

    @pl.when(jnp.logical_not(safe))
    def _():
        m_ref[...] = jnp.full(m_ref.shape, -jnp.inf, F32)
        l_ref[...] = jnp.zeros(l_ref.shape, F32)
        acc_ref[...] = jnp.zeros(acc_ref.shape, F32)

        def chunk(c, carry):
            r0 = pl.multiple_of(c * tk, tk)
            k_c = k_ref[pl.ds(r0, tk), :]
            v_c = v_ref[pl.ds(r0, tk), :]
            for g in range(A_GROUP):
                q = q_ref[:, g * HEAD_DIM:(g + 1) * HEAD_DIM]
                s = lax.dot_general(q, k_c, _NT, preferred_element_type=F32)
                _flash_step(s, 0.0, v_c, m_ref, l_ref, acc_ref, g)
            return carry

        lax.fori_loop(0, seq // tk, chunk, 0)
        for g in range(A_GROUP):
            o_ref[:, g * HEAD_DIM:(g + 1) * HEAD_DIM] = (acc_ref[g] / l_ref[g]).astype(BF16)


def _attn_a(off, safe, proj, batch, seq):
    t = proj.shape[0]
    tq = min(ATT_TQ, seq)
    tk = min(ATT_A_TK, seq)
    nq = seq // tq
    gw = A_GROUP * HEAD_DIM
    grid_spec = pltpu.PrefetchScalarGridSpec(
        num_scalar_prefetch=2,
        grid=(batch, A_KV_HEADS, nq),
        in_specs=[pl.BlockSpec((tq, gw), lambda b, g, i, *_: (b * nq + i, g)),
                  pl.BlockSpec((seq, HEAD_DIM), lambda b, g, i, *_: (b, OFF_KA // HEAD_DIM + g)),
                  pl.BlockSpec((seq, HEAD_DIM), lambda b, g, i, *_: (b, OFF_VA // HEAD_DIM + g))],
        out_specs=pl.BlockSpec((tq, gw), lambda b, g, i, *_: (b * nq + i, g)),
        scratch_shapes=[pltpu.VMEM((A_GROUP, tq, LANES), F32),
                        pltpu.VMEM((A_GROUP, tq, LANES), F32),
                        pltpu.VMEM((A_GROUP, tq, HEAD_DIM), F32),
                        pltpu.VMEM((seq, 2 * HEAD_DIM), BF16),
                        pltpu.VMEM((A_GROUP * tq, 2 * HEAD_DIM), F32)],
    )
    return pl.pallas_call(
        functools.partial(_attn_a_kernel, tk=tk),
        grid_spec=grid_spec,
        out_shape=jax.ShapeDtypeStruct((t, A_Q_W), BF16),
        compiler_params=_cparams(("arbitrary", "arbitrary", "arbitrary")),
        name="attn_a",
    )(off, safe, proj, proj, proj)


def _attn_b_kernel(tab_ref, off_ref, safe_ref, q_ref, k_ref, v_ref, band_ref, lq1_ref, lk1_ref,
                   lq2_ref, lk2_ref, sg_ref, o_ref, m_ref, l_ref, acc_ref, vext_ref, accx_ref):
    tq = q_ref.shape[0]
    seq = k_ref.shape[0]
    n_chunks = seq // tq
    mid = BAND_CHUNKS // 2
    h = pl.program_id(1)
    i = pl.program_id(2)
    head = pl.program_id(0) * pl.num_programs(1) + h
    lane = lax.broadcasted_iota(jnp.int32, (1, LANES), 1)
    low_half = lane < B_QK_DIM

    @pl.when(i == 0)
    def _():
        _extend_values(v_ref, vext_ref)

    q = q_ref[...]
    zero = jnp.zeros_like(q)
    q1 = jnp.where(low_half, q, zero)
    q2 = jnp.where(low_half, zero, q)
    safe = safe_ref[head] == 1

    lam1 = jnp.exp(jnp.sum(lq1_ref[...] * lk1_ref[...], axis=-1, keepdims=True))
    lam2 = jnp.exp(jnp.sum(lq2_ref[...] * lk2_ref[...], axis=-1, keepdims=True))
    lam = lam1 - lam2 + LAMBDA_INIT

    def finish(o1, o2):
        o = o1 - lam * o2
        ms = jnp.mean(o * o, axis=-1, keepdims=True)
        o = o * lax.rsqrt(ms + EPS) * sg_ref[...] * (1.0 - LAMBDA_INIT)
        o_ref[...] = o.astype(BF16)

    @pl.when(safe)
    def _():
        def bias_of(c):
            j = jnp.clip(c - i + mid, 0, BAND_CHUNKS - 1)
            bias = band_ref[0, :, pl.ds(pl.multiple_of(j * tq, tq), tq)]
            return jnp.concatenate([bias, bias], axis=0)

        off = jnp.concatenate([jnp.full((tq, LANES), off_ref[2 * head], F32),
                               jnp.full((tq, LANES), off_ref[2 * head + 1], F32)], axis=0)
        _fixed_offset_pass(jnp.concatenate([q1, q2], axis=0), off, k_ref,
                           vext_ref, accx_ref, tq, bias_of)
        a1 = accx_ref[0:tq, :]
        a2 = accx_ref[tq:2 * tq, :]
        finish(a1[:, :HEAD_DIM] / a1[:, HEAD_DIM:], a2[:, :HEAD_DIM] / a2[:, HEAD_DIM:])

    @pl.when(jnp.logical_not(safe))
    def _():
        m_ref[...] = jnp.full(m_ref.shape, -jnp.inf, F32)
        l_ref[...] = jnp.zeros(l_ref.shape, F32)
        acc_ref[...] = jnp.zeros(acc_ref.shape, F32)
        far_left = tab_ref[REL_BUCKETS // 2 - 1, h] * LOG2E
        far_right = tab_ref[REL_BUCKETS - 1, h] * LOG2E

        def step(c, shift, bias):
            r0 = pl.multiple_of(c * tq, tq)
            k_c = k_ref[pl.ds(r0, tq), :]
            v_c = v_ref[pl.ds(r0, tq), :]
            s1 = lax.dot_general(q1, k_c, _NT, preferred_element_type=F32)
            s2 = lax.dot_general(q2, k_c, _NT, preferred_element_type=F32)
            if bias is not None:
                s1 = s1 + bias
                s2 = s2 + bias
            _flash_step(s1, shift, v_c, m_ref, l_ref, acc_ref, 0)
            _flash_step(s2, shift, v_c, m_ref, l_ref, acc_ref, 1)

        def left(c, carry):
            step(c, far_left, None)
            return carry

        def right(c, carry):
            step(c, far_right, None)
            return carry

        lax.fori_loop(0, jnp.maximum(i - 1, 0), left, 0)
        for jj in range(3):
            c = i - 1 + jj

            @pl.when((c >= 0) & (c < n_chunks))
            def _():
                step(c, 0.0, band_ref[0, :, (mid - 1 + jj) * tq:(mid + jj) * tq])

        lax.fori_loop(jnp.minimum(i + 2, n_chunks), n_chunks, right, 0)
        finish(acc_ref[0] / l_ref[0], acc_ref[1] / l_ref[1])


def _attn_b(rel_table, off, safe, proj, band, lq1, lk1, lq2, lk2, subln, batch, seq):
    t = proj.shape[0]
    tq = band.shape[1]
    nq = seq // tq
    small = lambda w: pl.BlockSpec((1, w), lambda b, h, i, *_: (0, 0))
    grid_spec = pltpu.PrefetchScalarGridSpec(
        num_scalar_prefetch=3,
        grid=(batch, B_HEADS, nq),
        in_specs=[pl.BlockSpec((tq, HEAD_DIM), lambda b, h, i, *_: (b * nq + i, OFF_QB // HEAD_DIM + h)),
                  pl.BlockSpec((seq, HEAD_DIM), lambda b, h, i, *_: (b, OFF_KB // HEAD_DIM + h)),
                  pl.BlockSpec((seq, HEAD_DIM), lambda b, h, i, *_: (b, OFF_VB // HEAD_DIM + h)),
                  pl.BlockSpec((1, tq, BAND_CHUNKS * tq), lambda b, h, i, *_: (h, 0, 0)),
                  small(B_QK_DIM), small(B_QK_DIM), small(B_QK_DIM), small(B_QK_DIM),
                  small(HEAD_DIM)],
        out_specs=pl.BlockSpec((tq, HEAD_DIM), lambda b, h, i, *_: (b * nq + i, h)),
        scratch_shapes=[pltpu.VMEM((2, tq, LANES), F32),
                        pltpu.VMEM((2, tq, LANES), F32),
                        pltpu.VMEM((2, tq, HEAD_DIM), F32),
                        pltpu.VMEM((seq, 2 * HEAD_DIM), BF16),
                        pltpu.VMEM((2 * tq, 2 * HEAD_DIM), F32)],
    )
    return pl.pallas_call(
        _attn_b_kernel,
        grid_spec=grid_spec,
        out_shape=jax.ShapeDtypeStruct((t, B_V_W), BF16),
        compiler_params=_cparams(("arbitrary", "arbitrary", "arbitrary")),
        name="attn_b",
    )(rel_table, off, safe, proj, proj, proj, band, lq1, lk1, lq2, lk2, subln)


def _post_attn_kernel(oa_ref, ob_ref, x_ref, mod_ref, woa_ref, wob_ref, ng_ref, wr_ref,
                      br_ref, x1_ref, hp_ref, lg_ref):
    tm = x_ref.shape[0]
    d = x_ref.shape[1]
    for r in range(0, tm, POST_HALF):
        rows = slice(r, r + POST_HALF)
        mix = jnp.dot(oa_ref[rows, :], woa_ref[...], preferred_element_type=F32)
        mix = mix + jnp.dot(ob_ref[rows, :], wob_ref[...], preferred_element_type=F32)
        x1 = x_ref[rows, :] + mod_ref[0, 2:3, :] * mix
        x1_ref[rows, :] = x1
        ms = jnp.mean(x1 * x1, axis=-1, keepdims=True)
        y = x1 * lax.rsqrt(ms + EPS) * ng_ref[...]
        h2 = y * (1.0 + mod_ref[0, 4:5, :]) + mod_ref[0, 3:4, :]
        hp_ref[rows, :] = _pack_bf16_pairs(h2)
        hi = h2.astype(BF16)
        lo = (h2 - hi.astype(F32)).astype(BF16)
        lhs = jnp.concatenate([hi, lo, hi], axis=1)
        lg_ref[rows, :] = jnp.dot(lhs, wr_ref[...], preferred_element_type=F32) + br_ref[...]


def _post_attn(out_a, out_b, x2, mod3, w_out_bf, norm_g, w_router_pad, b_router_pad, seq):
    t, d = x2.shape
    tm = min(POST_TM, seq)
    per_b = seq // tm
    half = w_out_bf.shape[0] // 2
    return pl.pallas_call(
        _post_attn_kernel,
        grid=(t // tm,),
        in_specs=[pl.BlockSpec((tm, half), lambda i: (i, 0)),
                  pl.BlockSpec((tm, half), lambda i: (i, 0)),
                  pl.BlockSpec((tm, d), lambda i: (i, 0)),
                  pl.BlockSpec((1, 6, d), lambda i: (i // per_b, 0, 0)),
                  pl.BlockSpec((half, d), lambda i: (0, 0), pipeline_mode=pl.Buffered(1)),
                  pl.BlockSpec((half, d), lambda i: (1, 0), pipeline_mode=pl.Buffered(1)),
                  pl.BlockSpec((1, d), lambda i: (0, 0)),
                  pl.BlockSpec((3 * d, EXPERT_LANES), lambda i: (0, 0)),
                  pl.BlockSpec((1, EXPERT_LANES), lambda i: (0, 0))],
        out_specs=[pl.BlockSpec((tm, d), lambda i: (i, 0)),
                   pl.BlockSpec((tm, d // 2), lambda i: (i, 0)),
                   pl.BlockSpec((tm, EXPERT_LANES), lambda i: (i, 0))],
        out_shape=[jax.ShapeDtypeStruct((t, d), F32),
                   jax.ShapeDtypeStruct((t, d // 2), jnp.int32),
                   jax.ShapeDtypeStruct((t, EXPERT_LANES), F32)],
        compiler_params=_cparams(("arbitrary",)),
        name="post_attn",
    )(out_a, out_b, x2, mod3, w_out_bf, w_out_bf, norm_g, w_router_pad, b_router_pad)


def _route_kernel(lg_ref, pos_ref, gate_ref, cnt_ref, counts, start, carry):
    phase = pl.program_id(0)
    j = pl.program_id(1)
    tb = lg_ref.shape[0]
    lane_i = lax.broadcasted_iota(jnp.int32, (tb, EXPERT_LANES), 1)
    lane_f = lane_i.astype(F32)

    logit = lg_ref[...]
    vals, hots = [], []
    for _ in range(TOP_K):
        mk = jnp.max(logit, axis=1, keepdims=True)
        idx = jnp.min(jnp.where(logit == mk, lane_f, float(EXPERT_LANES)), axis=1, keepdims=True)
        hot = lane_f == idx
        logit = jnp.where(hot, -jnp.inf, logit)
        vals.append(mk)
        hots.append(hot)
    sel = jnp.zeros((tb, EXPERT_LANES), F32)
    for hot in hots:
        sel = sel + hot.astype(F32)
    col_sum = jnp.sum(sel, axis=0, keepdims=True)

    @pl.when((phase == 0) & (j == 0))
    def _():
        counts[...] = jnp.zeros_like(counts)

    @pl.when(phase == 0)
    def _():
        counts[...] += col_sum

    @pl.when((phase == 1) & (j == 0))
    def _():
        r = lax.broadcasted_iota(jnp.int32, (EXPERT_LANES, EXPERT_LANES), 0)
        c = lax.broadcasted_iota(jnp.int32, (EXPERT_LANES, EXPERT_LANES), 1)
        before = (r < c).astype(F32)
        padded = jnp.floor((counts[...] + (EXP_SUB - 1)) * (1.0 / EXP_SUB)) * EXP_SUB
        start[...] = jnp.dot(padded, before, preferred_element_type=F32, precision=HIGHEST)
        carry[...] = jnp.zeros_like(carry)
        cnt_ref[...] = counts[...].astype(jnp.int32)

    @pl.when(phase == 1)
    def _():
        r = lax.broadcasted_iota(jnp.int32, (tb, tb), 0)
        c = lax.broadcasted_iota(jnp.int32, (tb, tb), 1)
        earlier = (c < r).astype(BF16)
        prefix = jnp.dot(earlier, sel.astype(BF16), preferred_element_type=F32)
        base = prefix + carry[...] + start[...]
        exps = [jnp.exp(v - vals[0]) for v in vals]
        denom = exps[0] + exps[1] + exps[2] + exps[3]
        pos_out = jnp.zeros((tb, EXPERT_LANES), F32)
        gate_out = jnp.zeros((tb, EXPERT_LANES), F32)
        for k in range(TOP_K):
            pos_k = jnp.sum(jnp.where(hots[k], base, 0.0), axis=1, keepdims=True)
            pos_out = jnp.where(lane_i == k, pos_k, pos_out)
            gate_out = jnp.where(lane_i == k, exps[k] / denom, gate_out)
        pos_ref[...] = pos_out.astype(jnp.int32)
        gate_ref[...] = gate_out
        carry[...] += col_sum


def _route(logits):
    t = logits.shape[0]
    tb = min(ROUTE_TB, t)
    return pl.pallas_call(
        _route_kernel,
        grid=(2, t // tb),
        in_specs=[pl.BlockSpec((tb, EXPERT_LANES), lambda p, j: (j, 0))],
        out_specs=[pl.BlockSpec((tb, EXPERT_LANES), lambda p, j: (j * p, 0)),
                   pl.BlockSpec((tb, EXPERT_LANES), lambda p, j: (j * p, 0)),
                   pl.BlockSpec((1, EXPERT_LANES), lambda p, j: (0, 0))],
        out_shape=[jax.ShapeDtypeStruct((t, EXPERT_LANES), jnp.int32),
                   jax.ShapeDtypeStruct((t, EXPERT_LANES), F32),
                   jax.ShapeDtypeStruct((1, EXPERT_LANES), jnp.int32)],
        scratch_shapes=[pltpu.VMEM((1, EXPERT_LANES), F32),
                        pltpu.VMEM((1, EXPERT_LANES), F32),
                        pltpu.VMEM((1, EXPERT_LANES), F32)],
        compiler_params=_cparams(("arbitrary", "arbitrary")),
        name="route",
    )(logits)


def _row_copy(src_ref, src_row, dst_ref, dst_row, sem):
    return pltpu.make_async_copy(src_ref.at[pl.ds(src_row, 1), :],
                                 dst_ref.at[pl.ds(dst_row, 1), :], sem)


def _pack_bf16_pairs(x):
    half = x.shape[1] // 2
    bits = lax.bitcast_convert_type(x, jnp.int32)

    def rounded(b):
        lsb = lax.shift_right_logical(b, 16) & 1
        return b + 0x7FFF + lsb

    lo = lax.shift_right_logical(rounded(bits[:, :half]), 16)
    hi = rounded(bits[:, half:]) & jnp.int32(-65536)
    return lo | hi


def _unpack_bf16_pairs(w):
    lo = lax.bitcast_convert_type(lax.shift_left(w, 16), F32)
    hi = lax.bitcast_convert_type(w & jnp.int32(-65536), F32)
    return jnp.concatenate([lo, hi], axis=1)


def _dispatch_kernel(clr_ref, dst_ref, h_ref, xs_ref, zeros, csem, sem):
    tt = h_ref.shape[0]
    sub = zeros.shape[0]
    n_clear = clr_ref.shape[0]

    @pl.when(pl.program_id(0) == 0)
    def _():
        zeros[...] = jnp.zeros_like(zeros)

        def block_copy(blk):
            rows = pl.ds(pl.multiple_of(blk * sub, sub), sub)
            return pltpu.make_async_copy(zeros, xs_ref.at[rows, :], csem)

        def clear(b, carry):
            @pl.when(clr_ref[b] >= 0)
            def _():
                block_copy(clr_ref[b]).start()

            return carry

        def clear_done(b, carry):
            @pl.when(clr_ref[b] >= 0)
            def _():
                block_copy(0).wait()

            return carry

        lax.fori_loop(0, n_clear, clear, 0)
        lax.fori_loop(0, n_clear, clear_done, 0)

    def issue(r, carry):
        for k in range(TOP_K):
            _row_copy(h_ref, r, xs_ref, dst_ref[TOP_K * r + k], sem).start()
        return carry

    def drain(r, carry):
        for k in range(TOP_K):
            _row_copy(h_ref, 0, xs_ref, 0, sem).wait()
        return carry

    lax.fori_loop(0, tt, issue, 0)
    lax.fori_loop(0, tt, drain, 0, unroll=8)


def _dispatch(clear_blocks, pos_flat, hp, n_rows):
    t, dp = hp.shape
    tt = min(ROW_TT, t)
    grid_spec = pltpu.PrefetchScalarGridSpec(
        num_scalar_prefetch=1,
        grid=(t // tt,),
        in_specs=[pl.BlockSpec((tt * TOP_K,), lambda i, clr: (i,), memory_space=pltpu.SMEM),
                  pl.BlockSpec((tt, dp), lambda i, clr: (i, 0))],
        out_specs=pl.BlockSpec(memory_space=pl.ANY),
        scratch_shapes=[pltpu.VMEM((EXP_SUB, dp), jnp.int32), pltpu.SemaphoreType.DMA(()),
                        pltpu.SemaphoreType.DMA(())],
    )
    return pl.pallas_call(
        _dispatch_kernel,
        grid_spec=grid_spec,
        out_shape=jax.ShapeDtypeStruct((n_rows, dp), jnp.int32),
        compiler_params=_cparams(("arbitrary",)),
        name="dispatch",
    )(clear_blocks, pos_flat, hp)


def _clear_plan(counts, n_rows):
    n_e = counts.shape[0]
    n_blocks = n_rows // EXP_SUB
    pcb = (counts + EXP_SUB - 1) // EXP_SUB
    pend = jnp.cumsum(pcb)
    seg_last = jnp.where(pcb > 0, pend - 1, -1)
    tail = pend[-1] + jnp.arange(n_e + 1, dtype=jnp.int32)
    tail = jnp.where(tail < n_blocks, tail, -1)
    return jnp.concatenate([seg_last, tail]).astype(jnp.int32)


VISIT_IDLE, VISIT_COMPUTE, VISIT_CLEAR = 0, 1, 2


def _experts_kernel(exp_ref, row0_ref, nsub_ref, kind_ref, feff_ref,
                    xs_ref, wg_ref, wu_ref, wd_ref, bg_ref, bu_ref, bd_ref, ys_ref,
                    xbuf, acc, xsem, osem, *, sub, unit):
    v = pl.program_id(0)
    f = pl.program_id(1)
    last_f = pl.num_programs(1) - 1
    row0 = row0_ref[v]
    nsub = nsub_ref[v]
    kind = kind_ref[v]

    def x_copy(j, slot):
        rows = pl.ds(pl.multiple_of(row0 + j * unit, sub), unit)
        return pltpu.make_async_copy(xs_ref.at[rows, :], xbuf.at[slot], xsem.at[slot])

    def y_copy(j, src_row):
        rows = pl.ds(pl.multiple_of(row0 + j * unit, sub), unit)
        return pltpu.make_async_copy(acc.at[pl.ds(src_row, unit), :], ys_ref.at[rows, :], osem)

    def drain(j, carry):
        y_copy(0, 0).wait()
        return carry

    def x0_copy(visit):
        rows = pl.ds(pl.multiple_of(row0_ref[visit], sub), unit)
        return pltpu.make_async_copy(xs_ref.at[rows, :], xbuf.at[0], xsem.at[0])

    def run_visit(first, last):
        def compute(j, slot):
            xb = _unpack_bf16_pairs(xbuf[slot]).astype(BF16)
            g = jnp.dot(xb, wg_ref[0].astype(BF16), preferred_element_type=F32) + bg_ref[0]
            u = jnp.dot(xb, wu_ref[0].astype(BF16), preferred_element_type=F32) + bu_ref[0]
            g = jnp.minimum(g, SWIGLU_LIMIT)
            u = jnp.clip(u, -SWIGLU_LIMIT, SWIGLU_LIMIT)
            glu = g * jax.nn.sigmoid(SWIGLU_ALPHA * g)
            a = ((u + 1.0) * glu).astype(BF16)
            y = jnp.dot(a, wd_ref[0].astype(BF16), preferred_element_type=F32)
            r0 = pl.multiple_of(j * unit, unit)
            if first:
                acc[pl.ds(r0, unit), :] = y + bd_ref[0]
            else:
                acc[pl.ds(r0, unit), :] += y
            if last:
                y_copy(j, r0).start(priority=ROW_DMA_PRIORITY)

        if first:
            @pl.when(v == 0)
            def _():
                x0_copy(0).start(priority=ROW_DMA_PRIORITY)

        more = nsub > 1

        @pl.when(more)
        def _():
            x_copy(1, 1).start(priority=ROW_DMA_PRIORITY)

        if first:
            x0_copy(v).wait()
        compute(0, 0)
        if last:
            nxt = jnp.minimum(v + 1, pl.num_programs(0) - 1)

            @pl.when(kind_ref[nxt] == VISIT_COMPUTE)
            def _():
                x0_copy(nxt).start(priority=ROW_DMA_PRIORITY)

        @pl.when(more)
        def _():
            def later_unit(j, carry):
                slot = 1 + (j - 1) % 2
                x_copy(j, slot).wait()
                x_copy(jnp.minimum(j + 1, nsub - 1), 3 - slot).start(priority=ROW_DMA_PRIORITY)
                compute(j, slot)
                return carry

            lax.fori_loop(1, nsub, later_unit, 0)
            x_copy(0, 1 + (nsub - 1) % 2).wait()

        if last:
            lax.fori_loop(0, nsub, drain, 0)

    active = (kind == VISIT_COMPUTE) & (nsub > 0)
    pl.when(active & (f == 0))(lambda: run_visit(True, False))
    pl.when(active & (f > 0) & (f < last_f))(lambda: run_visit(False, False))
    pl.when(active & (f == last_f))(lambda: run_visit(False, True))

    @pl.when((kind == VISIT_CLEAR) & (f == 0) & (nsub > 0))
    def _():
        acc[pl.ds(0, sub), :] = jnp.zeros((sub, acc.shape[1]), F32)

        def clear_copy(j):
            rows = pl.ds(pl.multiple_of(row0 + j * sub, sub), sub)
            return pltpu.make_async_copy(acc.at[pl.ds(0, sub), :], ys_ref.at[rows, :], osem)

        def clear(j, carry):
            clear_copy(j).start()
            return carry

        def clear_done(j, carry):
            clear_copy(0).wait()
            return carry

        lax.fori_loop(0, nsub, clear, 0)
        lax.fori_loop(0, nsub, clear_done, 0)


def _experts(meta, xs, w_gu, b_gu3, w_down, b_down3):
    n_rows, dp = xs.shape
    n_e, d, two_f = w_gu.shape
    ff = two_f // 2
    tf = EXP_TF
    nf = ff // tf
    n_visits = meta[0].shape[0]
    wspec = lambda shape, imap: pl.BlockSpec(shape, imap)
    grid_spec = pltpu.PrefetchScalarGridSpec(
        num_scalar_prefetch=5,
        grid=(n_visits, nf),
        in_specs=[
            pl.BlockSpec(memory_space=pl.ANY),
            wspec((1, d, tf), lambda v, f, ex, r0, ns, kd, fe: (ex[v], 0, fe[v * nf + f])),
            wspec((1, d, tf), lambda v, f, ex, r0, ns, kd, fe: (ex[v], 0, nf + fe[v * nf + f])),
            wspec((1, tf, d), lambda v, f, ex, r0, ns, kd, fe: (ex[v], fe[v * nf + f], 0)),
            wspec((1, 1, tf), lambda v, f, ex, r0, ns, kd, fe: (ex[v], 0, fe[v * nf + f])),
            wspec((1, 1, tf), lambda v, f, ex, r0, ns, kd, fe: (ex[v], 0, nf + fe[v * nf + f])),
            wspec((1, 1, d), lambda v, f, ex, r0, ns, kd, fe: (ex[v], 0, 0)),
        ],
        out_specs=pl.BlockSpec(memory_space=pl.ANY),
        scratch_shapes=[pltpu.VMEM((3, EXP_UNIT, dp), jnp.int32),
                        pltpu.VMEM((EXP_MAX_UNITS * EXP_UNIT, d), F32),
                        pltpu.SemaphoreType.DMA((3,)),
                        pltpu.SemaphoreType.DMA(())],
    )
    return pl.pallas_call(
        functools.partial(_experts_kernel, sub=EXP_SUB, unit=EXP_UNIT),
        grid_spec=grid_spec,
        out_shape=jax.ShapeDtypeStruct((n_rows, d), F32),
        compiler_params=_cparams(("arbitrary", "arbitrary"), vmem=EXPERTS_VMEM_LIMIT),
        name="experts",
    )(*meta, xs, w_gu, w_gu, w_down, b_gu3, b_gu3, b_down3)


def _visit_plan(counts, n_rows, nf):
    n_e = counts.shape[0]
    sub = EXP_SUB
    tmx = EXP_MAX_UNITS * EXP_UNIT
    pc = (counts + sub - 1) // sub * sub
    pend = jnp.cumsum(pc)
    pstart = pend - pc
    nvis = (pc + tmx - 1) // tmx
    vend = jnp.cumsum(nvis)
    vstart = vend - nvis
    total = vend[-1]
    n_visits = n_rows // tmx + n_e + 1
    v = jnp.arange(n_visits, dtype=jnp.int32)
    compute = v < total
    v_clamped = jnp.minimum(v, total - 1)
    e_v = jnp.minimum(jnp.sum(vend[None, :] <= v_clamped[:, None], axis=1), n_e - 1).astype(jnp.int32)
    part = v - vstart[e_v]
    row0 = jnp.where(compute, pstart[e_v] + part * tmx, 0)
    nsub = jnp.where(compute, (jnp.minimum(pc[e_v] - part * tmx, tmx) + EXP_UNIT - 1) // EXP_UNIT, 0)
    clear = v == total
    row0 = jnp.where(clear, jnp.minimum(pend[-1], n_rows - sub), row0).astype(jnp.int32)
    nsub = jnp.where(clear, (n_rows - pend[-1]) // sub, nsub).astype(jnp.int32)
    kind = jnp.where(compute, VISIT_COMPUTE, jnp.where(clear, VISIT_CLEAR, VISIT_IDLE))
    f = jnp.arange(nf, dtype=jnp.int32)
    feff = jnp.where(compute[:, None], f[None, :], nf - 1).reshape(-1).astype(jnp.int32)
    return e_v, row0, nsub, kind.astype(jnp.int32), feff


def _combine_kernel(pos_ref, ys_ref, gate_ref, x1_ref, mod_ref, o_ref, rows, sem):
    tt = x1_ref.shape[0]

    def issue(r, carry):
        for k in range(TOP_K):
            pltpu.make_async_copy(ys_ref.at[pl.ds(pos_ref[TOP_K * r + k], 1), :],
                                  rows.at[k, pl.ds(r, 1), :], sem).start()
        return carry

    def drain(r, carry):
        for k in range(TOP_K):
            pltpu.make_async_copy(ys_ref.at[pl.ds(0, 1), :],
                                  rows.at[0, pl.ds(0, 1), :], sem).wait()
        return carry

    lax.fori_loop(0, tt, issue, 0)
    lax.fori_loop(0, tt, drain, 0, unroll=8)
    gate = gate_ref[...]
    y = gate[:, 0:1] * rows[0]
    for k in range(1, TOP_K):
        y = y + gate[:, k:k + 1] * rows[k]
    o_ref[...] = x1_ref[...] + mod_ref[0, 5:6, :] * y


def _combine(pos_flat, ys, gates, x1, mod3, seq):
    t, d = x1.shape
    tt = min(ROW_TT, seq)
    per_b = seq // tt
    return pl.pallas_call(
        _combine_kernel,
        grid=(t // tt,),
        in_specs=[pl.BlockSpec((tt * TOP_K,), lambda i: (i,), memory_space=pltpu.SMEM),
                  pl.BlockSpec(memory_space=pl.ANY),
                  pl.BlockSpec((tt, EXPERT_LANES), lambda i: (i, 0)),
                  pl.BlockSpec((tt, d), lambda i: (i, 0)),
                  pl.BlockSpec((1, 6, d), lambda i: (i // per_b, 0, 0))],
        out_specs=pl.BlockSpec((tt, d), lambda i: (i, 0)),
        out_shape=jax.ShapeDtypeStruct((t, d), F32),
        scratch_shapes=[pltpu.VMEM((TOP_K, tt, d), F32), pltpu.SemaphoreType.DMA(())],
        compiler_params=_cparams(("arbitrary",)),
        name="combine",
    )(pos_flat, ys, gates, x1, mod3)


def _rope_tables(seq):
    rows = seq // GRID_W
    row = jnp.repeat(jnp.arange(rows, dtype=F32), GRID_W)
    col = jnp.tile(jnp.arange(GRID_W, dtype=F32), rows)
    axis_dim = HEAD_DIM // 2
    inv = ROPE_THETA ** (-jnp.arange(0, axis_dim, 2, dtype=F32) / axis_dim)
    ang = jnp.concatenate([row[:, None] * inv, col[:, None] * inv], axis=-1)
    cos_rep = jnp.repeat(jnp.cos(ang), 2, axis=-1)
    sin_rep = jnp.repeat(jnp.sin(ang), 2, axis=-1)
    even = (jnp.arange(HEAD_DIM) % 2 == 0)[None, :]
    return cos_rep, jnp.where(even, -sin_rep, 0.0), jnp.where(even, 0.0, sin_rep)


def _layer(x, c, rel_table, w_ada, b_ada, norm_attn, norm_ffn, w_in, w_out, a_q_norm,
           a_k_norm, b_q_norm, b_k_norm, lq1, lk1, lq2, lk2, b_subln, w_router, b_router,
           w_gu, b_gu, w_down, b_down):
    batch, seq, d = x.shape
    t = batch * seq
    x2 = x.reshape(t, d)

    c_pad = jnp.pad(c, ((0, 8 - batch % 8 if batch % 8 else 0), (0, 0)))
    mod = _ada_mod(c_pad, w_ada, b_ada.reshape(1, -1))
    mod3 = mod[:batch].reshape(batch, 6, d)

    cos_rep, sin_even, sin_odd = _rope_tables(seq)
    row = lambda p: p.reshape(1, -1)
    proj = _in_proj(x2, mod3, row(norm_attn), w_in.astype(BF16), cos_rep, sin_even, sin_odd,
                    row(a_q_norm), row(a_k_norm), row(jnp.tile(b_q_norm, 2)),
                    row(jnp.tile(b_k_norm, 2)), seq)

    band = _bias_band(rel_table, min(ATT_B_TQ, seq))
    off_a, safe_a, off_b, safe_b = _softmax_offsets(_qk_stats(proj, batch, seq), rel_table)
    out_a = _attn_a(off_a, safe_a, proj, batch, seq)
    out_b = _attn_b(rel_table, off_b, safe_b, proj, band, row(lq1), row(lk1), row(lq2), row(lk2),
                    row(b_subln), batch, seq)

    n_e = w_router.shape[1]
    w_router_pad = jnp.pad(w_router, ((0, 0), (0, EXPERT_LANES - n_e)))
    wr_hi = w_router_pad.astype(BF16)
    wr_lo = (w_router_pad - wr_hi.astype(F32)).astype(BF16)
    wr_split = jnp.concatenate([wr_hi, wr_hi, wr_lo], axis=0)
    b_router_pad = jnp.pad(b_router, (0, EXPERT_LANES - n_e), constant_values=NEG_BIG)
    x1, h2, logits = _post_attn(out_a, out_b, x2, mod3, w_out.astype(BF16), row(norm_ffn),
                                wr_split, row(b_router_pad), seq)

    pos, gates, counts = _route(logits)
    pos_flat = pos[:, :TOP_K].reshape(-1)
    cnt = counts[0, :n_e]
    n_rows = t * TOP_K + (n_e + 1) * EXP_SUB
    xs = _dispatch(_clear_plan(cnt, n_rows), pos_flat, h2, n_rows)

    meta = _visit_plan(cnt, n_rows, (w_gu.shape[2] // 2) // EXP_TF)
    ys = _experts(meta, xs, w_gu, b_gu.reshape(n_e, 1, -1), w_down, b_down.reshape(n_e, 1, -1))

    out = _combine(pos_flat, ys, gates, x1, mod3, seq)
    return out.reshape(batch, seq, d)


def kernel(x, c, rel_bias_table, w_ada, b_ada, norm_attn, norm_ffn, w_in, w_out, a_q_norm,
           a_k_norm, b_q_norm, b_k_norm, lambda_q1, lambda_k1, lambda_q2, lambda_k2, b_subln,
           w_router, b_router, w_gu, b_gu, w_down, b_down):
    return _layer(x, c, rel_bias_table, w_ada[0], b_ada[0], norm_attn[0], norm_ffn[0],
                  w_in[0], w_out[0], a_q_norm[0], a_k_norm[0], b_q_norm[0], b_k_norm[0],
                  lambda_q1[0], lambda_k1[0], lambda_q2[0], lambda_k2[0], b_subln[0],
                  w_router[0], b_router[0], w_gu[0], b_gu[0], w_down[0], b_down[0])
```
